```python
import math
import jax, jax.numpy as jnp
from jax import lax
import numpy as np

D_MODEL = 4096
BATCH = 1
SEQ = 8192
DEPTH = 1
DEC_BATCH = 16
DEC_SEQ = 16
PAST_LEN = 4096

CHUNK = 64
N_META = 16
D_SSM = D_MODEL // 2
SSM_GROUP = 16
N_SSM_GROUPS = D_SSM // SSM_GROUP
SSM_STATE = 64
N_HEADS = 16
HEAD_DIM = 64
V_DIM = 2 * HEAD_DIM
D_ATT = N_HEADS * V_DIM
IN_SPLITS = [D_SSM, D_SSM + D_ATT, D_SSM + 2 * D_ATT, D_SSM + 3 * D_ATT,
             D_SSM + 3 * D_ATT + D_MODEL]
IN_WIDTH = D_SSM + 3 * D_ATT + 2 * D_MODEL
ROPE_THETA = 10000.0
N_EGROUPS = 8
EXPERTS_PER_GROUP = 8
N_EXPERTS = N_EGROUPS * EXPERTS_PER_GROUP
TOP_K = 2
D_EXPERT = 512
MOE_BLOCK = 128
Q_BLOCK = 128
EPS = 1e-6
DT_MIN = 1e-3
DT_MAX = 1e-1

kernel_name = 'hybrid_s5_diffattn_hmoe_stream_step'


def rmsnorm(x, g):
    xf = x.astype(jnp.float32)
    y = xf * lax.rsqrt(jnp.mean(xf * xf, axis=-1, keepdims=True) + EPS)
    return (y * g.astype(jnp.float32)).astype(x.dtype)


def rope(x, pos):
    half = HEAD_DIM // 2
    inv = ROPE_THETA ** (-jnp.arange(half, dtype=jnp.float32) / half)
    ang = pos.astype(jnp.float32)[:, None] * inv[None, :]
    cos = jnp.cos(ang)[None, :, None, None, :]
    sin = jnp.sin(ang)[None, :, None, None, :]
    xf = x.astype(jnp.float32)
    x1, x2 = xf[..., :half], xf[..., half:]
    return jnp.concatenate([x1 * cos - x2 * sin, x2 * cos + x1 * sin], -1).astype(x.dtype)


def s5_scan(u, h0_re, h0_im, a_re, a_im, log_dt, b_re, b_im, c_re, c_im, d_skip):
    f32 = jnp.float32
    bsz, L, _ = u.shape
    uf = u.astype(f32)
    ug = uf.reshape(bsz, L, N_SSM_GROUPS, SSM_GROUP)
    a_re = a_re.astype(f32)
    a_im = a_im.astype(f32)
    dt = jnp.exp(log_dt.astype(f32))[:, None]
    mag = jnp.exp(a_re * dt)
    abar_re = mag * jnp.cos(a_im * dt)
    abar_im = mag * jnp.sin(a_im * dt)
    nr, ni = abar_re - 1.0, abar_im
    den = a_re * a_re + a_im * a_im
    coef_re = (nr * a_re + ni * a_im) / den
    coef_im = (ni * a_re - nr * a_im) / den
    br, bi = b_re.astype(f32), b_im.astype(f32)
    bbar_re = coef_re[..., None] * br - coef_im[..., None] * bi
    bbar_im = coef_re[..., None] * bi + coef_im[..., None] * br
    bu_re = jnp.einsum('gpc,blgc->blgp', bbar_re, ug)
    bu_im = jnp.einsum('gpc,blgc->blgp', bbar_im, ug)
    h0r, h0i = h0_re.astype(f32), h0_im.astype(f32)
    bu_re = bu_re.at[:, 0].add(abar_re * h0r - abar_im * h0i)
    bu_im = bu_im.at[:, 0].add(abar_re * h0i + abar_im * h0r)
    ar = jnp.broadcast_to(abar_re, bu_re.shape)
    ai = jnp.broadcast_to(abar_im, bu_im.shape)

    def combine(e1, e2):
        a1r, a1i, b1r, b1i = e1
        a2r, a2i, b2r, b2i = e2
        return (a2r * a1r - a2i * a1i, a2r * a1i + a2i * a1r,
                a2r * b1r - a2i * b1i + b2r, a2r * b1i + a2i * b1r + b2i)

    _, _, hr, hi = lax.associative_scan(combine, (ar, ai, bu_re, bu_im), axis=1)
    y = (jnp.einsum('gcp,blgp->blgc', c_re.astype(f32), hr)
         - jnp.einsum('gcp,blgp->blgc', c_im.astype(f32), hi))
    y = y.reshape(bsz, L, D_SSM) + d_skip.astype(f32) * uf
    return y.astype(u.dtype), hr[:, -1], hi[:, -1]


def diff_attend(q, k, v, lam, mask):
    s = jnp.einsum('bqhcd,bkhcd->bhcqk', q, k).astype(jnp.float32) * (HEAD_DIM ** -0.5)
    if mask is not None:
        s = jnp.where(mask, s, -jnp.inf)
    p = jax.nn.softmax(s, axis=-1)
    w = p[:, :, 0] - lam * p[:, :, 1]
    return jnp.einsum('bhqk,bkhv->bqhv', w.astype(v.dtype), v)


def diff_attention_prompt(q, k, v, cid, lam):
    bsz, L = q.shape[0], q.shape[1]
    n_blk = -(-L // Q_BLOCK)
    Lp = n_blk * Q_BLOCK
    qp = jnp.pad(q, ((0, 0), (0, Lp - L), (0, 0), (0, 0), (0, 0)))
    qcid = jnp.pad(cid, (0, Lp - L), constant_values=2 ** 30)
    qb = qp.reshape(bsz, n_blk, Q_BLOCK, N_HEADS, 2, HEAD_DIM).swapaxes(0, 1)
    cb = qcid.reshape(n_blk, Q_BLOCK)

    def block(args):
        qi, ci = args
        mask = cid[None, :] <= ci[:, None]
        return diff_attend(qi, k, v, lam, mask)

    o = lax.map(block, (qb, cb))
    return o.swapaxes(0, 1).reshape(bsz, Lp, N_HEADS, V_DIM)[:, :L]


def grouped_expert_mlp(xf, expert, gate, w1, w3, w2):
    T, D = xf.shape
    S = T * TOP_K
    n_blocks = -(-(S + N_EXPERTS * (MOE_BLOCK - 1)) // MOE_BLOCK)
    P = n_blocks * MOE_BLOCK
    flat_e = expert.reshape(-1).astype(jnp.int32)
    flat_t = jnp.repeat(jnp.arange(T, dtype=jnp.int32), TOP_K)
    flat_g = gate.reshape(-1).astype(jnp.float32)
    order = jnp.argsort(flat_e)
    se = flat_e[order]
    counts = jnp.bincount(flat_e, length=N_EXPERTS)
    padded = (counts + MOE_BLOCK - 1) // MOE_BLOCK * MOE_BLOCK
    pad_end = jnp.cumsum(padded)
    pad_start = pad_end - padded
    start = jnp.cumsum(counts) - counts
    dest = pad_start[se] + jnp.arange(S, dtype=jnp.int32) - start[se]
    slot_tok = jnp.full((P,), T, jnp.int32).at[dest].set(flat_t[order])
    slot_gate = jnp.zeros((P,), jnp.float32).at[dest].set(flat_g[order])
    block_start = jnp.arange(n_blocks, dtype=jnp.int32) * MOE_BLOCK
    block_expert = jnp.minimum(jnp.searchsorted(pad_end, block_start, side='right'),
                               N_EXPERTS - 1).astype(jnp.int32)
    x_pad = jnp.concatenate([xf, jnp.zeros((1, D), xf.dtype)], 0)
    xb = x_pad[slot_tok].reshape(n_blocks, MOE_BLOCK, D)

    def run(args):
        xi, e = args
        hid = jax.nn.silu(xi @ w1[e]) * (xi @ w3[e])
        return hid @ w2[e]

    yb = lax.map(run, (xb, block_expert)).reshape(P, D)
    out = jnp.zeros((T + 1, D), jnp.float32).at[slot_tok].add(
        yb.astype(jnp.float32) * slot_gate[:, None])
    return out[:T]


def hier_moe(h, w_rg, b_rg, w_re, b_re, w1, w3, w2):
    bsz, L, D = h.shape
    xf = h.reshape(-1, D)
    T = xf.shape[0]
    g_prob = jax.nn.softmax((xf @ w_rg + b_rg).astype(jnp.float32), axis=-1)
    g_w, g_idx = lax.top_k(g_prob, 1)
    e_logits = (xf @ w_re + b_re).astype(jnp.float32).reshape(T, N_EGROUPS, EXPERTS_PER_GROUP)
    e_logits = jnp.take_along_axis(e_logits, g_idx[:, :, None], axis=1)[:, 0]
    e_prob = jax.nn.softmax(e_logits, axis=-1)
    e_w, e_idx = lax.top_k(e_prob, TOP_K)
    e_w = e_w / jnp.sum(e_w, axis=-1, keepdims=True)
    gate = g_w * e_w
    expert = g_idx * EXPERTS_PER_GROUP + e_idx
    out = grouped_expert_mlp(xf, expert, gate, w1, w3, w2)
    return out.reshape(bsz, L, D).astype(h.dtype)


def trunk_layer(x, positions, chunk_ids, past_k, past_v, h0_re, h0_im, p, lam_init):
    bsz, L, _ = x.shape
    h = rmsnorm(x, p['norm1_g'])
    z = h @ p['w_in'] + p['b_in']
    u, q, k, v, g_s, g_a = jnp.split(z, IN_SPLITS, axis=-1)
    ys, h_re, h_im = s5_scan(u, h0_re, h0_im, p['ssm_a_re'], p['ssm_a_im'], p['ssm_log_dt'],
                             p['ssm_b_re'], p['ssm_b_im'], p['ssm_c_re'], p['ssm_c_im'], p['ssm_d'])
    ys = jax.nn.gelu(ys)
    ys = ys * jax.nn.sigmoid(ys @ p['w_glu'] + p['b_glu'])
    ys = ys @ p['w_ssm_proj']
    q = q.reshape(bsz, L, N_HEADS, 2, HEAD_DIM)
    k = k.reshape(bsz, L, N_HEADS, 2, HEAD_DIM)
    v = v.reshape(bsz, L, N_HEADS, V_DIM)
    q = rope(rmsnorm(q, p['q_norm_g']), positions)
    k = rope(rmsnorm(k, p['k_norm_g']), positions)
    f32 = jnp.float32
    lam = (jnp.exp(jnp.sum(p['lam_q1'].astype(f32) * p['lam_k1'].astype(f32)))
           - jnp.exp(jnp.sum(p['lam_q2'].astype(f32) * p['lam_k2'].astype(f32))) + lam_init)
    if chunk_ids is not None:
        o = diff_attention_prompt(q, k, v, chunk_ids, lam)
    else:
        k_all = jnp.concatenate([past_k.astype(k.dtype), k], axis=1)
        v_all = jnp.concatenate([past_v.astype(v.dtype), v], axis=1)
        o = diff_attend(q, k_all, v_all, lam, None)
    o = rmsnorm(o, p['subln_g']) * (1.0 - lam_init)
    ya = o.reshape(bsz, L, D_ATT) @ p['w_att_proj']
    m = jax.nn.sigmoid(g_s) * ys + jax.nn.sigmoid(g_a) * ya
    x = x + m @ p['w_o']
    h2 = rmsnorm(x, p['norm2_g'])
    x = x + hier_moe(h2, p['w_router_group'], p['b_router_group'], p['w_router_expert'],
                     p['b_router_expert'], p['w1_e'], p['w3_e'], p['w2_e'])
    return x, k, v, h_re, h_im


def setup_inputs(seed: int = 0) -> dict:
    key = jax.random.key(seed)
    ks = iter(jax.random.split(key, 48))
    f32 = jnp.float32

    def nrm(shape, scale):
        return scale * jax.random.normal(next(ks), shape, f32)

    G, P = N_SSM_GROUPS, SSM_STATE
    x_prompt = nrm((BATCH, SEQ, D_MODEL), 1.0)
    x_sample = nrm((DEC_BATCH, DEC_SEQ, D_MODEL), 1.0)
    cache_k = nrm((DEPTH, DEC_BATCH, PAST_LEN, N_HEADS, 2, HEAD_DIM), 1.0)
    cache_v = nrm((DEPTH, DEC_BATCH, PAST_LEN, N_HEADS, V_DIM), 1.0)
    state_ssm_re = nrm((DEPTH, DEC_BATCH, G, P), 0.5)
    state_ssm_im = nrm((DEPTH, DEC_BATCH, G, P), 0.5)
    meta_tokens = nrm((N_META, D_MODEL), 1.0)
    norm1_g = 1.0 + nrm((DEPTH, D_MODEL), 0.02)
    w_in = nrm((DEPTH, D_MODEL, IN_WIDTH), D_MODEL ** -0.5)
    b_in = nrm((DEPTH, IN_WIDTH), 0.01)
    ssm_a_re = -0.5 + nrm((DEPTH, G, P), 0.01)
    ssm_a_im = math.pi * jnp.arange(P, dtype=f32)[None, None, :] + nrm((DEPTH, G, P), 0.01)
    ssm_log_dt = jax.random.uniform(next(ks), (DEPTH, G), f32, math.log(DT_MIN), math.log(DT_MAX))
    ssm_b_re = nrm((DEPTH, G, P, SSM_GROUP), (2 * SSM_GROUP) ** -0.5)
    ssm_b_im = nrm((DEPTH, G, P, SSM_GROUP), (2 * SSM_GROUP) ** -0.5)
    ssm_c_re = nrm((DEPTH, G, SSM_GROUP, P), P ** -0.5)
    ssm_c_im = nrm((DEPTH, G, SSM_GROUP, P), P ** -0.5)
    ssm_d = nrm((DEPTH, D_SSM), 0.5)
    w_glu = nrm((DEPTH, D_SSM, D_SSM), D_SSM ** -0.5)
    b_glu = nrm((DEPTH, D_SSM), 0.01)
    w_ssm_proj = nrm((DEPTH, D_SSM, D_MODEL), D_SSM ** -0.5)
    q_norm_g = 1.0 + nrm((DEPTH, HEAD_DIM), 0.02)
    k_norm_g = 1.0 + nrm((DEPTH, HEAD_DIM), 0.02)
    lam_q1 = nrm((DEPTH, HEAD_DIM), 0.1)
    lam_k1 = nrm((DEPTH, HEAD_DIM), 0.1)
    lam_q2 = nrm((DEPTH, HEAD_DIM), 0.1)
    lam_k2 = nrm((DEPTH, HEAD_DIM), 0.1)
    subln_g = 1.0 + nrm((DEPTH, V_DIM), 0.02)
    w_att_proj = nrm((DEPTH, D_ATT, D_MODEL), D_ATT ** -0.5)
    w_o = nrm((DEPTH, D_MODEL, D_MODEL), D_MODEL ** -0.5)
    norm2_g = 1.0 + nrm((DEPTH, D_MODEL), 0.02)
    w_router_group = nrm((DEPTH, D_MODEL, N_EGROUPS), D_MODEL ** -0.5)
    b_router_group = nrm((DEPTH, N_EGROUPS), 0.01)
    w_router_expert = nrm((DEPTH, D_MODEL, N_EXPERTS), D_MODEL ** -0.5)
    b_router_expert = nrm((DEPTH, N_EXPERTS), 0.01)
    w1_e = nrm((DEPTH, N_EXPERTS, D_MODEL, D_EXPERT), D_MODEL ** -0.5)
    w3_e = nrm((DEPTH, N_EXPERTS, D_MODEL, D_EXPERT), D_MODEL ** -0.5)
    w2_e = nrm((DEPTH, N_EXPERTS, D_EXPERT, D_MODEL), D_EXPERT ** -0.5)
    return {'x_prompt': x_prompt, 'x_sample': x_sample, 'cache_k': cache_k, 'cache_v': cache_v,
            'state_ssm_re': state_ssm_re, 'state_ssm_im': state_ssm_im,
            'meta_tokens': meta_tokens, 'norm1_g': norm1_g, 'w_in': w_in, 'b_in': b_in,
            'ssm_a_re': ssm_a_re, 'ssm_a_im': ssm_a_im, 'ssm_log_dt': ssm_log_dt,
            'ssm_b_re': ssm_b_re, 'ssm_b_im': ssm_b_im, 'ssm_c_re': ssm_c_re, 'ssm_c_im': ssm_c_im,
            'ssm_d': ssm_d, 'w_glu': w_glu, 'b_glu': b_glu, 'w_ssm_proj': w_ssm_proj,
            'q_norm_g': q_norm_g, 'k_norm_g': k_norm_g, 'lam_q1': lam_q1, 'lam_k1': lam_k1,
            'lam_q2': lam_q2, 'lam_k2': lam_k2, 'subln_g': subln_g, 'w_att_proj': w_att_proj,
            'w_o': w_o, 'norm2_g': norm2_g, 'w_router_group': w_router_group,
            'b_router_group': b_router_group, 'w_router_expert': w_router_expert,
            'b_router_expert': b_router_expert, 'w1_e': w1_e, 'w3_e': w3_e, 'w2_e': w2_e}


def reference(x_prompt, x_sample, cache_k, cache_v, state_ssm_re, state_ssm_im,
              meta_tokens, norm1_g, w_in, b_in, ssm_a_re, ssm_a_im, ssm_log_dt,
              ssm_b_re, ssm_b_im, ssm_c_re, ssm_c_im, ssm_d, w_glu, b_glu, w_ssm_proj,
              q_norm_g, k_norm_g, lam_q1, lam_k1, lam_q2, lam_k2, subln_g, w_att_proj,
              w_o, norm2_g, w_router_group, b_router_group, w_router_expert, b_router_expert,
              w1_e, w3_e, w2_e):
    bsz = x_prompt.shape[0]
    Lp = N_META + x_prompt.shape[1]
    pos_p = jnp.arange(Lp, dtype=jnp.int32)
    cid_p = jnp.where(pos_p < N_META, 0, (pos_p - N_META) // CHUNK + 1)
    past_len = cache_k.shape[2]
    pos_s = past_len + jnp.arange(x_sample.shape[1], dtype=jnp.int32)
    meta = jnp.broadcast_to(meta_tokens.astype(x_prompt.dtype)[None], (bsz, N_META, D_MODEL))
    xp = jnp.concatenate([meta, x_prompt], axis=1)
    xs = x_sample
    zero_h = jnp.zeros((bsz, N_SSM_GROUPS, SSM_STATE), jnp.float32)
    kp_l, vp_l, hrp_l, hip_l = [], [], [], []
    ks_l, vs_l, hrs_l, his_l = [], [], [], []
    for d in range(DEPTH):
        lp = dict(norm1_g=norm1_g[d], w_in=w_in[d], b_in=b_in[d],
                  ssm_a_re=ssm_a_re[d], ssm_a_im=ssm_a_im[d], ssm_log_dt=ssm_log_dt[d],
                  ssm_b_re=ssm_b_re[d], ssm_b_im=ssm_b_im[d], ssm_c_re=ssm_c_re[d],
                  ssm_c_im=ssm_c_im[d], ssm_d=ssm_d[d], w_glu=w_glu[d], b_glu=b_glu[d],
                  w_ssm_proj=w_ssm_proj[d], q_norm_g=q_norm_g[d], k_norm_g=k_norm_g[d],
                  lam_q1=lam_q1[d], lam_k1=lam_k1[d], lam_q2=lam_q2[d], lam_k2=lam_k2[d],
                  subln_g=subln_g[d], w_att_proj=w_att_proj[d], w_o=w_o[d], norm2_g=norm2_g[d],
                  w_router_group=w_router_group[d], b_router_group=b_router_group[d],
                  w_router_expert=w_router_expert[d], b_router_expert=b_router_expert[d],
                  w1_e=w1_e[d], w3_e=w3_e[d], w2_e=w2_e[d])
        lam_init = 0.8 - 0.6 * math.exp(-0.3 * d)
        xp, kp, vp, hrp, hip = trunk_layer(xp, pos_p, cid_p, None, None, zero_h, zero_h, lp, lam_init)
        xs, ksn, vsn, hrs, his = trunk_layer(xs, pos_s, None, cache_k[d], cache_v[d],
                                             state_ssm_re[d], state_ssm_im[d], lp, lam_init)
        kp_l.append(kp); vp_l.append(vp); hrp_l.append(hrp); hip_l.append(hip)
        ks_l.append(ksn); vs_l.append(vsn); hrs_l.append(hrs); his_l.append(his)
    y_prompt = xp[:, N_META:]
    y_sample = xs
    k_prompt = jnp.stack(kp_l)
    v_prompt = jnp.stack(vp_l)
    ssm_re_prompt = jnp.stack(hrp_l)
    ssm_im_prompt = jnp.stack(hip_l)
    k_sample = jnp.stack(ks_l)
    v_sample = jnp.stack(vs_l)
    ssm_re_sample = jnp.stack(hrs_l)
    ssm_im_sample = jnp.stack(his_l)
    return (y_prompt, y_sample, k_prompt, v_prompt, ssm_re_prompt, ssm_im_prompt,
            k_sample, v_sample, ssm_re_sample, ssm_im_sample)
```

```python
import functools
import math

import jax
import jax.numpy as jnp
from jax import lax
from jax.experimental import pallas as pl
from jax.experimental.pallas import tpu as pltpu

F32 = jnp.float32
BF16 = jnp.bfloat16

D_MODEL = 4096
N_META = 16
CHUNK = 64
N_HEADS = 16
HEAD_DIM = 64
V_DIM = 128
D_ATT = N_HEADS * V_DIM
D_SSM = 2048
SSM_GROUP = 16
N_SSM_GROUPS = D_SSM // SSM_GROUP
SSM_STATE = 64
IN_WIDTH = D_SSM + 3 * D_ATT + 2 * D_MODEL
ROPE_THETA = 10000.0
N_EGROUPS = 8
EXPERTS_PER_GROUP = 8
N_EXPERTS = N_EGROUPS * EXPERTS_PER_GROUP
TOP_K = 2
D_EXPERT = 512
MOE_BLOCK = 128
EPS = 1e-6

LANES = 128
ROW_ALIGN = 512
S5_CHUNK = 8
S5_LANE_GROUPS = LANES // SSM_GROUP
S5_TILES = D_SSM // LANES
VMEM_LIMIT = 56 * 1024 * 1024


def _cparams(sem, vmem=VMEM_LIMIT):
    return pltpu.CompilerParams(dimension_semantics=sem, vmem_limit_bytes=vmem)


def _row_tile(tp, cap):
    best = 16
    for t in range(16, cap + 1, 16):
        if tp % t == 0:
            best = t
    return best


def _dot(a, b):
    return jnp.dot(a, b, preferred_element_type=F32)


def _dot_nt(a, b):
    return lax.dot_general(a, b, (((1,), (1,)), ((), ())), preferred_element_type=F32)


def _rmsnorm_kernel(x_ref, g_ref, o_ref):
    x = x_ref[...]
    ms = jnp.mean(x * x, axis=-1, keepdims=True)
    o_ref[...] = (x * lax.rsqrt(ms + EPS) * g_ref[...]).astype(o_ref.dtype)


def _rmsnorm(x, g, tr):
    tp, d = x.shape
    return pl.pallas_call(
        _rmsnorm_kernel,
        grid=(tp // tr,),
        in_specs=[pl.BlockSpec((tr, d), lambda i: (i, 0)),
                  pl.BlockSpec((1, d), lambda i: (0, 0))],
        out_specs=pl.BlockSpec((tr, d), lambda i: (i, 0)),
        out_shape=jax.ShapeDtypeStruct((tp, d), BF16),
        compiler_params=_cparams(("parallel",)),
        name="rmsnorm1",
    )(x, g.reshape(1, d))


def _inproj_kernel(x_ref, w_ref, b_ref, o_ref):
    o_ref[...] = _dot(x_ref[...], w_ref[...].astype(BF16)) + b_ref[...]


def _inproj(h, w, b, tm, tn):
    tp, k = h.shape
    n = w.shape[1]
    return pl.pallas_call(
        _inproj_kernel,
        grid=(n // tn, tp // tm),
        in_specs=[pl.BlockSpec((tm, k), lambda j, i: (i, 0)),
                  pl.BlockSpec((k, tn), lambda j, i: (0, j)),
                  pl.BlockSpec((1, tn), lambda j, i: (0, j))],
        out_specs=pl.BlockSpec((tm, tn), lambda j, i: (i, j)),
        out_shape=jax.ShapeDtypeStruct((tp, n), F32),
        compiler_params=_cparams(("parallel", "parallel")),
        name="in_proj",
    )(h, w, b.reshape(1, n))


def _glu_kernel(x_ref, w_ref, b_ref, xe_ref, o_ref):
    a = _dot(x_ref[...].astype(BF16), w_ref[...].astype(BF16)) + b_ref[...]
    o_ref[...] = (xe_ref[...] * jax.nn.sigmoid(a)).astype(o_ref.dtype)


def _glu(ys, w, b, tm, tn):
    tp, k = ys.shape
    n = w.shape[1]
    return pl.pallas_call(
        _glu_kernel,
        grid=(n // tn, tp // tm),
        in_specs=[pl.BlockSpec((tm, k), lambda j, i: (i, 0)),
                  pl.BlockSpec((k, tn), lambda j, i: (0, j)),
                  pl.BlockSpec((1, tn), lambda j, i: (0, j)),
                  pl.BlockSpec((tm, tn), lambda j, i: (i, j))],
        out_specs=pl.BlockSpec((tm, tn), lambda j, i: (i, j)),
        out_shape=jax.ShapeDtypeStruct((tp, n), BF16),
        compiler_params=_cparams(("parallel", "parallel")),
        name="glu",
    )(ys, w, b.reshape(1, n), ys)


def _merge_kernel(ys_ref, oa_ref, ws_ref, wa_ref, gs_ref, ga_ref, o_ref):
    a = _dot(ys_ref[...], ws_ref[...].astype(BF16))
    b = _dot(oa_ref[...], wa_ref[...].astype(BF16))
    m = jax.nn.sigmoid(gs_ref[...]) * a + jax.nn.sigmoid(ga_ref[...]) * b
    o_ref[...] = m.astype(o_ref.dtype)


def _merge(ysg, oatt, w_ssm, w_att, z, tm, tn):
    tp, k = ysg.shape
    n = w_ssm.shape[1]
    gs_blk = (D_SSM + 3 * D_ATT) // tn
    ga_blk = (D_SSM + 3 * D_ATT + D_MODEL) // tn
    return pl.pallas_call(
        _merge_kernel,
        grid=(n // tn, tp // tm),
        in_specs=[pl.BlockSpec((tm, k), lambda j, i: (i, 0)),
                  pl.BlockSpec((tm, k), lambda j, i: (i, 0)),
                  pl.BlockSpec((k, tn), lambda j, i: (0, j)),
                  pl.BlockSpec((k, tn), lambda j, i: (0, j)),
                  pl.BlockSpec((tm, tn), lambda j, i: (i, gs_blk + j)),
                  pl.BlockSpec((tm, tn), lambda j, i: (i, ga_blk + j))],
        out_specs=pl.BlockSpec((tm, tn), lambda j, i: (i, j)),
        out_shape=jax.ShapeDtypeStruct((tp, n), BF16),
        compiler_params=_cparams(("parallel", "parallel")),
        name="merge_proj",
    )(ysg, oatt, w_ssm, w_att, z, z)


def _outproj_kernel(m_ref, w_ref, x_ref, o_ref):
    o_ref[...] = x_ref[...] + _dot(m_ref[...], w_ref[...].astype(BF16))


def _outproj(m, w, x, tm, tn):
    tp, k = m.shape
    n = w.shape[1]
    return pl.pallas_call(
        _outproj_kernel,
        grid=(n // tn, tp // tm),
        in_specs=[pl.BlockSpec((tm, k), lambda j, i: (i, 0)),
                  pl.BlockSpec((k, tn), lambda j, i: (0, j)),
                  pl.BlockSpec((tm, tn), lambda j, i: (i, j))],
        out_specs=pl.BlockSpec((tm, tn), lambda j, i: (i, j)),
        out_shape=jax.ShapeDtypeStruct((tp, n), F32),
        compiler_params=_cparams(("parallel", "parallel")),
        name="out_proj",
    )(m, w, x)


def _segment_sumsq(x, ones_bd):
    x2 = x * x
    hi = x2.astype(BF16)
    lo = (x2 - hi.astype(F32)).astype(BF16)
    return _dot(hi, ones_bd) + _dot(lo, ones_bd)


def _qk_rope_kernel(zq_ref, zk_ref, zv_ref, cos_ref, sin_ref, gq_ref, gk_ref, ones_ref,
                    qb_ref, kf_ref, kb_ref, vf_ref, vb_ref):
    cos = cos_ref[...]
    sin = sin_ref[...]
    ones_bd = ones_ref[...]
    lane = lax.broadcasted_iota(jnp.int32, cos.shape, 1)
    first_half = (lane % HEAD_DIM) < (HEAD_DIM // 2)

    def norm_rope(x, g):
        ss = _segment_sumsq(x, ones_bd)
        xn = x * lax.rsqrt(ss * (1.0 / HEAD_DIM) + EPS) * g
        partner = jnp.where(first_half,
                            pltpu.roll(xn, LANES - HEAD_DIM // 2, 1),
                            pltpu.roll(xn, HEAD_DIM // 2, 1))
        return xn * cos + partner * sin

    for h in range(N_HEADS):
        sl = slice(h * LANES, (h + 1) * LANES)
        q = norm_rope(zq_ref[:, sl], gq_ref[...])
        qb_ref[:, sl] = (q * (HEAD_DIM ** -0.5)).astype(BF16)
        k = norm_rope(zk_ref[:, sl], gk_ref[...])
        kf_ref[:, sl] = k
        kb_ref[:, sl] = k.astype(BF16)
    v = zv_ref[...]
    vf_ref[...] = v
    vb_ref[...] = v.astype(BF16)


def _qk_rope(z, cos_t, sin_t, gq, gk, tr):
    tp = z.shape[0]
    ones_bd = jnp.kron(jnp.eye(LANES // HEAD_DIM, dtype=F32),
                       jnp.ones((HEAD_DIM, HEAD_DIM), F32)).astype(BF16)
    zspec = lambda c: pl.BlockSpec((tr, D_ATT), lambda i: (i, c))
    row = pl.BlockSpec((tr, LANES), lambda i: (i, 0))
    const = pl.BlockSpec((1, LANES), lambda i: (0, 0))
    out = pl.BlockSpec((tr, D_ATT), lambda i: (i, 0))
    q_blk = D_SSM // D_ATT
    return pl.pallas_call(
        _qk_rope_kernel,
        grid=(tp // tr,),
        in_specs=[zspec(q_blk), zspec(q_blk + 1), zspec(q_blk + 2), row, row, const, const,
                  pl.BlockSpec((LANES, LANES), lambda i: (0, 0))],
        out_specs=[out, out, out, out, out],
        out_shape=[jax.ShapeDtypeStruct((tp, D_ATT), BF16),
                   jax.ShapeDtypeStruct((tp, D_ATT), F32),
                   jax.ShapeDtypeStruct((tp, D_ATT), BF16),
                   jax.ShapeDtypeStruct((tp, D_ATT), F32),
                   jax.ShapeDtypeStruct((tp, D_ATT), BF16)],
        compiler_params=_cparams(("parallel",)),
        name="qk_norm_rope",
    )(z, z, z, cos_t, sin_t, gq, gk, ones_bd)


def _softmax_step(c, qc, kt, vt, mask, m_ref, l_ref, acc_ref):
    s = _dot_nt(qc, kt)
    if mask is not None:
        s = jnp.where(mask, s, -jnp.inf)
    m_prev = m_ref[c]
    m_new = jnp.maximum(m_prev, jnp.max(s, axis=1, keepdims=True))
    alpha = jnp.exp(m_prev - m_new)
    p = jnp.exp(s - m_new[:, :1])
    l_ref[c] = alpha * l_ref[c] + jnp.sum(p, axis=1, keepdims=True)
    acc_ref[c] = alpha * acc_ref[c] + _dot(p.astype(BF16), vt)
    m_ref[c] = m_new


def _diff_finish(o0, o1, lam, g, out_scale):
    o = o0 - lam * o1
    ms = jnp.mean(o * o, axis=-1, keepdims=True)
    return o * lax.rsqrt(ms + EPS) * g * out_scale


def _split_components(q):
    lane = lax.broadcasted_iota(jnp.int32, q.shape, 1)
    zero = jnp.zeros_like(q)
    return jnp.where(lane < HEAD_DIM, q, zero), jnp.where(lane >= HEAD_DIM, q, zero)


def _attn_prompt_kernel(lam_ref, g_ref, q_ref, k_ref, v_ref, o_ref, m_ref, l_ref, acc_ref,
                        *, tq, nq_main, seq, out_scale):
    i = pl.program_id(1)
    q0, q1 = _split_components(q_ref[...])
    m_ref[...] = jnp.full(m_ref.shape, -jnp.inf, F32)
    l_ref[...] = jnp.zeros(l_ref.shape, F32)
    acc_ref[...] = jnp.zeros(acc_ref.shape, F32)

    def update(kt, vt, mask):
        _softmax_step(0, q0, kt, vt, mask, m_ref, l_ref, acc_ref)
        _softmax_step(1, q1, kt, vt, mask, m_ref, l_ref, acc_ref)

    col = lax.broadcasted_iota(jnp.int32, (tq, LANES), 1)
    update(k_ref[pl.ds(seq, LANES), :], v_ref[pl.ds(seq, LANES), :], col < N_META)

    is_main = i < nq_main

    def body(j, carry):
        start = pl.multiple_of(j * tq, tq)
        update(k_ref[pl.ds(start, tq), :], v_ref[pl.ds(start, tq), :], None)
        return carry

    lax.fori_loop(0, jnp.where(is_main, i, 0), body, 0)

    @pl.when(is_main)
    def _():
        start = pl.multiple_of(i * tq, tq)
        r = lax.broadcasted_iota(jnp.int32, (tq, tq), 0) // CHUNK
        c = lax.broadcasted_iota(jnp.int32, (tq, tq), 1) // CHUNK
        update(k_ref[pl.ds(start, tq), :], v_ref[pl.ds(start, tq), :], c <= r)

    o0 = acc_ref[0] / l_ref[0]
    o1 = acc_ref[1] / l_ref[1]
    o_ref[...] = _diff_finish(o0, o1, lam_ref[...], g_ref[...], out_scale).astype(o_ref.dtype)


def _attn_prompt(qb, kb, vb, lam_row, subln_g, seq, out_scale, tq):
    tp = qb.shape[0]
    kern = functools.partial(_attn_prompt_kernel, tq=tq, nq_main=seq // tq, seq=seq,
                             out_scale=out_scale)
    const = pl.BlockSpec((1, LANES), lambda h, i: (0, 0))
    return pl.pallas_call(
        kern,
        grid=(N_HEADS, tp // tq),
        in_specs=[const, const,
                  pl.BlockSpec((tq, LANES), lambda h, i: (i, h)),
                  pl.BlockSpec((tp, LANES), lambda h, i: (0, h)),
                  pl.BlockSpec((tp, LANES), lambda h, i: (0, h))],
        out_specs=pl.BlockSpec((tq, LANES), lambda h, i: (i, h)),
        out_shape=jax.ShapeDtypeStruct((tp, D_ATT), BF16),
        scratch_shapes=[pltpu.VMEM((2, tq, LANES), F32),
                        pltpu.VMEM((2, tq, LANES), F32),
                        pltpu.VMEM((2, tq, LANES), F32)],
        compiler_params=_cparams(("parallel", "parallel")),
        name="attn_prompt",
    )(lam_row, subln_g, qb, kb, vb)


def _attn_sample_kernel(lam_ref, g_ref, q_ref, kc_ref, vc_ref, kn_ref, vn_ref, o_in_ref, o_ref,
                        m_ref, l_ref, acc_ref, *, nq, out_scale):
    del o_in_ref
    j = pl.program_id(1)
    rows = 2 * nq

    @pl.when(j == 0)
    def _():
        m_ref[...] = jnp.full(m_ref.shape, -jnp.inf, F32)
        l_ref[...] = jnp.zeros(l_ref.shape, F32)
        acc_ref[...] = jnp.zeros(acc_ref.shape, F32)
        col = lax.broadcasted_iota(jnp.int32, (rows, LANES), 1)
        for h in range(N_HEADS):
            sl = slice(h * LANES, (h + 1) * LANES)
            _softmax_step(h, q_ref[h], kn_ref[:, sl], vn_ref[:, sl], col < nq,
                          m_ref, l_ref, acc_ref)

    for h in range(N_HEADS):
        sl = slice(h * LANES, (h + 1) * LANES)
        _softmax_step(h, q_ref[h], kc_ref[:, sl].astype(BF16), vc_ref[:, sl].astype(BF16), None,
                      m_ref, l_ref, acc_ref)

    @pl.when(j == pl.num_programs(1) - 1)
    def _():
        for h in range(N_HEADS):
            o = acc_ref[h] / l_ref[h]
            res = _diff_finish(o[:nq], o[nq:], lam_ref[...], g_ref[...], out_scale)
            o_ref[:, h * LANES:(h + 1) * LANES] = res.astype(o_ref.dtype)


def _attn_sample(qz, cache_k, cache_v, k_new, v_new, o_buf, lam_row, subln_g, row_off, out_scale, tk):
    nb, past = cache_k.shape[0], cache_k.shape[1]
    nq = qz.shape[2] // 2
    kern = functools.partial(_attn_sample_kernel, nq=nq, out_scale=out_scale)
    const = pl.BlockSpec((1, LANES), lambda b, j: (0, 0))
    blk_off = row_off // nq
    return pl.pallas_call(
        kern,
        grid=(nb, past // tk),
        in_specs=[const, const,
                  pl.BlockSpec((None, N_HEADS, 2 * nq, LANES), lambda b, j: (b, 0, 0, 0)),
                  pl.BlockSpec((None, tk, D_ATT), lambda b, j: (b, j, 0)),
                  pl.BlockSpec((None, tk, D_ATT), lambda b, j: (b, j, 0)),
                  pl.BlockSpec((None, LANES, D_ATT), lambda b, j: (b, 0, 0)),
                  pl.BlockSpec((None, LANES, D_ATT), lambda b, j: (b, 0, 0)),
                  pl.BlockSpec(memory_space=pl.ANY)],
        out_specs=pl.BlockSpec((nq, D_ATT), lambda b, j: (blk_off + b, 0)),
        out_shape=jax.ShapeDtypeStruct(o_buf.shape, o_buf.dtype),
        scratch_shapes=[pltpu.VMEM((N_HEADS, 2 * nq, LANES), F32),
                        pltpu.VMEM((N_HEADS, 2 * nq, LANES), F32),
                        pltpu.VMEM((N_HEADS, 2 * nq, LANES), F32)],
        input_output_aliases={7: 0},
        compiler_params=_cparams(("parallel", "arbitrary")),
        name="attn_sample",
    )(lam_row, subln_g, qz, cache_k, cache_v, k_new, v_new, o_buf)


def _s5_weights(a_re, a_im, log_dt, b_re, b_im, c_re, c_im):
    hp = lax.Precision.HIGHEST
    n_t, gl, tc = S5_TILES, S5_LANE_GROUPS, S5_CHUNK
    dt = jnp.exp(log_dt)[:, None]
    mag = jnp.exp(a_re * dt)
    abar_re = mag * jnp.cos(a_im * dt)
    abar_im = mag * jnp.sin(a_im * dt)
    nr, ni = abar_re - 1.0, abar_im
    den = a_re * a_re + a_im * a_im
    coef_re = (nr * a_re + ni * a_im) / den
    coef_im = (ni * a_re - nr * a_im) / den
    bbar_re = coef_re[..., None] * b_re - coef_im[..., None] * b_im
    bbar_im = coef_re[..., None] * b_im + coef_im[..., None] * b_re
    n = jnp.arange(tc + 1, dtype=F32)[:, None, None]
    pw_mag = jnp.exp(n * (a_re * dt))
    pw_re = pw_mag * jnp.cos(n * (a_im * dt))
    pw_im = pw_mag * jnp.sin(n * (a_im * dt))
    e_re = pw_re[:tc, :, :, None] * bbar_re - pw_im[:tc, :, :, None] * bbar_im
    e_im = pw_re[:tc, :, :, None] * bbar_im + pw_im[:tc, :, :, None] * bbar_re
    kern = (jnp.einsum('gcp,lgpd->glcd', c_re, e_re, precision=hp)
            - jnp.einsum('gcp,lgpd->glcd', c_im, e_im, precision=hp))
    s_idx = jnp.arange(tc)[:, None]
    t_idx = jnp.arange(tc)[None, :]
    lag = jnp.clip(t_idx - s_idx, 0, tc - 1)
    toe = kern[:, lag] * (t_idx >= s_idx)[None, :, :, None, None].astype(F32)
    toe = toe.reshape(n_t, gl, tc, tc, SSM_GROUP, SSM_GROUP)
    eye = jnp.eye(gl, dtype=F32)
    w_intra = jnp.einsum('jgstcd,gh->jsgdthc', toe, eye)
    w_intra = w_intra.reshape(n_t, tc * LANES, tc * LANES)
    eb = jnp.stack([e_re[::-1], e_im[::-1]], 0)
    eb = eb.reshape(2, tc, n_t, gl, SSM_STATE, SSM_GROUP)
    w_state = jnp.einsum('rsjgpd,gh->jsgdrhp', eb, eye)
    w_state = w_state.reshape(n_t, tc * LANES, 2 * gl * SSM_STATE)
    cp_re = c_re[None] * pw_re[1:, :, None, :] - c_im[None] * pw_im[1:, :, None, :]
    cp_im = c_re[None] * pw_im[1:, :, None, :] + c_im[None] * pw_re[1:, :, None, :]
    cp = jnp.stack([cp_re, -cp_im], 0).reshape(2, tc, n_t, gl, SSM_GROUP, SSM_STATE)
    w_read = jnp.einsum('rtjgcp,gh->jrgpthc', cp, eye)
    w_read = w_read.reshape(n_t, 2 * gl * SSM_STATE, tc * LANES)
    half = gl * SSM_STATE
    a_pow = jnp.concatenate([pw_re[tc].reshape(n_t, 1, half), pw_im[tc].reshape(n_t, 1, half)], -1)
    return w_intra.astype(BF16), w_state.astype(BF16), w_read.astype(BF16), a_pow


def _s5_kernel(u_ref, wi_ref, ws_ref, wr_ref, ap_ref, d_ref, h0r_ref, h0i_ref,
               y_ref, hpr_ref, hpi_ref, hsr_ref, hsi_ref,
               y_acc, v_ref, hs_ref, *, nc, n_main, n_meta_chunks, n_seq, seq_chunks):
    tc = S5_CHUNK
    half = hs_ref.shape[1] // 2
    lhs = jnp.concatenate(
        [u_ref[pl.ds(s, nc, stride=tc), :].astype(BF16) for s in range(tc)], axis=1)
    y_acc[...] = _dot(lhs, wi_ref[...])
    v_ref[...] = _dot(lhs, ws_ref[...])
    a_re = ap_ref[:, :half]
    a_im = ap_ref[:, half:]

    def advance(h_re, h_im, v):
        return (a_re * h_re - a_im * h_im + v[:, :half],
                a_re * h_im + a_im * h_re + v[:, half:])

    hs_ref[...] = jnp.zeros(hs_ref.shape, F32)

    h_re = jnp.zeros((1, half), F32)
    h_im = jnp.zeros((1, half), F32)
    for c in range(n_main, n_main + n_meta_chunks):
        hs_ref[pl.ds(c, 1), :] = jnp.concatenate([h_re, h_im], axis=1)
        h_re, h_im = advance(h_re, h_im, v_ref[pl.ds(c, 1), :])

    def body(c, carry):
        h_re, h_im = carry
        hs_ref[pl.ds(c, 1), :] = jnp.concatenate([h_re, h_im], axis=1)
        return advance(h_re, h_im, v_ref[pl.ds(c, 1), :])

    h_re, h_im = lax.fori_loop(0, n_main, body, (h_re, h_im))
    hpr_ref[...] = h_re
    hpi_ref[...] = h_im

    base = n_main + n_meta_chunks
    for b in range(n_seq):
        s_re = h0r_ref[pl.ds(b, 1), :]
        s_im = h0i_ref[pl.ds(b, 1), :]
        for c in range(base + b * seq_chunks, base + (b + 1) * seq_chunks):
            hs_ref[pl.ds(c, 1), :] = jnp.concatenate([s_re, s_im], axis=1)
            s_re, s_im = advance(s_re, s_im, v_ref[pl.ds(c, 1), :])
        hsr_ref[pl.ds(b, 1), :] = s_re
        hsi_ref[pl.ds(b, 1), :] = s_im

    y_acc[...] += _dot(hs_ref[...].astype(BF16), wr_ref[...])
    d = d_ref[...]
    for t in range(tc):
        rows = pl.ds(t, nc, stride=tc)
        y = y_acc[:, t * LANES:(t + 1) * LANES] + d * u_ref[rows, :]
        y_ref[rows, :] = jax.nn.gelu(y)


def _s5(z, weights, d_skip, h0_re, h0_im, seq, n_seq, seq_len):
    tp = z.shape[0]
    tc = S5_CHUNK
    nc = tp // tc
    w_intra, w_state, w_read, a_pow = weights
    half = S5_LANE_GROUPS * SSM_STATE
    kern = functools.partial(_s5_kernel, nc=nc, n_main=seq // tc, n_meta_chunks=N_META // tc,
                             n_seq=n_seq, seq_chunks=seq_len // tc)
    wspec = lambda a: pl.BlockSpec((None,) + a.shape[1:], lambda j: (j, 0, 0))
    col = pl.BlockSpec((tp, LANES), lambda j: (0, j))
    st = lambda r: pl.BlockSpec((r, half), lambda j: (0, j))
    gp = N_SSM_GROUPS * SSM_STATE
    return pl.pallas_call(
        kern,
        grid=(S5_TILES,),
        in_specs=[col, wspec(w_intra), wspec(w_state), wspec(w_read), wspec(a_pow),
                  pl.BlockSpec((1, LANES), lambda j: (0, j)), st(n_seq), st(n_seq)],
        out_specs=[col, st(1), st(1), st(n_seq), st(n_seq)],
        out_shape=[jax.ShapeDtypeStruct((tp, D_SSM), F32),
                   jax.ShapeDtypeStruct((1, gp), F32),
                   jax.ShapeDtypeStruct((1, gp), F32),
                   jax.ShapeDtypeStruct((n_seq, gp), F32),
                   jax.ShapeDtypeStruct((n_seq, gp), F32)],
        scratch_shapes=[pltpu.VMEM((nc, tc * LANES), F32),
                        pltpu.VMEM((nc, 2 * half), F32),
                        pltpu.VMEM((nc, 2 * half), F32)],
        compiler_params=_cparams(("parallel",)),
        name="s5_scan",
    )(z, w_intra, w_state, w_read, a_pow, d_skip.reshape(1, D_SSM), h0_re, h0_im)


def _router_kernel(x_ref, g_ref, w_ref, b_ref, h_ref, e_ref, gate_ref):
    x = x_ref[...]
    ms = jnp.mean(x * x, axis=-1, keepdims=True)
    h = x * lax.rsqrt(ms + EPS) * g_ref[...]
    h_ref[...] = h.astype(h_ref.dtype)
    logits = _dot(h.astype(BF16), w_ref[...]) + b_ref[...]
    lane = lax.broadcasted_iota(jnp.int32, logits.shape, 1)
    neg = -jnp.inf
    big = jnp.int32(LANES)

    def first_argmax(vals, vmax):
        return jnp.min(jnp.where(vals == vmax, lane, big), axis=1, keepdims=True)

    lg = jnp.where(lane < N_EGROUPS, logits, neg)
    mg = jnp.max(lg, axis=1, keepdims=True)
    sg = jnp.sum(jnp.exp(lg - mg), axis=1, keepdims=True)
    g_w = 1.0 / sg
    g_idx = first_argmax(lg, mg)
    lo = N_EGROUPS + EXPERTS_PER_GROUP * g_idx
    le = jnp.where((lane >= lo) & (lane < lo + EXPERTS_PER_GROUP), logits, neg)
    m1 = jnp.max(le, axis=1, keepdims=True)
    se = jnp.sum(jnp.exp(le - m1), axis=1, keepdims=True)
    i1 = first_argmax(le, m1)
    le2 = jnp.where(lane == i1, neg, le)
    m2 = jnp.max(le2, axis=1, keepdims=True)
    i2 = first_argmax(le2, m2)
    p1 = 1.0 / se
    p2 = jnp.exp(m2 - m1) / se
    tot = p1 + p2
    w1 = g_w * (p1 / tot)
    w2 = g_w * (p2 / tot)
    e_ref[...] = jnp.where(lane == 0, i1 - N_EGROUPS, jnp.where(lane == 1, i2 - N_EGROUPS, 0))
    gate_ref[...] = jnp.where(lane == 0, w1, jnp.where(lane == 1, w2, 0.0))


def _router(x2, g, w_r, b_r, tr):
    tp, d = x2.shape
    return pl.pallas_call(
        _router_kernel,
        grid=(tp // tr,),
        in_specs=[pl.BlockSpec((tr, d), lambda i: (i, 0)),
                  pl.BlockSpec((1, d), lambda i: (0, 0)),
                  pl.BlockSpec((d, LANES), lambda i: (0, 0)),
                  pl.BlockSpec((1, LANES), lambda i: (0, 0))],
        out_specs=[pl.BlockSpec((tr, d), lambda i: (i, 0)),
                   pl.BlockSpec((tr, LANES), lambda i: (i, 0)),
                   pl.BlockSpec((tr, LANES), lambda i: (i, 0))],
        out_shape=[jax.ShapeDtypeStruct((tp, d), BF16),
                   jax.ShapeDtypeStruct((tp, LANES), jnp.int32),
                   jax.ShapeDtypeStruct((tp, LANES), F32)],
        compiler_params=_cparams(("parallel",)),
        name="norm2_router",
    )(x2, g.reshape(1, d), w_r, b_r)


def _expert_kernel(be_ref, nu_ref, x_ref, g_ref, w1_ref, w3_ref, w2_ref, o_ref):
    b = pl.program_id(0)
    hh = pl.program_id(1)

    @pl.when(hh == 0)
    def _():
        o_ref[...] = jnp.zeros(o_ref.shape, F32)

    @pl.when(b < nu_ref[0])
    def _():
        x = x_ref[...]
        a = _dot(x, w1_ref[...].astype(BF16))
        c = _dot(x, w3_ref[...].astype(BF16))
        hid = (jax.nn.silu(a) * c).astype(BF16)
        o_ref[...] += _dot(hid, w2_ref[...].astype(BF16)) * g_ref[...]


def _experts(xb, slot_gate, block_expert, n_used, w1, w3, w2):
    p, d = xb.shape
    nb = p // MOE_BLOCK
    dh = D_EXPERT // 2

    def half(b, hh):
        return (hh + b) % 2

    def eidx(b, be, nu):
        return be[jnp.minimum(b, nu[0] - 1)]

    grid_spec = pltpu.PrefetchScalarGridSpec(
        num_scalar_prefetch=2,
        grid=(nb, 2),
        in_specs=[pl.BlockSpec((MOE_BLOCK, d), lambda b, hh, be, nu: (b, 0)),
                  pl.BlockSpec((MOE_BLOCK, 1), lambda b, hh, be, nu: (b, 0)),
                  pl.BlockSpec((None, d, dh), lambda b, hh, be, nu: (eidx(b, be, nu), 0, half(b, hh))),
                  pl.BlockSpec((None, d, dh), lambda b, hh, be, nu: (eidx(b, be, nu), 0, half(b, hh))),
                  pl.BlockSpec((None, dh, d), lambda b, hh, be, nu: (eidx(b, be, nu), half(b, hh), 0))],
        out_specs=pl.BlockSpec((MOE_BLOCK, d), lambda b, hh, be, nu: (b, 0)),
    )
    return pl.pallas_call(
        _expert_kernel,
        grid_spec=grid_spec,
        out_shape=jax.ShapeDtypeStruct((p, d), F32),
        compiler_params=_cparams(("arbitrary", "arbitrary")),
        name="expert_mlp",
    )(block_expert, n_used, xb, slot_gate, w1, w3, w2)


def _route_plan(expert, gate, t_real):
    s = t_real * TOP_K
    n_blocks = -(-(s + N_EXPERTS * (MOE_BLOCK - 1)) // MOE_BLOCK)
    p = n_blocks * MOE_BLOCK
    flat_e = expert.reshape(-1).astype(jnp.int32)
    flat_t = jnp.repeat(jnp.arange(t_real, dtype=jnp.int32), TOP_K)
    flat_g = gate.reshape(-1).astype(F32)
    order = jnp.argsort(flat_e)
    se = flat_e[order]
    counts = jnp.bincount(flat_e, length=N_EXPERTS)
    padded = (counts + MOE_BLOCK - 1) // MOE_BLOCK * MOE_BLOCK
    pad_end = jnp.cumsum(padded)
    pad_start = pad_end - padded
    start = jnp.cumsum(counts) - counts
    dest = (pad_start[se] + jnp.arange(s, dtype=jnp.int32) - start[se]).astype(jnp.int32)
    slot_tok = jnp.full((p,), t_real, jnp.int32).at[dest].set(flat_t[order])
    slot_gate = jnp.zeros((p,), F32).at[dest].set(flat_g[order])
    block_start = jnp.arange(n_blocks, dtype=jnp.int32) * MOE_BLOCK
    block_expert = jnp.minimum(jnp.searchsorted(pad_end, block_start, side='right'),
                               N_EXPERTS - 1).astype(jnp.int32)
    n_used = (pad_end[-1] // MOE_BLOCK).astype(jnp.int32).reshape(1)
    pos = jnp.zeros((s,), jnp.int32).at[order].set(dest).reshape(t_real, TOP_K)
    return slot_tok, slot_gate, block_expert, n_used, pos


def kernel(x_prompt, x_sample, cache_k, cache_v, state_ssm_re, state_ssm_im, meta_tokens, norm1_g, w_in, b_in, ssm_a_re, ssm_a_im, ssm_log_dt, ssm_b_re, ssm_b_im, ssm_c_re, ssm_c_im, ssm_d, w_glu, b_glu, w_ssm_proj, q_norm_g, k_norm_g, lam_q1, lam_k1, lam_q2, lam_k2, subln_g, w_att_proj, w_o, norm2_g, w_router_group, b_router_group, w_router_expert, b_router_expert, w1_e, w3_e, w2_e):
    assert x_prompt.shape[0] == 1 and w_in.shape[0] == 1
    seq = x_prompt.shape[1]
    nb, nq = x_sample.shape[0], x_sample.shape[1]
    past = cache_k.shape[2]
    n_s = nb * nq
    t_real = seq + N_META + n_s
    tp = -(-t_real // ROW_ALIGN) * ROW_ALIGN
    off_meta, off_s = seq, seq + N_META
    tq = 256
    assert seq % tq == 0 and nq == N_META and past % 512 == 0
    lam_init = 0.8 - 0.6 * math.exp(-0.3 * 0)
    out_scale = 1.0 - lam_init

    x_cat = jnp.concatenate([x_prompt[0], meta_tokens.astype(F32), x_sample.reshape(n_s, D_MODEL),
                             jnp.zeros((tp - t_real, D_MODEL), F32)], axis=0)

    tm = _row_tile(tp, 1088)
    tr = _row_tile(tp, 256)

    h1 = _rmsnorm(x_cat, norm1_g[0], tr)
    z = _inproj(h1, w_in[0], b_in[0], tm, 512)

    pos = jnp.concatenate([N_META + jnp.arange(seq), jnp.arange(N_META),
                           jnp.tile(past + jnp.arange(nq), nb),
                           jnp.zeros((tp - t_real,), jnp.int32)]).astype(F32)
    half = HEAD_DIM // 2
    inv = ROPE_THETA ** (-jnp.arange(half, dtype=F32) / half)
    ang = pos[:, None] * inv[None, :]
    cos_t = jnp.tile(jnp.cos(ang), (1, LANES // half))
    sin_h = jnp.sin(ang)
    sin_t = jnp.tile(jnp.concatenate([-sin_h, sin_h], axis=1), (1, LANES // HEAD_DIM))
    gq = jnp.tile(q_norm_g[0], LANES // HEAD_DIM).reshape(1, LANES)
    gk = jnp.tile(k_norm_g[0], LANES // HEAD_DIM).reshape(1, LANES)
    qb, kf, kb, vf, vb = _qk_rope(z, cos_t, sin_t, gq, gk, tr)

    lam = (jnp.exp(jnp.sum(lam_q1[0] * lam_k1[0])) - jnp.exp(jnp.sum(lam_q2[0] * lam_k2[0])) + lam_init)
    lam_row = jnp.full((1, LANES), lam, F32)
    sg = subln_g[0].reshape(1, LANES)

    o_att = _attn_prompt(qb, kb, vb, lam_row, sg, seq, out_scale, tq)

    q_s = qb[off_s:off_s + n_s].reshape(nb, nq, N_HEADS, LANES).transpose(0, 2, 1, 3)
    lane = jnp.arange(LANES)
    qz = jnp.concatenate([jnp.where(lane < HEAD_DIM, q_s, 0), jnp.where(lane >= HEAD_DIM, q_s, 0)], axis=2)
    pad_new = lambda a: jnp.pad(a[off_s:off_s + n_s].reshape(nb, nq, D_ATT), ((0, 0), (0, LANES - nq), (0, 0)))
    o_att = _attn_sample(qz, cache_k[0].reshape(nb, past, D_ATT), cache_v[0].reshape(nb, past, D_ATT),
                         pad_new(kb), pad_new(vb), o_att, lam_row, sg, off_s, out_scale, 512)

    gp = N_SSM_GROUPS * SSM_STATE
    s5w = _s5_weights(ssm_a_re[0], ssm_a_im[0], ssm_log_dt[0], ssm_b_re[0], ssm_b_im[0],
                      ssm_c_re[0], ssm_c_im[0])
    ys, hp_re, hp_im, hs_re, hs_im = _s5(z, s5w, ssm_d[0], state_ssm_re[0].reshape(nb, gp),
                                         state_ssm_im[0].reshape(nb, gp), seq, nb, nq)
    ysg = _glu(ys, w_glu[0], b_glu[0], tm, 512)
    m = _merge(ysg, o_att, w_ssm_proj[0], w_att_proj[0], z, tm, 512)
    x2 = _outproj(m, w_o[0], x_cat, tm, 512)

    w_r = jnp.concatenate([w_router_group[0], w_router_expert[0],
                           jnp.zeros((D_MODEL, LANES - N_EGROUPS - N_EXPERTS), F32)], axis=1).astype(BF16)
    b_r = jnp.concatenate([b_router_group[0], b_router_expert[0],
                           jnp.zeros((LANES - N_EGROUPS - N_EXPERTS,), F32)]).reshape(1, LANES)
    h2, e_sel, g_sel = _router(x2, norm2_g[0], w_r, b_r, tr)

    slot_tok, slot_gate, block_expert, n_used, pos_tk = _route_plan(
        e_sel[:t_real, :TOP_K], g_sel[:t_real, :TOP_K], t_real)
    xb = h2[jnp.minimum(slot_tok, tp - 1)]
    yb = _experts(xb, slot_gate.reshape(-1, 1), block_expert, n_used, w1_e[0], w3_e[0], w2_e[0])
    out = x2[:t_real] + yb[pos_tk[:, 0]] + yb[pos_tk[:, 1]]

    def heads(a, lead):
        return a.reshape(lead + (N_HEADS, 2, HEAD_DIM))

    y_prompt = out[:seq].reshape(1, seq, D_MODEL)
    y_sample = out[off_s:off_s + n_s].reshape(nb, nq, D_MODEL)
    k_p = jnp.concatenate([kf[off_meta:off_meta + N_META], kf[:seq]], axis=0)
    v_p = jnp.concatenate([vf[off_meta:off_meta + N_META], vf[:seq]], axis=0)
    k_prompt = heads(k_p, (1, 1, seq + N_META))
    v_prompt = v_p.reshape(1, 1, seq + N_META, N_HEADS, V_DIM)
    k_sample = heads(kf[off_s:off_s + n_s], (1, nb, nq))
    v_sample = vf[off_s:off_s + n_s].reshape(1, nb, nq, N_HEADS, V_DIM)
    st = lambda a, lead: a.reshape(lead + (N_SSM_GROUPS, SSM_STATE))
    return (y_prompt, y_sample, k_prompt, v_prompt, st(hp_re, (1, 1)), st(hp_im, (1, 1)),
            k_sample, v_sample, st(hs_re, (1, nb)), st(hs_im, (1, nb)))
```

```python
import functools
import math

import jax
import jax.numpy as jnp
from jax import lax
from jax.experimental import pallas as pl
from jax.experimental.pallas import tpu as pltpu

F32 = jnp.float32
BF16 = jnp.bfloat16

D_MODEL = 4096
N_META = 16
CHUNK = 64
N_HEADS = 16
HEAD_DIM = 64
V_DIM = 128
D_ATT = N_HEADS * V_DIM
D_SSM = 2048
SSM_GROUP = 16
N_SSM_GROUPS = D_SSM // SSM_GROUP
SSM_STATE = 64
IN_WIDTH = D_SSM + 3 * D_ATT + 2 * D_MODEL
ROPE_THETA = 10000.0
N_EGROUPS = 8
EXPERTS_PER_GROUP = 8
N_EXPERTS = N_EGROUPS * EXPERTS_PER_GROUP
TOP_K = 2
D_EXPERT = 512
MOE_BLOCK = 128
EPS = 1e-6

LANES = 128
ROW_ALIGN = 512
S5_CHUNK = 8
S5_LANE_GROUPS = LANES // SSM_GROUP
S5_TILES = D_SSM // LANES
VMEM_LIMIT = 56 * 1024 * 1024


def _cparams(sem, vmem=VMEM_LIMIT):
    return pltpu.CompilerParams(dimension_semantics=sem, vmem_limit_bytes=vmem)


def _row_tile(tp, cap):
    best = 16
    for t in range(16, cap + 1, 16):
        if tp % t == 0:
            best = t
    return best


def _dot(a, b):
    return jnp.dot(a, b, preferred_element_type=F32)


def _dot_nt(a, b):
    return lax.dot_general(a, b, (((1,), (1,)), ((), ())), preferred_element_type=F32)


def _rmsnorm_kernel(x_ref, g_ref, o_ref):
    x = x_ref[...]
    ms = jnp.mean(x * x, axis=-1, keepdims=True)
    o_ref[...] = (x * lax.rsqrt(ms + EPS) * g_ref[...]).astype(o_ref.dtype)


def _rmsnorm(x, g, tr):
    tp, d = x.shape
    return pl.pallas_call(
        _rmsnorm_kernel,
        grid=(tp // tr,),
        in_specs=[pl.BlockSpec((tr, d), lambda i: (i, 0)),
                  pl.BlockSpec((1, d), lambda i: (0, 0))],
        out_specs=pl.BlockSpec((tr, d), lambda i: (i, 0)),
        out_shape=jax.ShapeDtypeStruct((tp, d), BF16),
        compiler_params=_cparams(("parallel",)),
        name="rmsnorm1",
    )(x, g.reshape(1, d))


def _inproj_kernel(x_ref, w_ref, b_ref, o_ref):
    o_ref[...] = _dot(x_ref[...], w_ref[...].astype(BF16)) + b_ref[...]


def _inproj(h, w, b, tm, tn):
    tp, k = h.shape
    n = w.shape[1]
    return pl.pallas_call(
        _inproj_kernel,
        grid=(n // tn, tp // tm),
        in_specs=[pl.BlockSpec((tm, k), lambda j, i: (i, 0)),
                  pl.BlockSpec((k, tn), lambda j, i: (0, j)),
                  pl.BlockSpec((1, tn), lambda j, i: (0, j))],
        out_specs=pl.BlockSpec((tm, tn), lambda j, i: (i, j)),
        out_shape=jax.ShapeDtypeStruct((tp, n), F32),
        compiler_params=_cparams(("parallel", "parallel")),
        name="in_proj",
    )(h, w, b.reshape(1, n))


def _glu_kernel(x_ref, w_ref, b_ref, xe_ref, o_ref):
    a = _dot(x_ref[...].astype(BF16), w_ref[...].astype(BF16)) + b_ref[...]
    o_ref[...] = (xe_ref[...] * jax.nn.sigmoid(a)).astype(o_ref.dtype)


def _glu(ys, w, b, tm, tn):
    tp, k = ys.shape
    n = w.shape[1]
    return pl.pallas_call(
        _glu_kernel,
        grid=(n // tn, tp // tm),
        in_specs=[pl.BlockSpec((tm, k), lambda j, i: (i, 0)),
                  pl.BlockSpec((k, tn), lambda j, i: (0, j)),
                  pl.BlockSpec((1, tn), lambda j, i: (0, j)),
                  pl.BlockSpec((tm, tn), lambda j, i: (i, j))],
        out_specs=pl.BlockSpec((tm, tn), lambda j, i: (i, j)),
        out_shape=jax.ShapeDtypeStruct((tp, n), BF16),
        compiler_params=_cparams(("parallel", "parallel")),
        name="glu",
    )(ys, w, b.reshape(1, n), ys)


def _merge_kernel(ys_ref, oa_ref, ws_ref, wa_ref, gs_ref, ga_ref, o_ref):
    a = _dot(ys_ref[...], ws_ref[...].astype(BF16))
    b = _dot(oa_ref[...], wa_ref[...].astype(BF16))
    m = jax.nn.sigmoid(gs_ref[...]) * a + jax.nn.sigmoid(ga_ref[...]) * b
    o_ref[...] = m.astype(o_ref.dtype)


def _merge(ysg, oatt, w_ssm, w_att, z, tm, tn):
    tp, k = ysg.shape
    n = w_ssm.shape[1]
    gs_blk = (D_SSM + 3 * D_ATT) // tn
    ga_blk = (D_SSM + 3 * D_ATT + D_MODEL) // tn
    return pl.pallas_call(
        _merge_kernel,
        grid=(n // tn, tp // tm),
        in_specs=[pl.BlockSpec((tm, k), lambda j, i: (i, 0)),
                  pl.BlockSpec((tm, k), lambda j, i: (i, 0)),
                  pl.BlockSpec((k, tn), lambda j, i: (0, j)),
                  pl.BlockSpec((k, tn), lambda j, i: (0, j)),
                  pl.BlockSpec((tm, tn), lambda j, i: (i, gs_blk + j)),
                  pl.BlockSpec((tm, tn), lambda j, i: (i, ga_blk + j))],
        out_specs=pl.BlockSpec((tm, tn), lambda j, i: (i, j)),
        out_shape=jax.ShapeDtypeStruct((tp, n), BF16),
        compiler_params=_cparams(("parallel", "parallel")),
        name="merge_proj",
    )(ysg, oatt, w_ssm, w_att, z, z)


def _outproj_kernel(m_ref, w_ref, x_ref, o_ref):
    o_ref[...] = x_ref[...] + _dot(m_ref[...], w_ref[...].astype(BF16))


def _outproj(m, w, x, tm, tn):
    tp, k = m.shape
    n = w.shape[1]
    return pl.pallas_call(
        _outproj_kernel,
        grid=(n // tn, tp // tm),
        in_specs=[pl.BlockSpec((tm, k), lambda j, i: (i, 0)),
                  pl.BlockSpec((k, tn), lambda j, i: (0, j)),
                  pl.BlockSpec((tm, tn), lambda j, i: (i, j))],
        out_specs=pl.BlockSpec((tm, tn), lambda j, i: (i, j)),
        out_shape=jax.ShapeDtypeStruct((tp, n), F32),
        compiler_params=_cparams(("parallel", "parallel")),
        name="out_proj",
    )(m, w, x)


def _segment_sumsq(x, ones_bd):
    x2 = x * x
    hi = x2.astype(BF16)
    lo = (x2 - hi.astype(F32)).astype(BF16)
    return _dot(hi, ones_bd) + _dot(lo, ones_bd)


def _qk_rope_kernel(zq_ref, zk_ref, zv_ref, cos_ref, sin_ref, gq_ref, gk_ref, ones_ref,
                    qb_ref, kf_ref, kb_ref, vf_ref, vb_ref):
    cos = cos_ref[...]
    sin = sin_ref[...]
    ones_bd = ones_ref[...]
    lane = lax.broadcasted_iota(jnp.int32, cos.shape, 1)
    first_half = (lane % HEAD_DIM) < (HEAD_DIM // 2)

    def norm_rope(x, g):
        ss = _segment_sumsq(x, ones_bd)
        xn = x * lax.rsqrt(ss * (1.0 / HEAD_DIM) + EPS) * g
        partner = jnp.where(first_half,
                            pltpu.roll(xn, LANES - HEAD_DIM // 2, 1),
                            pltpu.roll(xn, HEAD_DIM // 2, 1))
        return xn * cos + partner * sin

    for h in range(N_HEADS):
        sl = slice(h * LANES, (h + 1) * LANES)
        q = norm_rope(zq_ref[:, sl], gq_ref[...])
        qb_ref[:, sl] = (q * (HEAD_DIM ** -0.5)).astype(BF16)
        k = norm_rope(zk_ref[:, sl], gk_ref[...])
        kf_ref[:, sl] = k
        kb_ref[:, sl] = k.astype(BF16)
    v = zv_ref[...]
    vf_ref[...] = v
    vb_ref[...] = v.astype(BF16)


def _qk_rope(z, cos_t, sin_t, gq, gk, tr):
    tp = z.shape[0]
    ones_bd = jnp.kron(jnp.eye(LANES // HEAD_DIM, dtype=F32),
                       jnp.ones((HEAD_DIM, HEAD_DIM), F32)).astype(BF16)
    zspec = lambda c: pl.BlockSpec((tr, D_ATT), lambda i: (i, c))
    row = pl.BlockSpec((tr, LANES), lambda i: (i, 0))
    const = pl.BlockSpec((1, LANES), lambda i: (0, 0))
    out = pl.BlockSpec((tr, D_ATT), lambda i: (i, 0))
    q_blk = D_SSM // D_ATT
    return pl.pallas_call(
        _qk_rope_kernel,
        grid=(tp // tr,),
        in_specs=[zspec(q_blk), zspec(q_blk + 1), zspec(q_blk + 2), row, row, const, const,
                  pl.BlockSpec((LANES, LANES), lambda i: (0, 0))],
        out_specs=[out, out, out, out, out],
        out_shape=[jax.ShapeDtypeStruct((tp, D_ATT), BF16),
                   jax.ShapeDtypeStruct((tp, D_ATT), F32),
                   jax.ShapeDtypeStruct((tp, D_ATT), BF16),
                   jax.ShapeDtypeStruct((tp, D_ATT), F32),
                   jax.ShapeDtypeStruct((tp, D_ATT), BF16)],
        compiler_params=_cparams(("parallel",)),
        name="qk_norm_rope",
    )(z, z, z, cos_t, sin_t, gq, gk, ones_bd)


def _softmax_step(c, qc, kt, vt, mask, m_ref, l_ref, acc_ref):
    s = _dot_nt(qc, kt)
    if mask is not None:
        s = jnp.where(mask, s, -jnp.inf)
    m_prev = m_ref[c]
    m_new = jnp.maximum(m_prev, jnp.max(s, axis=1, keepdims=True))
    alpha = jnp.exp(m_prev - m_new)
    p = jnp.exp(s - m_new[:, :1])
    l_ref[c] = alpha * l_ref[c] + jnp.sum(p, axis=1, keepdims=True)
    acc_ref[c] = alpha * acc_ref[c] + _dot(p.astype(BF16), vt)
    m_ref[c] = m_new


def _diff_finish(o0, o1, lam, g, out_scale):
    o = o0 - lam * o1
    ms = jnp.mean(o * o, axis=-1, keepdims=True)
    return o * lax.rsqrt(ms + EPS) * g * out_scale


def _split_components(q):
    lane = lax.broadcasted_iota(jnp.int32, q.shape, 1)
    zero = jnp.zeros_like(q)
    return jnp.where(lane < HEAD_DIM, q, zero), jnp.where(lane >= HEAD_DIM, q, zero)


def _attn_prompt_kernel(lam_ref, g_ref, q_ref, k_ref, v_ref, o_ref, m_ref, l_ref, acc_ref,
                        *, tq, nq_main, seq, out_scale):
    i = pl.program_id(1)
    q0, q1 = _split_components(q_ref[...])
    m_ref[...] = jnp.full(m_ref.shape, -jnp.inf, F32)
    l_ref[...] = jnp.zeros(l_ref.shape, F32)
    acc_ref[...] = jnp.zeros(acc_ref.shape, F32)

    def update(kt, vt, mask):
        _softmax_step(0, q0, kt, vt, mask, m_ref, l_ref, acc_ref)
        _softmax_step(1, q1, kt, vt, mask, m_ref, l_ref, acc_ref)

    col = lax.broadcasted_iota(jnp.int32, (tq, LANES), 1)
    update(k_ref[pl.ds(seq, LANES), :], v_ref[pl.ds(seq, LANES), :], col < N_META)

    is_main = i < nq_main

    def body(j, carry):
        start = pl.multiple_of(j * tq, tq)
        update(k_ref[pl.ds(start, tq), :], v_ref[pl.ds(start, tq), :], None)
        return carry

    lax.fori_loop(0, jnp.where(is_main, i, 0), body, 0)

    @pl.when(is_main)
    def _():
        start = pl.multiple_of(i * tq, tq)
        r = lax.broadcasted_iota(jnp.int32, (tq, tq), 0) // CHUNK
        c = lax.broadcasted_iota(jnp.int32, (tq, tq), 1) // CHUNK
        update(k_ref[pl.ds(start, tq), :], v_ref[pl.ds(start, tq), :], c <= r)

    o0 = acc_ref[0] / l_ref[0]
    o1 = acc_ref[1] / l_ref[1]
    o_ref[...] = _diff_finish(o0, o1, lam_ref[...], g_ref[...], out_scale).astype(o_ref.dtype)


def _attn_prompt(qb, kb, vb, lam_row, subln_g, seq, out_scale, tq):
    tp = qb.shape[0]
    kern = functools.partial(_attn_prompt_kernel, tq=tq, nq_main=seq // tq, seq=seq,
                             out_scale=out_scale)
    const = pl.BlockSpec((1, LANES), lambda h, i: (0, 0))
    return pl.pallas_call(
        kern,
        grid=(N_HEADS, tp // tq),
        in_specs=[const, const,
                  pl.BlockSpec((tq, LANES), lambda h, i: (i, h)),
                  pl.BlockSpec((tp, LANES), lambda h, i: (0, h)),
                  pl.BlockSpec((tp, LANES), lambda h, i: (0, h))],
        out_specs=pl.BlockSpec((tq, LANES), lambda h, i: (i, h)),
        out_shape=jax.ShapeDtypeStruct((tp, D_ATT), BF16),
        scratch_shapes=[pltpu.VMEM((2, tq, LANES), F32),
                        pltpu.VMEM((2, tq, LANES), F32),
                        pltpu.VMEM((2, tq, LANES), F32)],
        compiler_params=_cparams(("parallel", "parallel")),
        name="attn_prompt",
    )(lam_row, subln_g, qb, kb, vb)


LOGIT_BOUND_MAX = 40.0


def _with_ones(vt):
    return jnp.concatenate([vt, jnp.ones(vt.shape, vt.dtype)], axis=1)


def _attn_prompt_fast_kernel(lam_ref, g_ref, q_ref, k_ref, v_ref, o_ref, acc_ref,
                             *, tq, nq_main, seq, out_scale):
    i = pl.program_id(1)
    q0, q1 = _split_components(q_ref[...])
    qq = jnp.concatenate([q0, q1], axis=0)

    def update(kt, vt, mask, first=False):
        p = jnp.exp(_dot_nt(qq, kt))
        if mask is not None:
            p = jnp.where(mask, p, 0.0)
        pv = _dot(p.astype(BF16), _with_ones(vt))
        if first:
            acc_ref[...] = pv
        else:
            acc_ref[...] += pv

    col = lax.broadcasted_iota(jnp.int32, (2 * tq, LANES), 1)
    update(k_ref[pl.ds(seq, LANES), :], v_ref[pl.ds(seq, LANES), :], col < N_META, first=True)

    is_main = i < nq_main

    def body(j, carry):
        start = pl.multiple_of(j * tq, tq)
        update(k_ref[pl.ds(start, tq), :], v_ref[pl.ds(start, tq), :], None)
        return carry

    lax.fori_loop(0, jnp.where(is_main, i, 0), body, 0)

    @pl.when(is_main)
    def _():
        start = pl.multiple_of(i * tq, tq)
        r = (lax.broadcasted_iota(jnp.int32, (2 * tq, tq), 0) % tq) // CHUNK
        c = lax.broadcasted_iota(jnp.int32, (2 * tq, tq), 1) // CHUNK
        update(k_ref[pl.ds(start, tq), :], v_ref[pl.ds(start, tq), :], c <= r)

    acc = acc_ref[...]
    o0 = acc[:tq, :LANES] / acc[:tq, LANES:]
    o1 = acc[tq:, :LANES] / acc[tq:, LANES:]
    o_ref[...] = _diff_finish(o0, o1, lam_ref[...], g_ref[...], out_scale).astype(o_ref.dtype)


def _attn_prompt_fast(qb, kb, vb, lam_row, subln_g, seq, out_scale, tq):
    tp = qb.shape[0]
    kern = functools.partial(_attn_prompt_fast_kernel, tq=tq, nq_main=seq // tq, seq=seq,
                             out_scale=out_scale)
    const = pl.BlockSpec((1, LANES), lambda h, i: (0, 0))
    return pl.pallas_call(
        kern,
        grid=(N_HEADS, tp // tq),
        in_specs=[const, const,
                  pl.BlockSpec((tq, LANES), lambda h, i: (i, h)),
                  pl.BlockSpec((tp, LANES), lambda h, i: (0, h)),
                  pl.BlockSpec((tp, LANES), lambda h, i: (0, h))],
        out_specs=pl.BlockSpec((tq, LANES), lambda h, i: (i, h)),
        out_shape=jax.ShapeDtypeStruct((tp, D_ATT), BF16),
        scratch_shapes=[pltpu.VMEM((2 * tq, 2 * LANES), F32)],
        compiler_params=_cparams(("parallel", "parallel")),
        name="attn_prompt_fast",
    )(lam_row, subln_g, qb, kb, vb)


def _attn_sample_fast_kernel(lam_ref, g_ref, q_ref, kc_ref, vc_ref, kn_ref, vn_ref, o_in_ref, o_ref,
                             acc_ref, *, nq, out_scale):
    del o_in_ref
    j = pl.program_id(1)
    rows = 2 * nq

    def head_update(h, kt, vt, mask):
        p = jnp.exp(_dot_nt(q_ref[h], kt))
        if mask is not None:
            p = jnp.where(mask, p, 0.0)
        acc_ref[h] += _dot(p.astype(BF16), _with_ones(vt))

    @pl.when(j == 0)
    def _():
        acc_ref[...] = jnp.zeros(acc_ref.shape, F32)
        col = lax.broadcasted_iota(jnp.int32, (rows, LANES), 1)
        for h in range(N_HEADS):
            sl = slice(h * LANES, (h + 1) * LANES)
            head_update(h, kn_ref[:, sl], vn_ref[:, sl], col < nq)

    for h in range(N_HEADS):
        sl = slice(h * LANES, (h + 1) * LANES)
        head_update(h, kc_ref[:, sl], vc_ref[:, sl], None)

    @pl.when(j == pl.num_programs(1) - 1)
    def _():
        for h in range(N_HEADS):
            acc = acc_ref[h]
            o = acc[:, :LANES] / acc[:, LANES:]
            res = _diff_finish(o[:nq], o[nq:], lam_ref[...], g_ref[...], out_scale)
            o_ref[:, h * LANES:(h + 1) * LANES] = res.astype(o_ref.dtype)


def _attn_sample_kernel(lam_ref, g_ref, q_ref, kc_ref, vc_ref, kn_ref, vn_ref, o_in_ref, o_ref,
                        m_ref, l_ref, acc_ref, *, nq, out_scale):
    del o_in_ref
    j = pl.program_id(1)
    rows = 2 * nq

    @pl.when(j == 0)
    def _():
        m_ref[...] = jnp.full(m_ref.shape, -jnp.inf, F32)
        l_ref[...] = jnp.zeros(l_ref.shape, F32)
        acc_ref[...] = jnp.zeros(acc_ref.shape, F32)
        col = lax.broadcasted_iota(jnp.int32, (rows, LANES), 1)
        for h in range(N_HEADS):
            sl = slice(h * LANES, (h + 1) * LANES)
            _softmax_step(h, q_ref[h], kn_ref[:, sl], vn_ref[:, sl], col < nq,
                          m_ref, l_ref, acc_ref)

    for h in range(N_HEADS):
        sl = slice(h * LANES, (h + 1) * LANES)
        _softmax_step(h, q_ref[h], kc_ref[:, sl].astype(BF16), vc_ref[:, sl].astype(BF16), None,
                      m_ref, l_ref, acc_ref)

    @pl.when(j == pl.num_programs(1) - 1)
    def _():
        for h in range(N_HEADS):
            o = acc_ref[h] / l_ref[h]
            res = _diff_finish(o[:nq], o[nq:], lam_ref[...], g_ref[...], out_scale)
            o_ref[:, h * LANES:(h + 1) * LANES] = res.astype(o_ref.dtype)


def _attn_sample(qz, cache_k, cache_v, k_new, v_new, o_buf, lam_row, subln_g, row_off, out_scale, tk, fast):
    nb, past = cache_k.shape[0], cache_k.shape[1]
    nq = qz.shape[2] // 2
    if fast:
        kern = functools.partial(_attn_sample_fast_kernel, nq=nq, out_scale=out_scale)
        scratch = [pltpu.VMEM((N_HEADS, 2 * nq, 2 * LANES), F32)]
    else:
        kern = functools.partial(_attn_sample_kernel, nq=nq, out_scale=out_scale)
        scratch = [pltpu.VMEM((N_HEADS, 2 * nq, LANES), F32)] * 3
    const = pl.BlockSpec((1, LANES), lambda b, j: (0, 0))
    blk_off = row_off // nq
    return pl.pallas_call(
        kern,
        grid=(nb, past // tk),
        in_specs=[const, const,
                  pl.BlockSpec((None, N_HEADS, 2 * nq, LANES), lambda b, j: (b, 0, 0, 0)),
                  pl.BlockSpec((None, tk, D_ATT), lambda b, j: (b, j, 0)),
                  pl.BlockSpec((None, tk, D_ATT), lambda b, j: (b, j, 0)),
                  pl.BlockSpec((None, LANES, D_ATT), lambda b, j: (b, 0, 0)),
                  pl.BlockSpec((None, LANES, D_ATT), lambda b, j: (b, 0, 0)),
                  pl.BlockSpec(memory_space=pl.ANY)],
        out_specs=pl.BlockSpec((nq, D_ATT), lambda b, j: (blk_off + b, 0)),
        out_shape=jax.ShapeDtypeStruct(o_buf.shape, o_buf.dtype),
        scratch_shapes=scratch,
        input_output_aliases={7: 0},
        compiler_params=_cparams(("parallel", "arbitrary")),
        name="attn_sample_fast" if fast else "attn_sample",
    )(lam_row, subln_g, qz, cache_k, cache_v, k_new, v_new, o_buf)


def _s5_weights(a_re, a_im, log_dt, b_re, b_im, c_re, c_im):
    hp = lax.Precision.HIGHEST
    n_t, gl, tc = S5_TILES, S5_LANE_GROUPS, S5_CHUNK
    dt = jnp.exp(log_dt)[:, None]
    mag = jnp.exp(a_re * dt)
    abar_re = mag * jnp.cos(a_im * dt)
    abar_im = mag * jnp.sin(a_im * dt)
    nr, ni = abar_re - 1.0, abar_im
    den = a_re * a_re + a_im * a_im
    coef_re = (nr * a_re + ni * a_im) / den
    coef_im = (ni * a_re - nr * a_im) / den
    bbar_re = coef_re[..., None] * b_re - coef_im[..., None] * b_im
    bbar_im = coef_re[..., None] * b_im + coef_im[..., None] * b_re
    n = jnp.arange(tc + 1, dtype=F32)[:, None, None]
    pw_mag = jnp.exp(n * (a_re * dt))
    pw_re = pw_mag * jnp.cos(n * (a_im * dt))
    pw_im = pw_mag * jnp.sin(n * (a_im * dt))
    e_re = pw_re[:tc, :, :, None] * bbar_re - pw_im[:tc, :, :, None] * bbar_im
    e_im = pw_re[:tc, :, :, None] * bbar_im + pw_im[:tc, :, :, None] * bbar_re
    kern = (jnp.einsum('gcp,lgpd->glcd', c_re, e_re, precision=hp)
            - jnp.einsum('gcp,lgpd->glcd', c_im, e_im, precision=hp))
    eye = jnp.eye(gl, dtype=F32)
    w_intra = jnp.einsum('jglcd,gh->jlgdhc', kern.reshape(n_t, gl, tc, SSM_GROUP, SSM_GROUP), eye)
    w_intra = w_intra.reshape(n_t, tc, LANES, LANES)
    eb = jnp.stack([e_re[::-1], e_im[::-1]], 0)
    eb = eb.reshape(2, tc, n_t, gl, SSM_STATE, SSM_GROUP)
    w_state = jnp.einsum('rsjgpd,gh->jsgdrhp', eb, eye)
    w_state = w_state.reshape(n_t, tc * LANES, 2 * gl * SSM_STATE)
    cp_re = c_re[None] * pw_re[1:, :, None, :] - c_im[None] * pw_im[1:, :, None, :]
    cp_im = c_re[None] * pw_im[1:, :, None, :] + c_im[None] * pw_re[1:, :, None, :]
    cp = jnp.stack([cp_re, -cp_im], 0).reshape(2, tc, n_t, gl, SSM_GROUP, SSM_STATE)
    w_read = jnp.einsum('rtjgcp,gh->jrgpthc', cp, eye)
    w_read = w_read.reshape(n_t, 2 * gl * SSM_STATE, tc * LANES)
    half = gl * SSM_STATE
    a_pow = jnp.concatenate([pw_re[tc].reshape(n_t, 1, half), pw_im[tc].reshape(n_t, 1, half)], -1)
    return w_intra.astype(BF16), w_state.astype(BF16), w_read.astype(BF16), a_pow


def _s5_kernel(u_ref, wi_ref, ws_ref, wr_ref, ap_ref, d_ref, h0r_ref, h0i_ref,
               y_ref, hpr_ref, hpi_ref, hsr_ref, hsi_ref,
               y_acc, v_ref, hs_ref, wt_ref, *, nc, n_main, n_meta_chunks, n_seq, seq_chunks):
    tc = S5_CHUNK
    half = hs_ref.shape[1] // 2
    for s in range(tc):
        for t in range(tc):
            blk = wi_ref[t - s] if t >= s else jnp.zeros((LANES, LANES), BF16)
            wt_ref[s * LANES:(s + 1) * LANES, t * LANES:(t + 1) * LANES] = blk
    lhs = jnp.concatenate(
        [u_ref[pl.ds(s, nc, stride=tc), :].astype(BF16) for s in range(tc)], axis=1)
    y_acc[...] = _dot(lhs, wt_ref[...])
    v_ref[...] = _dot(lhs, ws_ref[...])
    a_re = ap_ref[:, :half]
    a_im = ap_ref[:, half:]

    def advance(h_re, h_im, v):
        return (a_re * h_re - a_im * h_im + v[:, :half],
                a_re * h_im + a_im * h_re + v[:, half:])

    hs_ref[...] = jnp.zeros(hs_ref.shape, F32)

    h_re = jnp.zeros((1, half), F32)
    h_im = jnp.zeros((1, half), F32)
    for c in range(n_main, n_main + n_meta_chunks):
        hs_ref[pl.ds(c, 1), :] = jnp.concatenate([h_re, h_im], axis=1)
        h_re, h_im = advance(h_re, h_im, v_ref[pl.ds(c, 1), :])

    def body(c, carry):
        h_re, h_im = carry
        hs_ref[pl.ds(c, 1), :] = jnp.concatenate([h_re, h_im], axis=1)
        return advance(h_re, h_im, v_ref[pl.ds(c, 1), :])

    h_re, h_im = lax.fori_loop(0, n_main, body, (h_re, h_im))
    hpr_ref[...] = h_re
    hpi_ref[...] = h_im

    base = n_main + n_meta_chunks
    for b in range(n_seq):
        s_re = h0r_ref[pl.ds(b, 1), :]
        s_im = h0i_ref[pl.ds(b, 1), :]
        for c in range(base + b * seq_chunks, base + (b + 1) * seq_chunks):
            hs_ref[pl.ds(c, 1), :] = jnp.concatenate([s_re, s_im], axis=1)
            s_re, s_im = advance(s_re, s_im, v_ref[pl.ds(c, 1), :])
        hsr_ref[pl.ds(b, 1), :] = s_re
        hsi_ref[pl.ds(b, 1), :] = s_im

    y_acc[...] += _dot(hs_ref[...].astype(BF16), wr_ref[...])
    d = d_ref[...]
    for t in range(tc):
        rows = pl.ds(t, nc, stride=tc)
        y = y_acc[:, t * LANES:(t + 1) * LANES] + d * u_ref[rows, :]
        y_ref[rows, :] = jax.nn.gelu(y)


def _s5(z, weights, d_skip, h0_re, h0_im, seq, n_seq, seq_len):
    tp = z.shape[0]
    tc = S5_CHUNK
    nc = tp // tc
    w_intra, w_state, w_read, a_pow = weights
    half = S5_LANE_GROUPS * SSM_STATE
    kern = functools.partial(_s5_kernel, nc=nc, n_main=seq // tc, n_meta_chunks=N_META // tc,
                             n_seq=n_seq, seq_chunks=seq_len // tc)
    wspec = lambda a: pl.BlockSpec((None,) + a.shape[1:], lambda j: (j,) + (0,) * (a.ndim - 1))
    col = pl.BlockSpec((tp, LANES), lambda j: (0, j))
    st = lambda r: pl.BlockSpec((r, half), lambda j: (0, j))
    gp = N_SSM_GROUPS * SSM_STATE
    return pl.pallas_call(
        kern,
        grid=(S5_TILES,),
        in_specs=[col, wspec(w_intra), wspec(w_state), wspec(w_read), wspec(a_pow),
                  pl.BlockSpec((1, LANES), lambda j: (0, j)), st(n_seq), st(n_seq)],
        out_specs=[col, st(1), st(1), st(n_seq), st(n_seq)],
        out_shape=[jax.ShapeDtypeStruct((tp, D_SSM), F32),
                   jax.ShapeDtypeStruct((1, gp), F32),
                   jax.ShapeDtypeStruct((1, gp), F32),
                   jax.ShapeDtypeStruct((n_seq, gp), F32),
                   jax.ShapeDtypeStruct((n_seq, gp), F32)],
        scratch_shapes=[pltpu.VMEM((nc, tc * LANES), F32),
                        pltpu.VMEM((nc, 2 * half), F32),
                        pltpu.VMEM((nc, 2 * half), F32),
                        pltpu.VMEM((tc * LANES, tc * LANES), BF16)],
        compiler_params=_cparams(("parallel",)),
        name="s5_scan",
    )(z, w_intra, w_state, w_read, a_pow, d_skip.reshape(1, D_SSM), h0_re, h0_im)


def _router_kernel(x_ref, g_ref, w_ref, b_ref, h_ref, e_ref, gate_ref):
    x = x_ref[...]
    ms = jnp.mean(x * x, axis=-1, keepdims=True)
    h = x * lax.rsqrt(ms + EPS) * g_ref[...]
    h_ref[...] = h.astype(h_ref.dtype)
    logits = _dot(h.astype(BF16), w_ref[...]) + b_ref[...]
    lane = lax.broadcasted_iota(jnp.int32, logits.shape, 1)
    neg = -jnp.inf
    big = jnp.int32(LANES)

    def first_argmax(vals, vmax):
        return jnp.min(jnp.where(vals == vmax, lane, big), axis=1, keepdims=True)

    lg = jnp.where(lane < N_EGROUPS, logits, neg)
    mg = jnp.max(lg, axis=1, keepdims=True)
    sg = jnp.sum(jnp.exp(lg - mg), axis=1, keepdims=True)
    g_w = 1.0 / sg
    g_idx = first_argmax(lg, mg)
    lo = N_EGROUPS + EXPERTS_PER_GROUP * g_idx
    le = jnp.where((lane >= lo) & (lane < lo + EXPERTS_PER_GROUP), logits, neg)
    m1 = jnp.max(le, axis=1, keepdims=True)
    se = jnp.sum(jnp.exp(le - m1), axis=1, keepdims=True)
    i1 = first_argmax(le, m1)
    le2 = jnp.where(lane == i1, neg, le)
    m2 = jnp.max(le2, axis=1, keepdims=True)
    i2 = first_argmax(le2, m2)
    p1 = 1.0 / se
    p2 = jnp.exp(m2 - m1) / se
    tot = p1 + p2
    w1 = g_w * (p1 / tot)
    w2 = g_w * (p2 / tot)
    e_ref[...] = jnp.where(lane == 0, i1 - N_EGROUPS, jnp.where(lane == 1, i2 - N_EGROUPS, 0))
    gate_ref[...] = jnp.where(lane == 0, w1, jnp.where(lane == 1, w2, 0.0))


def _router(x2, g, w_r, b_r, tr):
    tp, d = x2.shape
    return pl.pallas_call(
        _router_kernel,
        grid=(tp // tr,),
        in_specs=[pl.BlockSpec((tr, d), lambda i: (i, 0)),
                  pl.BlockSpec((1, d), lambda i: (0, 0)),
                  pl.BlockSpec((d, LANES), lambda i: (0, 0)),
                  pl.BlockSpec((1, LANES), lambda i: (0, 0))],
        out_specs=[pl.BlockSpec((tr, d), lambda i: (i, 0)),
                   pl.BlockSpec((tr, LANES), lambda i: (i, 0)),
                   pl.BlockSpec((tr, LANES), lambda i: (i, 0))],
        out_shape=[jax.ShapeDtypeStruct((tp, d), F32),
                   jax.ShapeDtypeStruct((tp, LANES), jnp.int32),
                   jax.ShapeDtypeStruct((tp, LANES), F32)],
        compiler_params=_cparams(("parallel",)),
        name="norm2_router",
    )(x2, g.reshape(1, d), w_r, b_r)


def _expert_kernel(be_ref, nu_ref, cnt_ref, tok_ref, dst_ref, h_hbm, g_ref, w1_ref, w3_ref, w2_ref,
                   y_hbm, xbuf, ybuf, gsem, ssem, *, nb):
    del be_ref
    b = pl.program_id(0)
    hh = pl.program_id(1)
    n_used = nu_ref[0]
    active = b < n_used

    def gather(r):
        return pltpu.make_async_copy(h_hbm.at[pl.ds(tok_ref[0, r], 1), :], xbuf.at[pl.ds(r, 1), :], gsem)

    def scatter(r, d):
        return pltpu.make_async_copy(ybuf.at[pl.ds(r, 1), :], y_hbm.at[pl.ds(d, 1), :], ssem)

    def wait_scatter(n):
        def body(r, c):
            scatter(0, 0).wait()
            return c
        lax.fori_loop(0, n, body, 0)

    @pl.when(active & (hh == 0))
    def _():
        for r in range(MOE_BLOCK):
            gather(r).start()

    @pl.when((hh == 0) & (b > 0) & (b - 1 < n_used))
    def _():
        wait_scatter(cnt_ref[jnp.maximum(b - 1, 0)])

    @pl.when(active & (hh == 0))
    def _():
        for r in range(MOE_BLOCK):
            gather(r).wait()

    @pl.when(active)
    def _():
        x = xbuf[...].astype(BF16)
        a = _dot(x, w1_ref[...].astype(BF16))
        c = _dot(x, w3_ref[...].astype(BF16))
        hid = (jax.nn.silu(a) * c).astype(BF16)
        y = _dot(hid, w2_ref[...].astype(BF16)) * g_ref[...]

        @pl.when(hh == 0)
        def _():
            ybuf[...] = y

        @pl.when(hh == 1)
        def _():
            ybuf[...] += y

    @pl.when(active & (hh == 1))
    def _():
        def body(r, c):
            scatter(r, dst_ref[0, r]).start()
            return c
        lax.fori_loop(0, cnt_ref[b], body, 0)

    @pl.when(active & (hh == 1) & (b == nb - 1))
    def _():
        wait_scatter(cnt_ref[b])


def _experts(h2, plan, t_real, w1, w3, w2):
    slot_tok, slot_dst, slot_gate, block_expert, n_used, cnt = plan
    d = h2.shape[1]
    nb = block_expert.shape[0]
    dh = D_EXPERT // 2

    def half(b, hh):
        return (hh + b) % 2

    def eidx(b, be, nu):
        return be[jnp.minimum(b, nu[0] - 1)]

    smem_row = pl.BlockSpec((None, 1, MOE_BLOCK), lambda b, hh, be, nu, cnt: (b, 0, 0),
                            memory_space=pltpu.SMEM)
    grid_spec = pltpu.PrefetchScalarGridSpec(
        num_scalar_prefetch=3,
        grid=(nb, 2),
        in_specs=[smem_row, smem_row,
                  pl.BlockSpec(memory_space=pl.ANY),
                  pl.BlockSpec((MOE_BLOCK, 1), lambda b, hh, be, nu, cnt: (b, 0)),
                  pl.BlockSpec((None, d, dh), lambda b, hh, be, nu, cnt: (eidx(b, be, nu), 0, half(b, hh))),
                  pl.BlockSpec((None, d, dh), lambda b, hh, be, nu, cnt: (eidx(b, be, nu), 0, half(b, hh))),
                  pl.BlockSpec((None, dh, d), lambda b, hh, be, nu, cnt: (eidx(b, be, nu), half(b, hh), 0))],
        out_specs=pl.BlockSpec(memory_space=pl.ANY),
        scratch_shapes=[pltpu.VMEM((MOE_BLOCK, d), F32),
                        pltpu.VMEM((MOE_BLOCK, d), F32),
                        pltpu.SemaphoreType.DMA(()),
                        pltpu.SemaphoreType.DMA(())],
    )
    return pl.pallas_call(
        functools.partial(_expert_kernel, nb=nb),
        grid_spec=grid_spec,
        out_shape=jax.ShapeDtypeStruct((TOP_K * t_real, d), F32),
        compiler_params=_cparams(("arbitrary", "arbitrary")),
        name="expert_mlp",
    )(block_expert, n_used, cnt, slot_tok.reshape(nb, 1, MOE_BLOCK), slot_dst.reshape(nb, 1, MOE_BLOCK),
      h2, slot_gate.reshape(-1, 1), w1, w3, w2)


def _route_plan(expert, gate, t_real):
    s = t_real * TOP_K
    n_blocks = -(-(s + N_EXPERTS * (MOE_BLOCK - 1)) // MOE_BLOCK)
    p = n_blocks * MOE_BLOCK
    flat_e = expert.reshape(-1).astype(jnp.int32)
    flat_g = gate.reshape(-1).astype(F32)
    order = jnp.argsort(flat_e).astype(jnp.int32)
    se = flat_e[order]
    counts = jnp.bincount(flat_e, length=N_EXPERTS).astype(jnp.int32)
    padded = (counts + MOE_BLOCK - 1) // MOE_BLOCK * MOE_BLOCK
    pad_end = jnp.cumsum(padded)
    pad_start = pad_end - padded
    start = jnp.cumsum(counts) - counts
    dest = (pad_start[se] + jnp.arange(s, dtype=jnp.int32) - start[se]).astype(jnp.int32)
    tok = order // TOP_K
    slot_tok = jnp.full((p,), t_real, jnp.int32).at[dest].set(tok)
    slot_dst = jnp.zeros((p,), jnp.int32).at[dest].set((order % TOP_K) * t_real + tok)
    slot_gate = jnp.zeros((p,), F32).at[dest].set(flat_g[order])
    block_start = jnp.arange(n_blocks, dtype=jnp.int32) * MOE_BLOCK
    block_expert = jnp.minimum(jnp.searchsorted(pad_end, block_start, side='right'),
                               N_EXPERTS - 1).astype(jnp.int32)
    n_used = (pad_end[-1] // MOE_BLOCK).astype(jnp.int32).reshape(1)
    cnt = jnp.clip(counts[block_expert] - (block_start - pad_start[block_expert]), 0, MOE_BLOCK)
    return slot_tok, slot_dst, slot_gate, block_expert, n_used, cnt.astype(jnp.int32)


def _combine_kernel(x_ref, y0_ref, y1_ref, o_ref):
    o_ref[...] = x_ref[...] + y0_ref[...] + y1_ref[...]


def _combine(x2, y2, row_off, n_rows, tile):
    d = x2.shape[1]
    off = row_off // tile
    y3 = y2.reshape(TOP_K, y2.shape[0] // TOP_K, d)
    return pl.pallas_call(
        _combine_kernel,
        grid=(n_rows // tile,),
        in_specs=[pl.BlockSpec((tile, d), lambda i: (off + i, 0)),
                  pl.BlockSpec((None, tile, d), lambda i: (0, off + i, 0)),
                  pl.BlockSpec((None, tile, d), lambda i: (1, off + i, 0))],
        out_specs=pl.BlockSpec((tile, d), lambda i: (i, 0)),
        out_shape=jax.ShapeDtypeStruct((n_rows, d), F32),
        compiler_params=_cparams(("parallel",)),
        name="moe_combine",
    )(x2, y3, y3)


def kernel(x_prompt, x_sample, cache_k, cache_v, state_ssm_re, state_ssm_im, meta_tokens, norm1_g, w_in, b_in, ssm_a_re, ssm_a_im, ssm_log_dt, ssm_b_re, ssm_b_im, ssm_c_re, ssm_c_im, ssm_d, w_glu, b_glu, w_ssm_proj, q_norm_g, k_norm_g, lam_q1, lam_k1, lam_q2, lam_k2, subln_g, w_att_proj, w_o, norm2_g, w_router_group, b_router_group, w_router_expert, b_router_expert, w1_e, w3_e, w2_e):
    assert x_prompt.shape[0] == 1 and w_in.shape[0] == 1
    seq = x_prompt.shape[1]
    nb, nq = x_sample.shape[0], x_sample.shape[1]
    past = cache_k.shape[2]
    n_s = nb * nq
    t_real = seq + N_META + n_s
    tp = -(-t_real // ROW_ALIGN) * ROW_ALIGN
    off_meta, off_s = seq, seq + N_META
    tq = 256
    assert seq % tq == 0 and nq == N_META and past % 512 == 0
    lam_init = 0.8 - 0.6 * math.exp(-0.3 * 0)
    out_scale = 1.0 - lam_init

    x_cat = jnp.concatenate([x_prompt[0], meta_tokens.astype(F32), x_sample.reshape(n_s, D_MODEL),
                             jnp.zeros((tp - t_real, D_MODEL), F32)], axis=0)

    tm = _row_tile(tp, 1088)
    tr = _row_tile(tp, 256)

    h1 = _rmsnorm(x_cat, norm1_g[0], tr)
    z = _inproj(h1, w_in[0], b_in[0], tm, 512)

    pos = jnp.concatenate([N_META + jnp.arange(seq), jnp.arange(N_META),
                           jnp.tile(past + jnp.arange(nq), nb),
                           jnp.zeros((tp - t_real,), jnp.int32)]).astype(F32)
    half = HEAD_DIM // 2
    inv = ROPE_THETA ** (-jnp.arange(half, dtype=F32) / half)
    ang = pos[:, None] * inv[None, :]
    cos_t = jnp.tile(jnp.cos(ang), (1, LANES // half))
    sin_h = jnp.sin(ang)
    sin_t = jnp.tile(jnp.concatenate([-sin_h, sin_h], axis=1), (1, LANES // HEAD_DIM))
    gq = jnp.tile(q_norm_g[0], LANES // HEAD_DIM).reshape(1, LANES)
    gk = jnp.tile(k_norm_g[0], LANES // HEAD_DIM).reshape(1, LANES)
    qb, kf, kb, vf, vb = _qk_rope(z, cos_t, sin_t, gq, gk, tr)

    lam = (jnp.exp(jnp.sum(lam_q1[0] * lam_k1[0])) - jnp.exp(jnp.sum(lam_q2[0] * lam_k2[0])) + lam_init)
    lam_row = jnp.full((1, LANES), lam, F32)
    sg = subln_g[0].reshape(1, LANES)

    q_s = qb[off_s:off_s + n_s].reshape(nb, nq, N_HEADS, LANES).transpose(0, 2, 1, 3)
    lane = jnp.arange(LANES)
    qz = jnp.concatenate([jnp.where(lane < HEAD_DIM, q_s, 0), jnp.where(lane >= HEAD_DIM, q_s, 0)], axis=2)
    pad_new = lambda a: jnp.pad(a[off_s:off_s + n_s].reshape(nb, nq, D_ATT), ((0, 0), (0, LANES - nq), (0, 0)))
    ck = cache_k[0].reshape(nb, past, D_ATT).astype(BF16)
    cv = cache_v[0].reshape(nb, past, D_ATT).astype(BF16)

    logit_bound = 8.1 * jnp.max(jnp.abs(q_norm_g[0])) * jnp.max(jnp.abs(k_norm_g[0]))

    def attention(fast):
        def run():
            if fast:
                o = _attn_prompt_fast(qb, kb, vb, lam_row, sg, seq, out_scale, 512)
            else:
                o = _attn_prompt(qb, kb, vb, lam_row, sg, seq, out_scale, tq)
            return _attn_sample(qz, ck, cv, pad_new(kb), pad_new(vb), o, lam_row, sg, off_s, out_scale,
                                1024, fast)
        return run

    o_att = lax.cond(logit_bound <= LOGIT_BOUND_MAX, attention(True), attention(False))

    gp = N_SSM_GROUPS * SSM_STATE
    s5w = _s5_weights(ssm_a_re[0], ssm_a_im[0], ssm_log_dt[0], ssm_b_re[0], ssm_b_im[0],
                      ssm_c_re[0], ssm_c_im[0])
    ys, hp_re, hp_im, hs_re, hs_im = _s5(z, s5w, ssm_d[0], state_ssm_re[0].reshape(nb, gp),
                                         state_ssm_im[0].reshape(nb, gp), seq, nb, nq)
    ysg = _glu(ys, w_glu[0], b_glu[0], tm, 512)
    m = _merge(ysg, o_att, w_ssm_proj[0], w_att_proj[0], z, tm, 512)
    x2 = _outproj(m, w_o[0], x_cat, tm, 512)

    w_r = jnp.concatenate([w_router_group[0], w_router_expert[0],
                           jnp.zeros((D_MODEL, LANES - N_EGROUPS - N_EXPERTS), F32)], axis=1).astype(BF16)
    b_r = jnp.concatenate([b_router_group[0], b_router_expert[0],
                           jnp.zeros((LANES - N_EGROUPS - N_EXPERTS,), F32)]).reshape(1, LANES)
    h2, e_sel, g_sel = _router(x2, norm2_g[0], w_r, b_r, tr)

    plan = _route_plan(e_sel[:t_real, :TOP_K], g_sel[:t_real, :TOP_K], t_real)
    y2 = _experts(h2, plan, t_real, w1_e[0], w3_e[0], w2_e[0])

    def heads(a, lead):
        return a.reshape(lead + (N_HEADS, 2, HEAD_DIM))

    y_prompt = _combine(x2, y2, 0, seq, tr).reshape(1, seq, D_MODEL)
    y_sample = _combine(x2, y2, off_s, n_s, nq).reshape(nb, nq, D_MODEL)
    k_p = jnp.concatenate([kf[off_meta:off_meta + N_META], kf[:seq]], axis=0)
    v_p = jnp.concatenate([vf[off_meta:off_meta + N_META], vf[:seq]], axis=0)
    k_prompt = heads(k_p, (1, 1, seq + N_META))
    v_prompt = v_p.reshape(1, 1, seq + N_META, N_HEADS, V_DIM)
    k_sample = heads(kf[off_s:off_s + n_s], (1, nb, nq))
    v_sample = vf[off_s:off_s + n_s].reshape(1, nb, nq, N_HEADS, V_DIM)
    st = lambda a, lead: a.reshape(lead + (N_SSM_GROUPS, SSM_STATE))
    return (y_prompt, y_sample, k_prompt, v_prompt, st(hp_re, (1, 1)), st(hp_im, (1, 1)),
            k_sample, v_sample, st(hs_re, (1, nb)), st(hs_im, (1, nb)))
```

```python
import functools
import math

import jax
import jax.numpy as jnp
from jax import lax
from jax.experimental import pallas as pl
from jax.experimental.pallas import tpu as pltpu

F32 = jnp.float32
BF16 = jnp.bfloat16

D_MODEL = 4096
N_META = 16
CHUNK = 64
N_HEADS = 16
HEAD_DIM = 64
V_DIM = 128
D_ATT = N_HEADS * V_DIM
D_SSM = 2048
SSM_GROUP = 16
N_SSM_GROUPS = D_SSM // SSM_GROUP
SSM_STATE = 64
IN_WIDTH = D_SSM + 3 * D_ATT + 2 * D_MODEL
ROPE_THETA = 10000.0
N_EGROUPS = 8
EXPERTS_PER_GROUP = 8
N_EXPERTS = N_EGROUPS * EXPERTS_PER_GROUP
TOP_K = 2
D_EXPERT = 512
MOE_BLOCK = 256
EPS = 1e-6

LANES = 128
ROW_ALIGN = 512
S5_CHUNK = 8
S5_LANE_GROUPS = LANES // SSM_GROUP
S5_TILES = D_SSM // LANES
VMEM_LIMIT = 56 * 1024 * 1024


def _cparams(sem, vmem=VMEM_LIMIT):
    return pltpu.CompilerParams(dimension_semantics=sem, vmem_limit_bytes=vmem)


def _row_tile(tp, cap):
    best = 16
    for t in range(16, cap + 1, 16):
        if tp % t == 0:
            best = t
    return best


def _dot(a, b):
    return jnp.dot(a, b, preferred_element_type=F32)


def _dot_nt(a, b):
    return lax.dot_general(a, b, (((1,), (1,)), ((), ())), preferred_element_type=F32)


def _rmsnorm_kernel(x_ref, g_ref, o_ref):
    x = x_ref[...]
    ms = jnp.mean(x * x, axis=-1, keepdims=True)
    o_ref[...] = (x * lax.rsqrt(ms + EPS) * g_ref[...]).astype(o_ref.dtype)


def _rmsnorm(x, g, tr):
    tp, d = x.shape
    return pl.pallas_call(
        _rmsnorm_kernel,
        grid=(tp // tr,),
        in_specs=[pl.BlockSpec((tr, d), lambda i: (i, 0)),
                  pl.BlockSpec((1, d), lambda i: (0, 0))],
        out_specs=pl.BlockSpec((tr, d), lambda i: (i, 0)),
        out_shape=jax.ShapeDtypeStruct((tp, d), BF16),
        compiler_params=_cparams(("parallel",)),
        name="rmsnorm1",
    )(x, g.reshape(1, d))


def _inproj_kernel(x_ref, w_ref, b_ref, o_ref):
    o_ref[...] = _dot(x_ref[...], w_ref[...].astype(BF16)) + b_ref[...]


def _inproj(h, w, b, tm, tn):
    tp, k = h.shape
    n = w.shape[1]
    return pl.pallas_call(
        _inproj_kernel,
        grid=(n // tn, tp // tm),
        in_specs=[pl.BlockSpec((tm, k), lambda j, i: (i, 0)),
                  pl.BlockSpec((k, tn), lambda j, i: (0, j)),
                  pl.BlockSpec((1, tn), lambda j, i: (0, j))],
        out_specs=pl.BlockSpec((tm, tn), lambda j, i: (i, j)),
        out_shape=jax.ShapeDtypeStruct((tp, n), F32),
        compiler_params=_cparams(("parallel", "parallel")),
        name="in_proj",
    )(h, w, b.reshape(1, n))


def _glu_kernel(x_ref, w_ref, b_ref, xe_ref, o_ref):
    a = _dot(x_ref[...].astype(BF16), w_ref[...].astype(BF16)) + b_ref[...]
    o_ref[...] = (xe_ref[...] * jax.nn.sigmoid(a)).astype(o_ref.dtype)


def _glu(ys, w, b, tm, tn):
    tp, k = ys.shape
    n = w.shape[1]
    return pl.pallas_call(
        _glu_kernel,
        grid=(n // tn, tp // tm),
        in_specs=[pl.BlockSpec((tm, k), lambda j, i: (i, 0)),
                  pl.BlockSpec((k, tn), lambda j, i: (0, j)),
                  pl.BlockSpec((1, tn), lambda j, i: (0, j)),
                  pl.BlockSpec((tm, tn), lambda j, i: (i, j))],
        out_specs=pl.BlockSpec((tm, tn), lambda j, i: (i, j)),
        out_shape=jax.ShapeDtypeStruct((tp, n), BF16),
        compiler_params=_cparams(("parallel", "parallel")),
        name="glu",
    )(ys, w, b.reshape(1, n), ys)


def _merge_kernel(ys_ref, oa_ref, ws_ref, wa_ref, gs_ref, ga_ref, o_ref):
    a = _dot(ys_ref[...], ws_ref[...].astype(BF16))
    b = _dot(oa_ref[...], wa_ref[...].astype(BF16))
    m = jax.nn.sigmoid(gs_ref[...]) * a + jax.nn.sigmoid(ga_ref[...]) * b
    o_ref[...] = m.astype(o_ref.dtype)


def _merge(ysg, oatt, w_ssm, w_att, z, tm, tn):
    tp, k = ysg.shape
    n = w_ssm.shape[1]
    gs_blk = (D_SSM + 3 * D_ATT) // tn
    ga_blk = (D_SSM + 3 * D_ATT + D_MODEL) // tn
    return pl.pallas_call(
        _merge_kernel,
        grid=(n // tn, tp // tm),
        in_specs=[pl.BlockSpec((tm, k), lambda j, i: (i, 0)),
                  pl.BlockSpec((tm, k), lambda j, i: (i, 0)),
                  pl.BlockSpec((k, tn), lambda j, i: (0, j)),
                  pl.BlockSpec((k, tn), lambda j, i: (0, j)),
                  pl.BlockSpec((tm, tn), lambda j, i: (i, gs_blk + j)),
                  pl.BlockSpec((tm, tn), lambda j, i: (i, ga_blk + j))],
        out_specs=pl.BlockSpec((tm, tn), lambda j, i: (i, j)),
        out_shape=jax.ShapeDtypeStruct((tp, n), BF16),
        compiler_params=_cparams(("parallel", "parallel")),
        name="merge_proj",
    )(ysg, oatt, w_ssm, w_att, z, z)


def _outproj_kernel(m_ref, w_ref, x_ref, o_ref):
    o_ref[...] = x_ref[...] + _dot(m_ref[...], w_ref[...].astype(BF16))


def _outproj(m, w, x, tm, tn):
    tp, k = m.shape
    n = w.shape[1]
    return pl.pallas_call(
        _outproj_kernel,
        grid=(n // tn, tp // tm),
        in_specs=[pl.BlockSpec((tm, k), lambda j, i: (i, 0)),
                  pl.BlockSpec((k, tn), lambda j, i: (0, j)),
                  pl.BlockSpec((tm, tn), lambda j, i: (i, j))],
        out_specs=pl.BlockSpec((tm, tn), lambda j, i: (i, j)),
        out_shape=jax.ShapeDtypeStruct((tp, n), F32),
        compiler_params=_cparams(("parallel", "parallel")),
        name="out_proj",
    )(m, w, x)


def _segment_sumsq(x, ones_bd):
    x2 = x * x
    hi = x2.astype(BF16)
    lo = (x2 - hi.astype(F32)).astype(BF16)
    return _dot(hi, ones_bd) + _dot(lo, ones_bd)


def _qk_rope_kernel(zq_ref, zk_ref, zv_ref, cos_ref, sin_ref, gq_ref, gk_ref, ones_ref,
                    qb_ref, kf_ref, kb_ref, vf_ref, vb_ref):
    cos = cos_ref[...]
    sin = sin_ref[...]
    ones_bd = ones_ref[...]
    lane = lax.broadcasted_iota(jnp.int32, cos.shape, 1)
    first_half = (lane % HEAD_DIM) < (HEAD_DIM // 2)

    def norm_rope(x, g):
        ss = _segment_sumsq(x, ones_bd)
        xn = x * lax.rsqrt(ss * (1.0 / HEAD_DIM) + EPS) * g
        partner = jnp.where(first_half,
                            pltpu.roll(xn, LANES - HEAD_DIM // 2, 1),
                            pltpu.roll(xn, HEAD_DIM // 2, 1))
        return xn * cos + partner * sin

    for h in range(N_HEADS):
        sl = slice(h * LANES, (h + 1) * LANES)
        q = norm_rope(zq_ref[:, sl], gq_ref[...])
        qb_ref[:, sl] = (q * (HEAD_DIM ** -0.5)).astype(BF16)
        k = norm_rope(zk_ref[:, sl], gk_ref[...])
        kf_ref[:, sl] = k
        kb_ref[:, sl] = k.astype(BF16)
    v = zv_ref[...]
    vf_ref[...] = v
    vb_ref[...] = v.astype(BF16)


def _qk_rope(z, cos_t, sin_t, gq, gk, tr):
    tp = z.shape[0]
    ones_bd = jnp.kron(jnp.eye(LANES // HEAD_DIM, dtype=F32),
                       jnp.ones((HEAD_DIM, HEAD_DIM), F32)).astype(BF16)
    zspec = lambda c: pl.BlockSpec((tr, D_ATT), lambda i: (i, c))
    row = pl.BlockSpec((tr, LANES), lambda i: (i, 0))
    const = pl.BlockSpec((1, LANES), lambda i: (0, 0))
    out = pl.BlockSpec((tr, D_ATT), lambda i: (i, 0))
    q_blk = D_SSM // D_ATT
    return pl.pallas_call(
        _qk_rope_kernel,
        grid=(tp // tr,),
        in_specs=[zspec(q_blk), zspec(q_blk + 1), zspec(q_blk + 2), row, row, const, const,
                  pl.BlockSpec((LANES, LANES), lambda i: (0, 0))],
        out_specs=[out, out, out, out, out],
        out_shape=[jax.ShapeDtypeStruct((tp, D_ATT), BF16),
                   jax.ShapeDtypeStruct((tp, D_ATT), F32),
                   jax.ShapeDtypeStruct((tp, D_ATT), BF16),
                   jax.ShapeDtypeStruct((tp, D_ATT), F32),
                   jax.ShapeDtypeStruct((tp, D_ATT), BF16)],
        compiler_params=_cparams(("parallel",)),
        name="qk_norm_rope",
    )(z, z, z, cos_t, sin_t, gq, gk, ones_bd)


def _softmax_step(c, qc, kt, vt, mask, m_ref, l_ref, acc_ref):
    s = _dot_nt(qc, kt)
    if mask is not None:
        s = jnp.where(mask, s, -jnp.inf)
    m_prev = m_ref[c]
    m_new = jnp.maximum(m_prev, jnp.max(s, axis=1, keepdims=True))
    alpha = jnp.exp(m_prev - m_new)
    p = jnp.exp(s - m_new[:, :1])
    l_ref[c] = alpha * l_ref[c] + jnp.sum(p, axis=1, keepdims=True)
    acc_ref[c] = alpha * acc_ref[c] + _dot(p.astype(BF16), vt)
    m_ref[c] = m_new


def _diff_finish(o0, o1, lam, g, out_scale):
    o = o0 - lam * o1
    ms = jnp.mean(o * o, axis=-1, keepdims=True)
    return o * lax.rsqrt(ms + EPS) * g * out_scale


def _split_components(q):
    lane = lax.broadcasted_iota(jnp.int32, q.shape, 1)
    zero = jnp.zeros_like(q)
    return jnp.where(lane < HEAD_DIM, q, zero), jnp.where(lane >= HEAD_DIM, q, zero)


def _attn_prompt_kernel(lam_ref, g_ref, q_ref, k_ref, v_ref, o_ref, m_ref, l_ref, acc_ref,
                        *, tq, nq_main, seq, out_scale):
    i = pl.program_id(1)
    q0, q1 = _split_components(q_ref[...])
    m_ref[...] = jnp.full(m_ref.shape, -jnp.inf, F32)
    l_ref[...] = jnp.zeros(l_ref.shape, F32)
    acc_ref[...] = jnp.zeros(acc_ref.shape, F32)

    def update(kt, vt, mask):
        _softmax_step(0, q0, kt, vt, mask, m_ref, l_ref, acc_ref)
        _softmax_step(1, q1, kt, vt, mask, m_ref, l_ref, acc_ref)

    col = lax.broadcasted_iota(jnp.int32, (tq, LANES), 1)
    update(k_ref[pl.ds(seq, LANES), :], v_ref[pl.ds(seq, LANES), :], col < N_META)

    is_main = i < nq_main

    def body(j, carry):
        start = pl.multiple_of(j * tq, tq)
        update(k_ref[pl.ds(start, tq), :], v_ref[pl.ds(start, tq), :], None)
        return carry

    lax.fori_loop(0, jnp.where(is_main, i, 0), body, 0)

    @pl.when(is_main)
    def _():
        start = pl.multiple_of(i * tq, tq)
        r = lax.broadcasted_iota(jnp.int32, (tq, tq), 0) // CHUNK
        c = lax.broadcasted_iota(jnp.int32, (tq, tq), 1) // CHUNK
        update(k_ref[pl.ds(start, tq), :], v_ref[pl.ds(start, tq), :], c <= r)

    o0 = acc_ref[0] / l_ref[0]
    o1 = acc_ref[1] / l_ref[1]
    o_ref[...] = _diff_finish(o0, o1, lam_ref[...], g_ref[...], out_scale).astype(o_ref.dtype)


def _attn_prompt(qb, kb, vb, lam_row, subln_g, seq, out_scale, tq):
    tp = qb.shape[0]
    kern = functools.partial(_attn_prompt_kernel, tq=tq, nq_main=seq // tq, seq=seq,
                             out_scale=out_scale)
    const = pl.BlockSpec((1, LANES), lambda h, i: (0, 0))
    return pl.pallas_call(
        kern,
        grid=(N_HEADS, tp // tq),
        in_specs=[const, const,
                  pl.BlockSpec((tq, LANES), lambda h, i: (i, h)),
                  pl.BlockSpec((tp, LANES), lambda h, i: (0, h)),
                  pl.BlockSpec((tp, LANES), lambda h, i: (0, h))],
        out_specs=pl.BlockSpec((tq, LANES), lambda h, i: (i, h)),
        out_shape=jax.ShapeDtypeStruct((tp, D_ATT), BF16),
        scratch_shapes=[pltpu.VMEM((2, tq, LANES), F32),
                        pltpu.VMEM((2, tq, LANES), F32),
                        pltpu.VMEM((2, tq, LANES), F32)],
        compiler_params=_cparams(("parallel", "parallel")),
        name="attn_prompt",
    )(lam_row, subln_g, qb, kb, vb)


LOGIT_BOUND_MAX = 40.0


def _with_ones(vt):
    return jnp.concatenate([vt, jnp.ones(vt.shape, vt.dtype)], axis=1)


def _cache_v_head(vc_ref, h):
    tk = vc_ref.shape[0] // N_HEADS
    return vc_ref[pl.ds(h, tk, stride=N_HEADS), :].astype(BF16)


def _attn_prompt_fast_kernel(lam_ref, g_ref, q_ref, k_ref, v_ref, o_ref, acc_ref,
                             *, tq, nq_main, seq, out_scale):
    i = pl.program_id(1)
    q0, q1 = _split_components(q_ref[...])
    qq = jnp.concatenate([q0, q1], axis=0)

    def scores(start, rows):
        return _dot_nt(qq, k_ref[pl.ds(start, rows), :])

    def weighted(s, start, rows, mask):
        p = jnp.exp(s)
        if mask is not None:
            p = jnp.where(mask, p, 0.0)
        return _dot(p.astype(BF16), _with_ones(v_ref[pl.ds(start, rows), :]))

    def tile_pv(start, rows, mask):
        return weighted(scores(start, rows), start, rows, mask)

    col = lax.broadcasted_iota(jnp.int32, (2 * tq, LANES), 1)
    acc_ref[...] = tile_pv(seq, LANES, col < N_META)

    is_main = i < nq_main
    n_full = jnp.where(is_main, i, 0)

    def body(j, carry):
        start = pl.multiple_of(2 * j * tq, tq)
        acc_ref[...] += tile_pv(start, 2 * tq, None)
        return carry

    lax.fori_loop(0, n_full // 2, body, 0)

    @pl.when(n_full % 2 == 1)
    def _():
        acc_ref[...] += tile_pv(pl.multiple_of((n_full - 1) * tq, tq), tq, None)

    @pl.when(is_main)
    def _():
        r = (lax.broadcasted_iota(jnp.int32, (2 * tq, tq), 0) % tq) // CHUNK
        c = lax.broadcasted_iota(jnp.int32, (2 * tq, tq), 1) // CHUNK
        acc_ref[...] += tile_pv(pl.multiple_of(i * tq, tq), tq, c <= r)

    acc = acc_ref[...]
    o0 = acc[:tq, :LANES] / acc[:tq, LANES:]
    o1 = acc[tq:, :LANES] / acc[tq:, LANES:]
    o_ref[...] = _diff_finish(o0, o1, lam_ref[...], g_ref[...], out_scale).astype(o_ref.dtype)


def _attn_prompt_fast(qb, kb, vb, lam_row, subln_g, seq, out_scale, tq):
    tp = qb.shape[0]
    kern = functools.partial(_attn_prompt_fast_kernel, tq=tq, nq_main=seq // tq, seq=seq,
                             out_scale=out_scale)
    const = pl.BlockSpec((1, LANES), lambda h, i: (0, 0))
    return pl.pallas_call(
        kern,
        grid=(N_HEADS, tp // tq),
        in_specs=[const, const,
                  pl.BlockSpec((tq, LANES), lambda h, i: (i, h)),
                  pl.BlockSpec((tp, LANES), lambda h, i: (0, h)),
                  pl.BlockSpec((tp, LANES), lambda h, i: (0, h))],
        out_specs=pl.BlockSpec((tq, LANES), lambda h, i: (i, h)),
        out_shape=jax.ShapeDtypeStruct((tp, D_ATT), BF16),
        scratch_shapes=[pltpu.VMEM((2 * tq, 2 * LANES), F32)],
        compiler_params=_cparams(("parallel", "parallel")),
        name="attn_prompt_fast",
    )(lam_row, subln_g, qb, kb, vb)


def _attn_sample_fast_kernel(lam_ref, g_ref, q_ref, kc_ref, vc_ref, kn_ref, vn_ref, o_in_ref, o_ref,
                             acc_ref, *, nq, out_scale):
    del o_in_ref
    j = pl.program_id(1)
    rows = 2 * nq

    def head_update(h, kt, vt, mask):
        p = jnp.exp(_dot_nt(q_ref[h], kt))
        if mask is not None:
            p = jnp.where(mask, p, 0.0)
        acc_ref[h] += _dot(p.astype(BF16), _with_ones(vt))

    @pl.when(j == 0)
    def _():
        acc_ref[...] = jnp.zeros(acc_ref.shape, F32)
        col = lax.broadcasted_iota(jnp.int32, (rows, LANES), 1)
        for h in range(N_HEADS):
            sl = slice(h * LANES, (h + 1) * LANES)
            head_update(h, kn_ref[:, sl], vn_ref[:, sl], col < nq)

    for h in range(N_HEADS):
        sl = slice(h * LANES, (h + 1) * LANES)
        head_update(h, kc_ref[:, sl].astype(BF16), _cache_v_head(vc_ref, h), None)

    @pl.when(j == pl.num_programs(1) - 1)
    def _():
        for h in range(N_HEADS):
            acc = acc_ref[h]
            o = acc[:, :LANES] / acc[:, LANES:]
            res = _diff_finish(o[:nq], o[nq:], lam_ref[...], g_ref[...], out_scale)
            o_ref[:, h * LANES:(h + 1) * LANES] = res.astype(o_ref.dtype)


def _attn_sample_kernel(lam_ref, g_ref, q_ref, kc_ref, vc_ref, kn_ref, vn_ref, o_in_ref, o_ref,
                        m_ref, l_ref, acc_ref, *, nq, out_scale):
    del o_in_ref
    j = pl.program_id(1)
    rows = 2 * nq

    @pl.when(j == 0)
    def _():
        m_ref[...] = jnp.full(m_ref.shape, -jnp.inf, F32)
        l_ref[...] = jnp.zeros(l_ref.shape, F32)
        acc_ref[...] = jnp.zeros(acc_ref.shape, F32)
        col = lax.broadcasted_iota(jnp.int32, (rows, LANES), 1)
        for h in range(N_HEADS):
            sl = slice(h * LANES, (h + 1) * LANES)
            _softmax_step(h, q_ref[h], kn_ref[:, sl], vn_ref[:, sl], col < nq,
                          m_ref, l_ref, acc_ref)

    for h in range(N_HEADS):
        sl = slice(h * LANES, (h + 1) * LANES)
        _softmax_step(h, q_ref[h], kc_ref[:, sl].astype(BF16), _cache_v_head(vc_ref, h), None,
                      m_ref, l_ref, acc_ref)

    @pl.when(j == pl.num_programs(1) - 1)
    def _():
        for h in range(N_HEADS):
            o = acc_ref[h] / l_ref[h]
            res = _diff_finish(o[:nq], o[nq:], lam_ref[...], g_ref[...], out_scale)
            o_ref[:, h * LANES:(h + 1) * LANES] = res.astype(o_ref.dtype)


def _attn_sample(qz, cache_k, cache_v, k_new, v_new, o_buf, lam_row, subln_g, row_off, out_scale, tk, fast):
    nb, past = cache_k.shape[0], cache_k.shape[1]
    nq = qz.shape[2] // 2
    if fast:
        kern = functools.partial(_attn_sample_fast_kernel, nq=nq, out_scale=out_scale)
        scratch = [pltpu.VMEM((N_HEADS, 2 * nq, 2 * LANES), F32)]
    else:
        kern = functools.partial(_attn_sample_kernel, nq=nq, out_scale=out_scale)
        scratch = [pltpu.VMEM((N_HEADS, 2 * nq, LANES), F32)] * 3
    const = pl.BlockSpec((1, LANES), lambda b, j: (0, 0))
    blk_off = row_off // nq
    return pl.pallas_call(
        kern,
        grid=(nb, past // tk),
        in_specs=[const, const,
                  pl.BlockSpec((None, N_HEADS, 2 * nq, LANES), lambda b, j: (b, 0, 0, 0)),
                  pl.BlockSpec((None, tk, D_ATT), lambda b, j: (b, j, 0)),
                  pl.BlockSpec((None, tk * N_HEADS, V_DIM), lambda b, j: (b, j, 0)),
                  pl.BlockSpec((None, LANES, D_ATT), lambda b, j: (b, 0, 0)),
                  pl.BlockSpec((None, LANES, D_ATT), lambda b, j: (b, 0, 0)),
                  pl.BlockSpec(memory_space=pl.ANY)],
        out_specs=pl.BlockSpec((nq, D_ATT), lambda b, j: (blk_off + b, 0)),
        out_shape=jax.ShapeDtypeStruct(o_buf.shape, o_buf.dtype),
        scratch_shapes=scratch,
        input_output_aliases={7: 0},
        compiler_params=_cparams(("parallel", "arbitrary")),
        name="attn_sample_fast" if fast else "attn_sample",
    )(lam_row, subln_g, qz, cache_k, cache_v, k_new, v_new, o_buf)


def _s5_weights(a_re, a_im, log_dt, b_re, b_im, c_re, c_im):
    hp = lax.Precision.HIGHEST
    n_t, gl, tc = S5_TILES, S5_LANE_GROUPS, S5_CHUNK
    dt = jnp.exp(log_dt)[:, None]
    mag = jnp.exp(a_re * dt)
    abar_re = mag * jnp.cos(a_im * dt)
    abar_im = mag * jnp.sin(a_im * dt)
    nr, ni = abar_re - 1.0, abar_im
    den = a_re * a_re + a_im * a_im
    coef_re = (nr * a_re + ni * a_im) / den
    coef_im = (ni * a_re - nr * a_im) / den
    bbar_re = coef_re[..., None] * b_re - coef_im[..., None] * b_im
    bbar_im = coef_re[..., None] * b_im + coef_im[..., None] * b_re
    n = jnp.arange(tc + 1, dtype=F32)[:, None, None]
    pw_mag = jnp.exp(n * (a_re * dt))
    pw_re = pw_mag * jnp.cos(n * (a_im * dt))
    pw_im = pw_mag * jnp.sin(n * (a_im * dt))
    e_re = pw_re[:tc, :, :, None] * bbar_re - pw_im[:tc, :, :, None] * bbar_im
    e_im = pw_re[:tc, :, :, None] * bbar_im + pw_im[:tc, :, :, None] * bbar_re
    kern = (jnp.einsum('gcp,lgpd->glcd', c_re, e_re, precision=hp)
            - jnp.einsum('gcp,lgpd->glcd', c_im, e_im, precision=hp))
    eye = jnp.eye(gl, dtype=F32)
    w_intra = jnp.einsum('jglcd,gh->jlgdhc', kern.reshape(n_t, gl, tc, SSM_GROUP, SSM_GROUP), eye)
    w_intra = w_intra.reshape(n_t, tc, LANES, LANES)
    eb = jnp.stack([e_re[::-1], e_im[::-1]], 0)
    eb = eb.reshape(2, tc, n_t, gl, SSM_STATE, SSM_GROUP)
    w_state = eb.transpose(2, 1, 3, 5, 0, 4).reshape(n_t, tc, LANES, 2 * SSM_STATE)
    cp_re = c_re[None] * pw_re[1:, :, None, :] - c_im[None] * pw_im[1:, :, None, :]
    cp_im = c_re[None] * pw_im[1:, :, None, :] + c_im[None] * pw_re[1:, :, None, :]
    cp = jnp.stack([cp_re, -cp_im], 0).reshape(2, tc, n_t, gl, SSM_GROUP, SSM_STATE)
    w_read = cp.transpose(2, 1, 0, 5, 3, 4).reshape(n_t, tc, 2 * SSM_STATE, LANES)
    half = gl * SSM_STATE
    a_pow = jnp.concatenate([pw_re[tc].reshape(n_t, 1, half), pw_im[tc].reshape(n_t, 1, half)], -1)
    rp = jnp.arange(2 * SSM_STATE)
    col = jnp.arange(2 * half)
    spread = ((rp[:, None] // SSM_STATE == col[None, :] // half)
              & (rp[:, None] % SSM_STATE == col[None, :] % SSM_STATE)).astype(BF16)
    return (w_intra.astype(BF16), w_state.astype(BF16), w_read.astype(BF16), a_pow, spread, spread.T)


def _s5_kernel(u_ref, wi_ref, wsc_ref, wrc_ref, ap_ref, sp_ref, spt_ref, d_ref, h0r_ref, h0i_ref,
               y_ref, hpr_ref, hpi_ref, hsr_ref, hsi_ref,
               y_acc, v_ref, hs_ref, wt_ref, ws_ref, wr_ref,
               *, nc, n_main, n_meta_chunks, n_seq, seq_chunks):
    tc = S5_CHUNK
    half = hs_ref.shape[1] // 2
    grp_r = lax.broadcasted_iota(jnp.int32, (LANES, 2 * half), 0) // SSM_GROUP
    grp_c = (lax.broadcasted_iota(jnp.int32, (LANES, 2 * half), 1) % half) // SSM_STATE
    for s in range(tc):
        full = _dot(wsc_ref[s], sp_ref[...])
        ws_ref[s * LANES:(s + 1) * LANES, :] = jnp.where(grp_r == grp_c, full, 0.0).astype(BF16)
    grp_r = (lax.broadcasted_iota(jnp.int32, (2 * half, LANES), 0) % half) // SSM_STATE
    grp_c = lax.broadcasted_iota(jnp.int32, (2 * half, LANES), 1) // SSM_GROUP
    for t in range(tc):
        full = _dot(spt_ref[...], wrc_ref[t])
        wr_ref[:, t * LANES:(t + 1) * LANES] = jnp.where(grp_r == grp_c, full, 0.0).astype(BF16)
    for s in range(tc):
        for t in range(tc):
            blk = wi_ref[t - s] if t >= s else jnp.zeros((LANES, LANES), BF16)
            wt_ref[s * LANES:(s + 1) * LANES, t * LANES:(t + 1) * LANES] = blk
    lhs = jnp.concatenate(
        [u_ref[pl.ds(s, nc, stride=tc), :].astype(BF16) for s in range(tc)], axis=1)
    y_acc[...] = _dot(lhs, wt_ref[...])
    v_ref[...] = _dot(lhs, ws_ref[...])
    a_re = ap_ref[:, :half]
    a_im = ap_ref[:, half:]

    def advance(h_re, h_im, v):
        return (a_re * h_re - a_im * h_im + v[:, :half],
                a_re * h_im + a_im * h_re + v[:, half:])

    hs_ref[...] = jnp.zeros(hs_ref.shape, F32)

    h_re = jnp.zeros((1, half), F32)
    h_im = jnp.zeros((1, half), F32)
    for c in range(n_main, n_main + n_meta_chunks):
        hs_ref[pl.ds(c, 1), :] = jnp.concatenate([h_re, h_im], axis=1)
        h_re, h_im = advance(h_re, h_im, v_ref[pl.ds(c, 1), :])

    def body(c, carry):
        h_re, h_im = carry
        hs_ref[pl.ds(c, 1), :] = jnp.concatenate([h_re, h_im], axis=1)
        return advance(h_re, h_im, v_ref[pl.ds(c, 1), :])

    h_re, h_im = lax.fori_loop(0, n_main, body, (h_re, h_im))
    hpr_ref[...] = h_re
    hpi_ref[...] = h_im

    base = n_main + n_meta_chunks
    for b in range(n_seq):
        s_re = h0r_ref[pl.ds(b, 1), :]
        s_im = h0i_ref[pl.ds(b, 1), :]
        for c in range(base + b * seq_chunks, base + (b + 1) * seq_chunks):
            hs_ref[pl.ds(c, 1), :] = jnp.concatenate([s_re, s_im], axis=1)
            s_re, s_im = advance(s_re, s_im, v_ref[pl.ds(c, 1), :])
        hsr_ref[pl.ds(b, 1), :] = s_re
        hsi_ref[pl.ds(b, 1), :] = s_im

    y_acc[...] += _dot(hs_ref[...].astype(BF16), wr_ref[...])
    d = d_ref[...]
    for t in range(tc):
        rows = pl.ds(t, nc, stride=tc)
        y = y_acc[:, t * LANES:(t + 1) * LANES] + d * u_ref[rows, :]
        y_ref[rows, :] = jax.nn.gelu(y)


def _s5(z, weights, d_skip, h0_re, h0_im, seq, n_seq, seq_len):
    tp = z.shape[0]
    tc = S5_CHUNK
    nc = tp // tc
    w_intra, w_state, w_read, a_pow, spread, spread_t = weights
    half = S5_LANE_GROUPS * SSM_STATE
    whole = lambda a: pl.BlockSpec(a.shape, lambda j: (0,) * a.ndim)
    kern = functools.partial(_s5_kernel, nc=nc, n_main=seq // tc, n_meta_chunks=N_META // tc,
                             n_seq=n_seq, seq_chunks=seq_len // tc)
    wspec = lambda a: pl.BlockSpec((None,) + a.shape[1:], lambda j: (j,) + (0,) * (a.ndim - 1))
    col = pl.BlockSpec((tp, LANES), lambda j: (0, j))
    st = lambda r: pl.BlockSpec((r, half), lambda j: (0, j))
    gp = N_SSM_GROUPS * SSM_STATE
    return pl.pallas_call(
        kern,
        grid=(S5_TILES,),
        in_specs=[col, wspec(w_intra), wspec(w_state), wspec(w_read), wspec(a_pow),
                  whole(spread), whole(spread_t),
                  pl.BlockSpec((1, LANES), lambda j: (0, j)), st(n_seq), st(n_seq)],
        out_specs=[col, st(1), st(1), st(n_seq), st(n_seq)],
        out_shape=[jax.ShapeDtypeStruct((tp, D_SSM), F32),
                   jax.ShapeDtypeStruct((1, gp), F32),
                   jax.ShapeDtypeStruct((1, gp), F32),
                   jax.ShapeDtypeStruct((n_seq, gp), F32),
                   jax.ShapeDtypeStruct((n_seq, gp), F32)],
        scratch_shapes=[pltpu.VMEM((nc, tc * LANES), F32),
                        pltpu.VMEM((nc, 2 * half), F32),
                        pltpu.VMEM((nc, 2 * half), F32),
                        pltpu.VMEM((tc * LANES, tc * LANES), BF16),
                        pltpu.VMEM((tc * LANES, 2 * half), BF16),
                        pltpu.VMEM((2 * half, tc * LANES), BF16)],
        compiler_params=_cparams(("parallel",)),
        name="s5_scan",
    )(z, w_intra, w_state, w_read, a_pow, spread, spread_t, d_skip.reshape(1, D_SSM), h0_re, h0_im)


def _router_kernel(x_ref, g_ref, w_ref, b_ref, h_ref, e_ref, gate_ref):
    x = x_ref[...]
    ms = jnp.mean(x * x, axis=-1, keepdims=True)
    h = x * lax.rsqrt(ms + EPS) * g_ref[...]
    h_ref[...] = h.astype(h_ref.dtype)
    logits = _dot(h.astype(BF16), w_ref[...]) + b_ref[...]
    lane = lax.broadcasted_iota(jnp.int32, logits.shape, 1)
    neg = -jnp.inf
    big = jnp.int32(LANES)

    def first_argmax(vals, vmax):
        return jnp.min(jnp.where(vals == vmax, lane, big), axis=1, keepdims=True)

    lg = jnp.where(lane < N_EGROUPS, logits, neg)
    mg = jnp.max(lg, axis=1, keepdims=True)
    sg = jnp.sum(jnp.exp(lg - mg), axis=1, keepdims=True)
    g_w = 1.0 / sg
    g_idx = first_argmax(lg, mg)
    lo = N_EGROUPS + EXPERTS_PER_GROUP * g_idx
    le = jnp.where((lane >= lo) & (lane < lo + EXPERTS_PER_GROUP), logits, neg)
    m1 = jnp.max(le, axis=1, keepdims=True)
    se = jnp.sum(jnp.exp(le - m1), axis=1, keepdims=True)
    i1 = first_argmax(le, m1)
    le2 = jnp.where(lane == i1, neg, le)
    m2 = jnp.max(le2, axis=1, keepdims=True)
    i2 = first_argmax(le2, m2)
    p1 = 1.0 / se
    p2 = jnp.exp(m2 - m1) / se
    tot = p1 + p2
    w1 = g_w * (p1 / tot)
    w2 = g_w * (p2 / tot)
    e_ref[...] = jnp.where(lane == 0, i1 - N_EGROUPS, jnp.where(lane == 1, i2 - N_EGROUPS, 0))
    gate_ref[...] = jnp.where(lane == 0, w1, jnp.where(lane == 1, w2, 0.0))


def _router(x2, g, w_r, b_r, tr):
    tp, d = x2.shape
    return pl.pallas_call(
        _router_kernel,
        grid=(tp // tr,),
        in_specs=[pl.BlockSpec((tr, d), lambda i: (i, 0)),
                  pl.BlockSpec((1, d), lambda i: (0, 0)),
                  pl.BlockSpec((d, LANES), lambda i: (0, 0)),
                  pl.BlockSpec((1, LANES), lambda i: (0, 0))],
        out_specs=[pl.BlockSpec((tr, d), lambda i: (i, 0)),
                   pl.BlockSpec((tr, LANES), lambda i: (i, 0)),
                   pl.BlockSpec((tr, LANES), lambda i: (i, 0))],
        out_shape=[jax.ShapeDtypeStruct((tp, d), F32),
                   jax.ShapeDtypeStruct((tp, LANES), jnp.int32),
                   jax.ShapeDtypeStruct((tp, LANES), F32)],
        compiler_params=_cparams(("parallel",)),
        name="norm2_router",
    )(x2, g.reshape(1, d), w_r, b_r)


def _expert_kernel(be_ref, nu_ref, cnt_ref, tok_ref, tokn_ref, dst_ref, h_hbm, g_ref, w1_ref, w3_ref,
                   w2_ref, y_hbm, xbuf, ybuf, gsem, ssem, *, nb):
    del be_ref
    b = pl.program_id(0)
    hh = pl.program_id(1)
    n_used = nu_ref[0]
    active = b < n_used
    slot = b % 2
    other = 1 - slot

    def gather(idx_ref, r, s):
        return pltpu.make_async_copy(h_hbm.at[pl.ds(idx_ref[0, r], 1), :],
                                     xbuf.at[s, pl.ds(r, 1), :], gsem.at[s])

    def scatter(r, d, s):
        return pltpu.make_async_copy(ybuf.at[s, pl.ds(r, 1), :], y_hbm.at[pl.ds(d, 1), :], ssem.at[s])

    def start_gather(idx_ref, n, s):
        def body(r, c):
            gather(idx_ref, r, s).start()
            return c
        lax.fori_loop(0, n, body, 0)

    def wait_gather(n, s):
        def body(r, c):
            gather(tok_ref, 0, s).wait()
            return c
        lax.fori_loop(0, n, body, 0)

    def wait_scatter(n, s):
        def body(r, c):
            scatter(0, 0, s).wait()
            return c
        lax.fori_loop(0, n, body, 0)

    @pl.when((b == 0) & (hh == 0))
    def _():
        xbuf[...] = jnp.zeros(xbuf.shape, F32)

        @pl.when(active)
        def _():
            start_gather(tok_ref, cnt_ref[0], 0)

    @pl.when(active & (hh == 0))
    def _():
        wait_gather(cnt_ref[b], slot)

    @pl.when(active & (hh == 1) & (b + 1 < n_used))
    def _():
        start_gather(tokn_ref, cnt_ref[jnp.minimum(b + 1, nb - 1)], other)

    @pl.when(active)
    def _():
        x = xbuf[slot].astype(BF16)
        a = _dot(x, w1_ref[...].astype(BF16))
        c = _dot(x, w3_ref[...].astype(BF16))
        hid = (jax.nn.silu(a) * c).astype(BF16)
        y = _dot(hid, w2_ref[...].astype(BF16)) * g_ref[...]

        @pl.when(hh == 0)
        def _():
            ybuf[slot] = y

        @pl.when(hh == 1)
        def _():
            ybuf[slot] += y

    @pl.when(active & (hh == 1))
    def _():
        def body(r, c):
            scatter(r, dst_ref[0, r], slot).start()
            return c
        lax.fori_loop(0, cnt_ref[b], body, 0)

        @pl.when(b > 0)
        def _():
            wait_scatter(cnt_ref[jnp.maximum(b - 1, 0)], other)

        @pl.when(b + 1 >= n_used)
        def _():
            wait_scatter(cnt_ref[b], slot)


def _experts(h2, plan, t_real, w1, w3, w2):
    slot_tok, slot_dst, slot_gate, block_expert, n_used, cnt = plan
    d = h2.shape[1]
    nb = block_expert.shape[0]
    dh = D_EXPERT // 2

    def half(b, hh):
        return (hh + b) % 2

    def eidx(b, be, nu):
        return be[jnp.minimum(b, nu[0] - 1)]

    smem_row = pl.BlockSpec((None, 1, MOE_BLOCK), lambda b, hh, be, nu, cnt: (b, 0, 0),
                            memory_space=pltpu.SMEM)
    smem_next = pl.BlockSpec((None, 1, MOE_BLOCK),
                             lambda b, hh, be, nu, cnt: (jnp.minimum(b + 1, nb - 1), 0, 0),
                             memory_space=pltpu.SMEM)
    tok3 = slot_tok.reshape(nb, 1, MOE_BLOCK)
    grid_spec = pltpu.PrefetchScalarGridSpec(
        num_scalar_prefetch=3,
        grid=(nb, 2),
        in_specs=[smem_row, smem_next, smem_row,
                  pl.BlockSpec(memory_space=pl.ANY),
                  pl.BlockSpec((MOE_BLOCK, 1), lambda b, hh, be, nu, cnt: (b, 0)),
                  pl.BlockSpec((None, d, dh), lambda b, hh, be, nu, cnt: (eidx(b, be, nu), 0, half(b, hh))),
                  pl.BlockSpec((None, d, dh), lambda b, hh, be, nu, cnt: (eidx(b, be, nu), 0, half(b, hh))),
                  pl.BlockSpec((None, dh, d), lambda b, hh, be, nu, cnt: (eidx(b, be, nu), half(b, hh), 0))],
        out_specs=pl.BlockSpec(memory_space=pl.ANY),
        scratch_shapes=[pltpu.VMEM((2, MOE_BLOCK, d), F32),
                        pltpu.VMEM((2, MOE_BLOCK, d), F32),
                        pltpu.SemaphoreType.DMA((2,)),
                        pltpu.SemaphoreType.DMA((2,))],
    )
    return pl.pallas_call(
        functools.partial(_expert_kernel, nb=nb),
        grid_spec=grid_spec,
        out_shape=jax.ShapeDtypeStruct((TOP_K * t_real, d), F32),
        compiler_params=_cparams(("arbitrary", "arbitrary")),
        name="expert_mlp",
    )(block_expert, n_used, cnt, tok3, tok3, slot_dst.reshape(nb, 1, MOE_BLOCK),
      h2, slot_gate.reshape(-1, 1), w1, w3, w2)


def _route_plan(expert, gate, t_real):
    s = t_real * TOP_K
    n_blocks = -(-(s + N_EXPERTS * (MOE_BLOCK - 1)) // MOE_BLOCK)
    p = n_blocks * MOE_BLOCK
    flat_e = expert.reshape(-1).astype(jnp.int32)
    flat_g = gate.reshape(-1).astype(F32)
    order = jnp.argsort(flat_e).astype(jnp.int32)
    se = flat_e[order]
    bounds = jnp.searchsorted(se, jnp.arange(N_EXPERTS + 1, dtype=jnp.int32)).astype(jnp.int32)
    start = bounds[:-1]
    counts = bounds[1:] - start
    padded = (counts + MOE_BLOCK - 1) // MOE_BLOCK * MOE_BLOCK
    pad_end = jnp.cumsum(padded)
    pad_start = pad_end - padded
    block_start = jnp.arange(n_blocks, dtype=jnp.int32) * MOE_BLOCK
    block_expert = jnp.minimum(jnp.searchsorted(pad_end, block_start, side='right'),
                               N_EXPERTS - 1).astype(jnp.int32)
    n_used = (pad_end[-1] // MOE_BLOCK).astype(jnp.int32).reshape(1)
    cnt = jnp.clip(counts[block_expert] - (block_start - pad_start[block_expert]), 0, MOE_BLOCK)
    e_slot = jnp.repeat(block_expert, MOE_BLOCK)
    off = jnp.arange(p, dtype=jnp.int32) - pad_start[e_slot]
    valid = off < counts[e_slot]
    flat = order[jnp.clip(start[e_slot] + off, 0, s - 1)]
    tok = flat // TOP_K
    slot_tok = jnp.where(valid, tok, 0)
    slot_dst = jnp.where(valid, (flat % TOP_K) * t_real + tok, 0)
    slot_gate = jnp.where(valid, flat_g[flat], 0.0)
    return slot_tok, slot_dst, slot_gate, block_expert, n_used, cnt.astype(jnp.int32)


def _combine_kernel(x_ref, y0_ref, y1_ref, o_ref):
    o_ref[...] = x_ref[...] + y0_ref[...] + y1_ref[...]


def _combine(x2, y2, row_off, n_rows, tile):
    d = x2.shape[1]
    off = row_off // tile
    y3 = y2.reshape(TOP_K, y2.shape[0] // TOP_K, d)
    return pl.pallas_call(
        _combine_kernel,
        grid=(n_rows // tile,),
        in_specs=[pl.BlockSpec((tile, d), lambda i: (off + i, 0)),
                  pl.BlockSpec((None, tile, d), lambda i: (0, off + i, 0)),
                  pl.BlockSpec((None, tile, d), lambda i: (1, off + i, 0))],
        out_specs=pl.BlockSpec((tile, d), lambda i: (i, 0)),
        out_shape=jax.ShapeDtypeStruct((n_rows, d), F32),
        compiler_params=_cparams(("parallel",)),
        name="moe_combine",
    )(x2, y3, y3)


def kernel(x_prompt, x_sample, cache_k, cache_v, state_ssm_re, state_ssm_im, meta_tokens, norm1_g, w_in, b_in, ssm_a_re, ssm_a_im, ssm_log_dt, ssm_b_re, ssm_b_im, ssm_c_re, ssm_c_im, ssm_d, w_glu, b_glu, w_ssm_proj, q_norm_g, k_norm_g, lam_q1, lam_k1, lam_q2, lam_k2, subln_g, w_att_proj, w_o, norm2_g, w_router_group, b_router_group, w_router_expert, b_router_expert, w1_e, w3_e, w2_e):
    assert x_prompt.shape[0] == 1 and w_in.shape[0] == 1
    seq = x_prompt.shape[1]
    nb, nq = x_sample.shape[0], x_sample.shape[1]
    past = cache_k.shape[2]
    n_s = nb * nq
    t_real = seq + N_META + n_s
    tp = -(-t_real // ROW_ALIGN) * ROW_ALIGN
    off_meta, off_s = seq, seq + N_META
    tq = 256
    assert seq % tq == 0 and nq == N_META and past % 512 == 0
    lam_init = 0.8 - 0.6 * math.exp(-0.3 * 0)
    out_scale = 1.0 - lam_init

    x_cat = jnp.concatenate([x_prompt[0], meta_tokens.astype(F32), x_sample.reshape(n_s, D_MODEL),
                             jnp.zeros((tp - t_real, D_MODEL), F32)], axis=0)

    tm = _row_tile(tp, 1088)
    tr = _row_tile(tp, 256)

    h1 = _rmsnorm(x_cat, norm1_g[0], tr)
    z = _inproj(h1, w_in[0], b_in[0], tm, 512)

    pos = jnp.concatenate([N_META + jnp.arange(seq), jnp.arange(N_META),
                           jnp.tile(past + jnp.arange(nq), nb),
                           jnp.zeros((tp - t_real,), jnp.int32)]).astype(F32)
    half = HEAD_DIM // 2
    inv = ROPE_THETA ** (-jnp.arange(half, dtype=F32) / half)
    ang = pos[:, None] * inv[None, :]
    cos_t = jnp.tile(jnp.cos(ang), (1, LANES // half))
    sin_h = jnp.sin(ang)
    sin_t = jnp.tile(jnp.concatenate([-sin_h, sin_h], axis=1), (1, LANES // HEAD_DIM))
    gq = jnp.tile(q_norm_g[0], LANES // HEAD_DIM).reshape(1, LANES)
    gk = jnp.tile(k_norm_g[0], LANES // HEAD_DIM).reshape(1, LANES)
    qb, kf, kb, vf, vb = _qk_rope(z, cos_t, sin_t, gq, gk, tr)

    lam = (jnp.exp(jnp.sum(lam_q1[0] * lam_k1[0])) - jnp.exp(jnp.sum(lam_q2[0] * lam_k2[0])) + lam_init)
    lam_row = jnp.full((1, LANES), lam, F32)
    sg = subln_g[0].reshape(1, LANES)

    q_s = qb[off_s:off_s + n_s].reshape(nb, nq, N_HEADS, LANES).transpose(0, 2, 1, 3)
    lane = jnp.arange(LANES)
    qz = jnp.concatenate([jnp.where(lane < HEAD_DIM, q_s, 0), jnp.where(lane >= HEAD_DIM, q_s, 0)], axis=2)
    pad_new = lambda a: jnp.pad(a[off_s:off_s + n_s].reshape(nb, nq, D_ATT), ((0, 0), (0, LANES - nq), (0, 0)))
    ck = cache_k[0].reshape(nb, past, D_ATT)
    cv = cache_v[0].reshape(nb, past * N_HEADS, V_DIM)

    logit_bound = 8.1 * jnp.max(jnp.abs(q_norm_g[0])) * jnp.max(jnp.abs(k_norm_g[0]))

    def attention(fast):
        def run():
            if fast:
                o = _attn_prompt_fast(qb, kb, vb, lam_row, sg, seq, out_scale, 512)
            else:
                o = _attn_prompt(qb, kb, vb, lam_row, sg, seq, out_scale, tq)
            return _attn_sample(qz, ck, cv, pad_new(kb), pad_new(vb), o, lam_row, sg, off_s, out_scale,
                                512, fast)
        return run

    o_att = lax.cond(logit_bound <= LOGIT_BOUND_MAX, attention(True), attention(False))

    gp = N_SSM_GROUPS * SSM_STATE
    s5w = _s5_weights(ssm_a_re[0], ssm_a_im[0], ssm_log_dt[0], ssm_b_re[0], ssm_b_im[0],
                      ssm_c_re[0], ssm_c_im[0])
    ys, hp_re, hp_im, hs_re, hs_im = _s5(z, s5w, ssm_d[0], state_ssm_re[0].reshape(nb, gp),
                                         state_ssm_im[0].reshape(nb, gp), seq, nb, nq)
    ysg = _glu(ys, w_glu[0], b_glu[0], tm, 512)
    m = _merge(ysg, o_att, w_ssm_proj[0], w_att_proj[0], z, tm, 512)
    x2 = _outproj(m, w_o[0], x_cat, tm, 512)

    w_r = jnp.concatenate([w_router_group[0], w_router_expert[0],
                           jnp.zeros((D_MODEL, LANES - N_EGROUPS - N_EXPERTS), F32)], axis=1).astype(BF16)
    b_r = jnp.concatenate([b_router_group[0], b_router_expert[0],
                           jnp.zeros((LANES - N_EGROUPS - N_EXPERTS,), F32)]).reshape(1, LANES)
    h2, e_sel, g_sel = _router(x2, norm2_g[0], w_r, b_r, tr)

    plan = _route_plan(e_sel[:t_real, :TOP_K], g_sel[:t_real, :TOP_K], t_real)
    y2 = _experts(h2, plan, t_real, w1_e[0], w3_e[0], w2_e[0])

    def heads(a, lead):
        return a.reshape(lead + (N_HEADS, 2, HEAD_DIM))

    y_prompt = _combine(x2, y2, 0, seq, tr).reshape(1, seq, D_MODEL)
    y_sample = _combine(x2, y2, off_s, n_s, nq).reshape(nb, nq, D_MODEL)
    k_p = jnp.concatenate([kf[off_meta:off_meta + N_META], kf[:seq]], axis=0)
    v_p = jnp.concatenate([vf[off_meta:off_meta + N_META], vf[:seq]], axis=0)
    k_prompt = heads(k_p, (1, 1, seq + N_META))
    v_prompt = v_p.reshape(1, 1, seq + N_META, N_HEADS, V_DIM)
    k_sample = heads(kf[off_s:off_s + n_s], (1, nb, nq))
    v_sample = vf[off_s:off_s + n_s].reshape(1, nb, nq, N_HEADS, V_DIM)
    st = lambda a, lead: a.reshape(lead + (N_SSM_GROUPS, SSM_STATE))
    return (y_prompt, y_sample, k_prompt, v_prompt, st(hp_re, (1, 1)), st(hp_im, (1, 1)),
            k_sample, v_sample, st(hs_re, (1, nb)), st(hs_im, (1, nb)))
```

```python
import functools
import math

import jax
import jax.numpy as jnp
from jax import lax
from jax.experimental import pallas as pl
from jax.experimental.pallas import tpu as pltpu

F32 = jnp.float32
BF16 = jnp.bfloat16

D_MODEL = 4096
N_META = 16
CHUNK = 64
N_HEADS = 16
HEAD_DIM = 64
V_DIM = 128
D_ATT = N_HEADS * V_DIM
D_SSM = 2048
SSM_GROUP = 16
N_SSM_GROUPS = D_SSM // SSM_GROUP
SSM_STATE = 64
IN_WIDTH = D_SSM + 3 * D_ATT + 2 * D_MODEL
ROPE_THETA = 10000.0
N_EGROUPS = 8
EXPERTS_PER_GROUP = 8
N_EXPERTS = N_EGROUPS * EXPERTS_PER_GROUP
TOP_K = 2
D_EXPERT = 512
MOE_BLOCK = 384
EXPERT_SLICES = 4
EPS = 1e-6

LANES = 128
ROW_ALIGN = 512
S5_CHUNK = 8
S5_LANE_GROUPS = LANES // SSM_GROUP
S5_TILES = D_SSM // LANES
VMEM_LIMIT = 56 * 1024 * 1024


def _cparams(sem, vmem=VMEM_LIMIT):
    return pltpu.CompilerParams(dimension_semantics=sem, vmem_limit_bytes=vmem)


def _row_tile(tp, cap):
    best = 16
    for t in range(16, cap + 1, 16):
        if tp % t == 0:
            best = t
    return best


def _dot(a, b):
    return jnp.dot(a, b, preferred_element_type=F32)


def _dot_nt(a, b):
    return lax.dot_general(a, b, (((1,), (1,)), ((), ())), preferred_element_type=F32)


def _rmsnorm_kernel(x_ref, g_ref, o_ref):
    x = x_ref[...]
    ms = jnp.mean(x * x, axis=-1, keepdims=True)
    o_ref[...] = (x * lax.rsqrt(ms + EPS) * g_ref[...]).astype(o_ref.dtype)


def _rmsnorm(x, g, tr):
    tp, d = x.shape
    return pl.pallas_call(
        _rmsnorm_kernel,
        grid=(tp // tr,),
        in_specs=[pl.BlockSpec((tr, d), lambda i: (i, 0)),
                  pl.BlockSpec((1, d), lambda i: (0, 0))],
        out_specs=pl.BlockSpec((tr, d), lambda i: (i, 0)),
        out_shape=jax.ShapeDtypeStruct((tp, d), BF16),
        compiler_params=_cparams(("parallel",)),
        name="rmsnorm1",
    )(x, g.reshape(1, d))


def _inproj_kernel(x_ref, w_ref, b_ref, o_ref):
    o_ref[...] = _dot(x_ref[...], w_ref[...].astype(BF16)) + b_ref[...]


def _inproj(h, w, b, tm, tn):
    tp, k = h.shape
    n = w.shape[1]
    return pl.pallas_call(
        _inproj_kernel,
        grid=(n // tn, tp // tm),
        in_specs=[pl.BlockSpec((tm, k), lambda j, i: (i, 0)),
                  pl.BlockSpec((k, tn), lambda j, i: (0, j)),
                  pl.BlockSpec((1, tn), lambda j, i: (0, j))],
        out_specs=pl.BlockSpec((tm, tn), lambda j, i: (i, j)),
        out_shape=jax.ShapeDtypeStruct((tp, n), F32),
        compiler_params=_cparams(("parallel", "parallel")),
        name="in_proj",
    )(h, w, b.reshape(1, n))


def _glu_kernel(x_ref, w_ref, b_ref, xe_ref, o_ref):
    a = _dot(x_ref[...].astype(BF16), w_ref[...].astype(BF16)) + b_ref[...]
    o_ref[...] = (xe_ref[...] * jax.nn.sigmoid(a)).astype(o_ref.dtype)


def _glu(ys, w, b, tm, tn):
    tp, k = ys.shape
    n = w.shape[1]
    return pl.pallas_call(
        _glu_kernel,
        grid=(n // tn, tp // tm),
        in_specs=[pl.BlockSpec((tm, k), lambda j, i: (i, 0)),
                  pl.BlockSpec((k, tn), lambda j, i: (0, j)),
                  pl.BlockSpec((1, tn), lambda j, i: (0, j)),
                  pl.BlockSpec((tm, tn), lambda j, i: (i, j))],
        out_specs=pl.BlockSpec((tm, tn), lambda j, i: (i, j)),
        out_shape=jax.ShapeDtypeStruct((tp, n), BF16),
        compiler_params=_cparams(("parallel", "parallel")),
        name="glu",
    )(ys, w, b.reshape(1, n), ys)


def _merge_kernel(ys_ref, oa_ref, ws_ref, wa_ref, gs_ref, ga_ref, o_ref):
    a = _dot(ys_ref[...], ws_ref[...].astype(BF16))
    b = _dot(oa_ref[...], wa_ref[...].astype(BF16))
    m = jax.nn.sigmoid(gs_ref[...]) * a + jax.nn.sigmoid(ga_ref[...]) * b
    o_ref[...] = m.astype(o_ref.dtype)


def _merge(ysg, oatt, w_ssm, w_att, z, tm, tn):
    tp, k = ysg.shape
    n = w_ssm.shape[1]
    gs_blk = (D_SSM + 3 * D_ATT) // tn
    ga_blk = (D_SSM + 3 * D_ATT + D_MODEL) // tn
    return pl.pallas_call(
        _merge_kernel,
        grid=(n // tn, tp // tm),
        in_specs=[pl.BlockSpec((tm, k), lambda j, i: (i, 0)),
                  pl.BlockSpec((tm, k), lambda j, i: (i, 0)),
                  pl.BlockSpec((k, tn), lambda j, i: (0, j)),
                  pl.BlockSpec((k, tn), lambda j, i: (0, j)),
                  pl.BlockSpec((tm, tn), lambda j, i: (i, gs_blk + j)),
                  pl.BlockSpec((tm, tn), lambda j, i: (i, ga_blk + j))],
        out_specs=pl.BlockSpec((tm, tn), lambda j, i: (i, j)),
        out_shape=jax.ShapeDtypeStruct((tp, n), BF16),
        compiler_params=_cparams(("parallel", "parallel")),
        name="merge_proj",
    )(ysg, oatt, w_ssm, w_att, z, z)


def _outproj_kernel(m_ref, w_ref, x_ref, o_ref):
    o_ref[...] = x_ref[...] + _dot(m_ref[...], w_ref[...].astype(BF16))


def _outproj(m, w, x, tm, tn):
    tp, k = m.shape
    n = w.shape[1]
    return pl.pallas_call(
        _outproj_kernel,
        grid=(n // tn, tp // tm),
        in_specs=[pl.BlockSpec((tm, k), lambda j, i: (i, 0)),
                  pl.BlockSpec((k, tn), lambda j, i: (0, j)),
                  pl.BlockSpec((tm, tn), lambda j, i: (i, j))],
        out_specs=pl.BlockSpec((tm, tn), lambda j, i: (i, j)),
        out_shape=jax.ShapeDtypeStruct((tp, n), F32),
        compiler_params=_cparams(("parallel", "parallel")),
        name="out_proj",
    )(m, w, x)


def _segment_sumsq(x, ones_bd):
    x2 = x * x
    hi = x2.astype(BF16)
    lo = (x2 - hi.astype(F32)).astype(BF16)
    return _dot(hi, ones_bd) + _dot(lo, ones_bd)


def _qk_rope_kernel(zq_ref, zk_ref, zv_ref, cos_ref, sin_ref, gq_ref, gk_ref, ones_ref,
                    qb_ref, kf_ref, kb_ref, vf_ref, vb_ref):
    cos = cos_ref[...]
    sin = sin_ref[...]
    ones_bd = ones_ref[...]
    lane = lax.broadcasted_iota(jnp.int32, cos.shape, 1)
    first_half = (lane % HEAD_DIM) < (HEAD_DIM // 2)

    def norm_rope(x, g):
        ss = _segment_sumsq(x, ones_bd)
        xn = x * lax.rsqrt(ss * (1.0 / HEAD_DIM) + EPS) * g
        partner = jnp.where(first_half,
                            pltpu.roll(xn, LANES - HEAD_DIM // 2, 1),
                            pltpu.roll(xn, HEAD_DIM // 2, 1))
        return xn * cos + partner * sin

    for h in range(N_HEADS):
        sl = slice(h * LANES, (h + 1) * LANES)
        q = norm_rope(zq_ref[:, sl], gq_ref[...])
        qb_ref[:, sl] = (q * (HEAD_DIM ** -0.5)).astype(BF16)
        k = norm_rope(zk_ref[:, sl], gk_ref[...])
        kf_ref[:, sl] = k
        kb_ref[:, sl] = k.astype(BF16)
    v = zv_ref[...]
    vf_ref[...] = v
    vb_ref[...] = v.astype(BF16)


def _qk_rope(z, cos_t, sin_t, gq, gk, tr):
    tp = z.shape[0]
    ones_bd = jnp.kron(jnp.eye(LANES // HEAD_DIM, dtype=F32),
                       jnp.ones((HEAD_DIM, HEAD_DIM), F32)).astype(BF16)
    zspec = lambda c: pl.BlockSpec((tr, D_ATT), lambda i: (i, c))
    row = pl.BlockSpec((tr, LANES), lambda i: (i, 0))
    const = pl.BlockSpec((1, LANES), lambda i: (0, 0))
    out = pl.BlockSpec((tr, D_ATT), lambda i: (i, 0))
    q_blk = D_SSM // D_ATT
    return pl.pallas_call(
        _qk_rope_kernel,
        grid=(tp // tr,),
        in_specs=[zspec(q_blk), zspec(q_blk + 1), zspec(q_blk + 2), row, row, const, const,
                  pl.BlockSpec((LANES, LANES), lambda i: (0, 0))],
        out_specs=[out, out, out, out, out],
        out_shape=[jax.ShapeDtypeStruct((tp, D_ATT), BF16),
                   jax.ShapeDtypeStruct((tp, D_ATT), F32),
                   jax.ShapeDtypeStruct((tp, D_ATT), BF16),
                   jax.ShapeDtypeStruct((tp, D_ATT), F32),
                   jax.ShapeDtypeStruct((tp, D_ATT), BF16)],
        compiler_params=_cparams(("parallel",)),
        name="qk_norm_rope",
    )(z, z, z, cos_t, sin_t, gq, gk, ones_bd)


def _softmax_step(c, qc, kt, vt, mask, m_ref, l_ref, acc_ref):
    s = _dot_nt(qc, kt)
    if mask is not None:
        s = jnp.where(mask, s, -jnp.inf)
    m_prev = m_ref[c]
    m_new = jnp.maximum(m_prev, jnp.max(s, axis=1, keepdims=True))
    alpha = jnp.exp(m_prev - m_new)
    p = jnp.exp(s - m_new[:, :1])
    l_ref[c] = alpha * l_ref[c] + jnp.sum(p, axis=1, keepdims=True)
    acc_ref[c] = alpha * acc_ref[c] + _dot(p.astype(BF16), vt)
    m_ref[c] = m_new


def _diff_finish(o0, o1, lam, g, out_scale):
    o = o0 - lam * o1
    ms = jnp.mean(o * o, axis=-1, keepdims=True)
    return o * lax.rsqrt(ms + EPS) * g * out_scale


def _split_components(q):
    lane = lax.broadcasted_iota(jnp.int32, q.shape, 1)
    zero = jnp.zeros_like(q)
    return jnp.where(lane < HEAD_DIM, q, zero), jnp.where(lane >= HEAD_DIM, q, zero)


def _attn_prompt_kernel(lam_ref, g_ref, q_ref, k_ref, v_ref, o_ref, m_ref, l_ref, acc_ref,
                        *, tq, nq_main, seq, out_scale):
    i = pl.program_id(1)
    q0, q1 = _split_components(q_ref[...])
    m_ref[...] = jnp.full(m_ref.shape, -jnp.inf, F32)
    l_ref[...] = jnp.zeros(l_ref.shape, F32)
    acc_ref[...] = jnp.zeros(acc_ref.shape, F32)

    def update(kt, vt, mask):
        _softmax_step(0, q0, kt, vt, mask, m_ref, l_ref, acc_ref)
        _softmax_step(1, q1, kt, vt, mask, m_ref, l_ref, acc_ref)

    col = lax.broadcasted_iota(jnp.int32, (tq, LANES), 1)
    update(k_ref[pl.ds(seq, LANES), :], v_ref[pl.ds(seq, LANES), :], col < N_META)

    is_main = i < nq_main

    def body(j, carry):
        start = pl.multiple_of(j * tq, tq)
        update(k_ref[pl.ds(start, tq), :], v_ref[pl.ds(start, tq), :], None)
        return carry

    lax.fori_loop(0, jnp.where(is_main, i, 0), body, 0)

    @pl.when(is_main)
    def _():
        start = pl.multiple_of(i * tq, tq)
        r = lax.broadcasted_iota(jnp.int32, (tq, tq), 0) // CHUNK
        c = lax.broadcasted_iota(jnp.int32, (tq, tq), 1) // CHUNK
        update(k_ref[pl.ds(start, tq), :], v_ref[pl.ds(start, tq), :], c <= r)

    o0 = acc_ref[0] / l_ref[0]
    o1 = acc_ref[1] / l_ref[1]
    o_ref[...] = _diff_finish(o0, o1, lam_ref[...], g_ref[...], out_scale).astype(o_ref.dtype)


def _attn_prompt(qb, kb, vb, lam_row, subln_g, seq, out_scale, tq):
    tp = qb.shape[0]
    kern = functools.partial(_attn_prompt_kernel, tq=tq, nq_main=seq // tq, seq=seq,
                             out_scale=out_scale)
    const = pl.BlockSpec((1, LANES), lambda h, i: (0, 0))
    return pl.pallas_call(
        kern,
        grid=(N_HEADS, tp // tq),
        in_specs=[const, const,
                  pl.BlockSpec((tq, LANES), lambda h, i: (i, h)),
                  pl.BlockSpec((tp, LANES), lambda h, i: (0, h)),
                  pl.BlockSpec((tp, LANES), lambda h, i: (0, h))],
        out_specs=pl.BlockSpec((tq, LANES), lambda h, i: (i, h)),
        out_shape=jax.ShapeDtypeStruct((tp, D_ATT), BF16),
        scratch_shapes=[pltpu.VMEM((2, tq, LANES), F32),
                        pltpu.VMEM((2, tq, LANES), F32),
                        pltpu.VMEM((2, tq, LANES), F32)],
        compiler_params=_cparams(("parallel", "parallel")),
        name="attn_prompt",
    )(lam_row, subln_g, qb, kb, vb)


LOGIT_BOUND_MAX = 40.0
ATTN_WIDE = 4


def _with_ones(vt):
    return jnp.concatenate([vt, jnp.ones(vt.shape, vt.dtype)], axis=1)


def _cache_v_head(vc_ref, h):
    tk = vc_ref.shape[0] // N_HEADS
    return vc_ref[pl.ds(h, tk, stride=N_HEADS), :].astype(BF16)


def _attn_prompt_fast_kernel(lam_ref, g_ref, q_ref, k_ref, v_ref, o_ref, acc_ref,
                             *, tq, nq_main, seq, out_scale):
    i = pl.program_id(1)
    q0, q1 = _split_components(q_ref[...])
    qq = jnp.concatenate([q0, q1], axis=0)

    def scores(start, rows):
        return _dot_nt(qq, k_ref[pl.ds(start, rows), :])

    def weighted(s, start, rows, mask):
        p = jnp.exp(s)
        if mask is not None:
            p = jnp.where(mask, p, 0.0)
        return _dot(p.astype(BF16), _with_ones(v_ref[pl.ds(start, rows), :]))

    def tile_pv(start, rows, mask):
        return weighted(scores(start, rows), start, rows, mask)

    col = lax.broadcasted_iota(jnp.int32, (2 * tq, LANES), 1)
    acc_ref[...] = tile_pv(seq, LANES, col < N_META)

    is_main = i < nq_main
    n_full = jnp.where(is_main, i, 0)

    wide = ATTN_WIDE

    def body(j, carry):
        start = pl.multiple_of(wide * j * tq, tq)
        acc_ref[...] += tile_pv(start, wide * tq, None)
        return carry

    n_wide = n_full // wide
    lax.fori_loop(0, n_wide, body, 0)
    rem = n_full - wide * n_wide
    base = wide * n_wide

    @pl.when(rem >= 2)
    def _():
        acc_ref[...] += tile_pv(pl.multiple_of(base * tq, tq), 2 * tq, None)

    @pl.when(rem % 2 == 1)
    def _():
        acc_ref[...] += tile_pv(pl.multiple_of((n_full - 1) * tq, tq), tq, None)

    @pl.when(is_main)
    def _():
        r = (lax.broadcasted_iota(jnp.int32, (2 * tq, tq), 0) % tq) // CHUNK
        c = lax.broadcasted_iota(jnp.int32, (2 * tq, tq), 1) // CHUNK
        acc_ref[...] += tile_pv(pl.multiple_of(i * tq, tq), tq, c <= r)

    acc = acc_ref[...]
    o0 = acc[:tq, :LANES] / acc[:tq, LANES:]
    o1 = acc[tq:, :LANES] / acc[tq:, LANES:]
    o_ref[...] = _diff_finish(o0, o1, lam_ref[...], g_ref[...], out_scale).astype(o_ref.dtype)


def _attn_prompt_fast(qb, kb, vb, lam_row, subln_g, seq, out_scale, tq):
    tp = qb.shape[0]
    kern = functools.partial(_attn_prompt_fast_kernel, tq=tq, nq_main=seq // tq, seq=seq,
                             out_scale=out_scale)
    const = pl.BlockSpec((1, LANES), lambda h, i: (0, 0))
    return pl.pallas_call(
        kern,
        grid=(N_HEADS, tp // tq),
        in_specs=[const, const,
                  pl.BlockSpec((tq, LANES), lambda h, i: (i, h)),
                  pl.BlockSpec((tp, LANES), lambda h, i: (0, h)),
                  pl.BlockSpec((tp, LANES), lambda h, i: (0, h))],
        out_specs=pl.BlockSpec((tq, LANES), lambda h, i: (i, h)),
        out_shape=jax.ShapeDtypeStruct((tp, D_ATT), BF16),
        scratch_shapes=[pltpu.VMEM((2 * tq, 2 * LANES), F32)],
        compiler_params=_cparams(("parallel", "parallel")),
        name="attn_prompt_fast",
    )(lam_row, subln_g, qb, kb, vb)


def _attn_sample_fast_kernel(lam_ref, g_ref, q_ref, kc_ref, vc_ref, kn_ref, vn_ref, o_in_ref, o_ref,
                             acc_ref, *, nq, out_scale):
    del o_in_ref
    j = pl.program_id(1)
    rows = 2 * nq

    def head_update(h, kt, vt, mask):
        p = jnp.exp(_dot_nt(q_ref[h], kt))
        if mask is not None:
            p = jnp.where(mask, p, 0.0)
        acc_ref[h] += _dot(p.astype(BF16), _with_ones(vt))

    @pl.when(j == 0)
    def _():
        acc_ref[...] = jnp.zeros(acc_ref.shape, F32)
        col = lax.broadcasted_iota(jnp.int32, (rows, LANES), 1)
        for h in range(N_HEADS):
            sl = slice(h * LANES, (h + 1) * LANES)
            head_update(h, kn_ref[:, sl], vn_ref[:, sl], col < nq)

    for h in range(N_HEADS):
        sl = slice(h * LANES, (h + 1) * LANES)
        head_update(h, kc_ref[:, sl].astype(BF16), _cache_v_head(vc_ref, h), None)

    @pl.when(j == pl.num_programs(1) - 1)
    def _():
        for h in range(N_HEADS):
            acc = acc_ref[h]
            o = acc[:, :LANES] / acc[:, LANES:]
            res = _diff_finish(o[:nq], o[nq:], lam_ref[...], g_ref[...], out_scale)
            o_ref[:, h * LANES:(h + 1) * LANES] = res.astype(o_ref.dtype)


def _attn_sample_kernel(lam_ref, g_ref, q_ref, kc_ref, vc_ref, kn_ref, vn_ref, o_in_ref, o_ref,
                        m_ref, l_ref, acc_ref, *, nq, out_scale):
    del o_in_ref
    j = pl.program_id(1)
    rows = 2 * nq

    @pl.when(j == 0)
    def _():
        m_ref[...] = jnp.full(m_ref.shape, -jnp.inf, F32)
        l_ref[...] = jnp.zeros(l_ref.shape, F32)
        acc_ref[...] = jnp.zeros(acc_ref.shape, F32)
        col = lax.broadcasted_iota(jnp.int32, (rows, LANES), 1)
        for h in range(N_HEADS):
            sl = slice(h * LANES, (h + 1) * LANES)
            _softmax_step(h, q_ref[h], kn_ref[:, sl], vn_ref[:, sl], col < nq,
                          m_ref, l_ref, acc_ref)

    for h in range(N_HEADS):
        sl = slice(h * LANES, (h + 1) * LANES)
        _softmax_step(h, q_ref[h], kc_ref[:, sl].astype(BF16), _cache_v_head(vc_ref, h), None,
                      m_ref, l_ref, acc_ref)

    @pl.when(j == pl.num_programs(1) - 1)
    def _():
        for h in range(N_HEADS):
            o = acc_ref[h] / l_ref[h]
            res = _diff_finish(o[:nq], o[nq:], lam_ref[...], g_ref[...], out_scale)
            o_ref[:, h * LANES:(h + 1) * LANES] = res.astype(o_ref.dtype)


def _attn_sample(qz, cache_k, cache_v, k_new, v_new, o_buf, lam_row, subln_g, row_off, out_scale, tk, fast):
    nb, past = cache_k.shape[0], cache_k.shape[1]
    nq = qz.shape[2] // 2
    if fast:
        kern = functools.partial(_attn_sample_fast_kernel, nq=nq, out_scale=out_scale)
        scratch = [pltpu.VMEM((N_HEADS, 2 * nq, 2 * LANES), F32)]
    else:
        kern = functools.partial(_attn_sample_kernel, nq=nq, out_scale=out_scale)
        scratch = [pltpu.VMEM((N_HEADS, 2 * nq, LANES), F32)] * 3
    const = pl.BlockSpec((1, LANES), lambda b, j: (0, 0))
    blk_off = row_off // nq
    return pl.pallas_call(
        kern,
        grid=(nb, past // tk),
        in_specs=[const, const,
                  pl.BlockSpec((None, N_HEADS, 2 * nq, LANES), lambda b, j: (b, 0, 0, 0)),
                  pl.BlockSpec((None, tk, D_ATT), lambda b, j: (b, j, 0)),
                  pl.BlockSpec((None, tk * N_HEADS, V_DIM), lambda b, j: (b, j, 0)),
                  pl.BlockSpec((None, LANES, D_ATT), lambda b, j: (b, 0, 0)),
                  pl.BlockSpec((None, LANES, D_ATT), lambda b, j: (b, 0, 0)),
                  pl.BlockSpec(memory_space=pl.ANY)],
        out_specs=pl.BlockSpec((nq, D_ATT), lambda b, j: (blk_off + b, 0)),
        out_shape=jax.ShapeDtypeStruct(o_buf.shape, o_buf.dtype),
        scratch_shapes=scratch,
        input_output_aliases={7: 0},
        compiler_params=_cparams(("parallel", "arbitrary")),
        name="attn_sample_fast" if fast else "attn_sample",
    )(lam_row, subln_g, qz, cache_k, cache_v, k_new, v_new, o_buf)


def _s5_weights(a_re, a_im, log_dt, b_re, b_im, c_re, c_im):
    hp = lax.Precision.HIGHEST
    n_t, gl, tc = S5_TILES, S5_LANE_GROUPS, S5_CHUNK
    dt = jnp.exp(log_dt)[:, None]
    mag = jnp.exp(a_re * dt)
    abar_re = mag * jnp.cos(a_im * dt)
    abar_im = mag * jnp.sin(a_im * dt)
    nr, ni = abar_re - 1.0, abar_im
    den = a_re * a_re + a_im * a_im
    coef_re = (nr * a_re + ni * a_im) / den
    coef_im = (ni * a_re - nr * a_im) / den
    bbar_re = coef_re[..., None] * b_re - coef_im[..., None] * b_im
    bbar_im = coef_re[..., None] * b_im + coef_im[..., None] * b_re
    n = jnp.arange(tc + 1, dtype=F32)[:, None, None]
    pw_mag = jnp.exp(n * (a_re * dt))
    pw_re = pw_mag * jnp.cos(n * (a_im * dt))
    pw_im = pw_mag * jnp.sin(n * (a_im * dt))
    e_re = pw_re[:tc, :, :, None] * bbar_re - pw_im[:tc, :, :, None] * bbar_im
    e_im = pw_re[:tc, :, :, None] * bbar_im + pw_im[:tc, :, :, None] * bbar_re
    kern = (jnp.einsum('gcp,lgpd->glcd', c_re, e_re, precision=hp)
            - jnp.einsum('gcp,lgpd->glcd', c_im, e_im, precision=hp))
    eye = jnp.eye(gl, dtype=F32)
    w_intra = jnp.einsum('jglcd,gh->jlgdhc', kern.reshape(n_t, gl, tc, SSM_GROUP, SSM_GROUP), eye)
    w_intra = w_intra.reshape(n_t, tc, LANES, LANES)
    eb = jnp.stack([e_re[::-1], e_im[::-1]], 0)
    eb = eb.reshape(2, tc, n_t, gl, SSM_STATE, SSM_GROUP)
    w_state = eb.transpose(2, 1, 3, 5, 0, 4).reshape(n_t, tc, LANES, 2 * SSM_STATE)
    cp_re = c_re[None] * pw_re[1:, :, None, :] - c_im[None] * pw_im[1:, :, None, :]
    cp_im = c_re[None] * pw_im[1:, :, None, :] + c_im[None] * pw_re[1:, :, None, :]
    cp = jnp.stack([cp_re, -cp_im], 0).reshape(2, tc, n_t, gl, SSM_GROUP, SSM_STATE)
    w_read = cp.transpose(2, 1, 0, 5, 3, 4).reshape(n_t, tc, 2 * SSM_STATE, LANES)
    half = gl * SSM_STATE
    a_pow = jnp.concatenate([pw_re[tc].reshape(n_t, 1, half), pw_im[tc].reshape(n_t, 1, half)], -1)
    rp = jnp.arange(2 * SSM_STATE)
    col = jnp.arange(2 * half)
    spread = ((rp[:, None] // SSM_STATE == col[None, :] // half)
              & (rp[:, None] % SSM_STATE == col[None, :] % SSM_STATE)).astype(BF16)
    return (w_intra.astype(BF16), w_state.astype(BF16), w_read.astype(BF16), a_pow, spread, spread.T)


def _s5_kernel(u_ref, wi_ref, wsc_ref, wrc_ref, ap_ref, sp_ref, spt_ref, d_ref, h0r_ref, h0i_ref,
               y_ref, hpr_ref, hpi_ref, hsr_ref, hsi_ref,
               y_acc, v_ref, hs_ref, wt_ref, ws_ref, wr_ref,
               *, nc, n_main, n_meta_chunks, n_seq, seq_chunks):
    tc = S5_CHUNK
    half = hs_ref.shape[1] // 2
    grp_r = lax.broadcasted_iota(jnp.int32, (LANES, 2 * half), 0) // SSM_GROUP
    grp_c = (lax.broadcasted_iota(jnp.int32, (LANES, 2 * half), 1) % half) // SSM_STATE
    for s in range(tc):
        full = _dot(wsc_ref[s], sp_ref[...])
        ws_ref[s * LANES:(s + 1) * LANES, :] = jnp.where(grp_r == grp_c, full, 0.0).astype(BF16)
    grp_r = (lax.broadcasted_iota(jnp.int32, (2 * half, LANES), 0) % half) // SSM_STATE
    grp_c = lax.broadcasted_iota(jnp.int32, (2 * half, LANES), 1) // SSM_GROUP
    for t in range(tc):
        full = _dot(spt_ref[...], wrc_ref[t])
        wr_ref[:, t * LANES:(t + 1) * LANES] = jnp.where(grp_r == grp_c, full, 0.0).astype(BF16)
    for s in range(tc):
        for t in range(tc):
            blk = wi_ref[t - s] if t >= s else jnp.zeros((LANES, LANES), BF16)
            wt_ref[s * LANES:(s + 1) * LANES, t * LANES:(t + 1) * LANES] = blk
    lhs = jnp.concatenate(
        [u_ref[pl.ds(s, nc, stride=tc), :].astype(BF16) for s in range(tc)], axis=1)
    y_acc[...] = _dot(lhs, wt_ref[...])
    v_ref[...] = _dot(lhs, ws_ref[...])
    a_re = ap_ref[:, :half]
    a_im = ap_ref[:, half:]

    def advance(h_re, h_im, v):
        return (a_re * h_re - a_im * h_im + v[:, :half],
                a_re * h_im + a_im * h_re + v[:, half:])

    hs_ref[...] = jnp.zeros(hs_ref.shape, F32)

    h_re = jnp.zeros((1, half), F32)
    h_im = jnp.zeros((1, half), F32)
    for c in range(n_main, n_main + n_meta_chunks):
        hs_ref[pl.ds(c, 1), :] = jnp.concatenate([h_re, h_im], axis=1)
        h_re, h_im = advance(h_re, h_im, v_ref[pl.ds(c, 1), :])

    def body(c, carry):
        h_re, h_im = carry
        hs_ref[pl.ds(c, 1), :] = jnp.concatenate([h_re, h_im], axis=1)
        return advance(h_re, h_im, v_ref[pl.ds(c, 1), :])

    h_re, h_im = lax.fori_loop(0, n_main, body, (h_re, h_im))
    hpr_ref[...] = h_re
    hpi_ref[...] = h_im

    base = n_main + n_meta_chunks
    for b in range(n_seq):
        s_re = h0r_ref[pl.ds(b, 1), :]
        s_im = h0i_ref[pl.ds(b, 1), :]
        for c in range(base + b * seq_chunks, base + (b + 1) * seq_chunks):
            hs_ref[pl.ds(c, 1), :] = jnp.concatenate([s_re, s_im], axis=1)
            s_re, s_im = advance(s_re, s_im, v_ref[pl.ds(c, 1), :])
        hsr_ref[pl.ds(b, 1), :] = s_re
        hsi_ref[pl.ds(b, 1), :] = s_im

    y_acc[...] += _dot(hs_ref[...].astype(BF16), wr_ref[...])
    d = d_ref[...]
    for t in range(tc):
        rows = pl.ds(t, nc, stride=tc)
        y = y_acc[:, t * LANES:(t + 1) * LANES] + d * u_ref[rows, :]
        y_ref[rows, :] = jax.nn.gelu(y)


def _s5(z, weights, d_skip, h0_re, h0_im, seq, n_seq, seq_len):
    tp = z.shape[0]
    tc = S5_CHUNK
    nc = tp // tc
    w_intra, w_state, w_read, a_pow, spread, spread_t = weights
    half = S5_LANE_GROUPS * SSM_STATE
    whole = lambda a: pl.BlockSpec(a.shape, lambda j: (0,) * a.ndim)
    kern = functools.partial(_s5_kernel, nc=nc, n_main=seq // tc, n_meta_chunks=N_META // tc,
                             n_seq=n_seq, seq_chunks=seq_len // tc)
    wspec = lambda a: pl.BlockSpec((None,) + a.shape[1:], lambda j: (j,) + (0,) * (a.ndim - 1))
    col = pl.BlockSpec((tp, LANES), lambda j: (0, j))
    st = lambda r: pl.BlockSpec((r, half), lambda j: (0, j))
    gp = N_SSM_GROUPS * SSM_STATE
    return pl.pallas_call(
        kern,
        grid=(S5_TILES,),
        in_specs=[col, wspec(w_intra), wspec(w_state), wspec(w_read), wspec(a_pow),
                  whole(spread), whole(spread_t),
                  pl.BlockSpec((1, LANES), lambda j: (0, j)), st(n_seq), st(n_seq)],
        out_specs=[col, st(1), st(1), st(n_seq), st(n_seq)],
        out_shape=[jax.ShapeDtypeStruct((tp, D_SSM), F32),
                   jax.ShapeDtypeStruct((1, gp), F32),
                   jax.ShapeDtypeStruct((1, gp), F32),
                   jax.ShapeDtypeStruct((n_seq, gp), F32),
                   jax.ShapeDtypeStruct((n_seq, gp), F32)],
        scratch_shapes=[pltpu.VMEM((nc, tc * LANES), F32),
                        pltpu.VMEM((nc, 2 * half), F32),
                        pltpu.VMEM((nc, 2 * half), F32),
                        pltpu.VMEM((tc * LANES, tc * LANES), BF16),
                        pltpu.VMEM((tc * LANES, 2 * half), BF16),
                        pltpu.VMEM((2 * half, tc * LANES), BF16)],
        compiler_params=_cparams(("parallel",)),
        name="s5_scan",
    )(z, w_intra, w_state, w_read, a_pow, spread, spread_t, d_skip.reshape(1, D_SSM), h0_re, h0_im)


def _router_kernel(x_ref, g_ref, w_ref, b_ref, h_ref, e_ref, gate_ref):
    x = x_ref[...]
    ms = jnp.mean(x * x, axis=-1, keepdims=True)
    h = x * lax.rsqrt(ms + EPS) * g_ref[...]
    h_ref[...] = h.astype(h_ref.dtype)
    logits = _dot(h.astype(BF16), w_ref[...]) + b_ref[...]
    lane = lax.broadcasted_iota(jnp.int32, logits.shape, 1)
    neg = -jnp.inf
    big = jnp.int32(LANES)

    def first_argmax(vals, vmax):
        return jnp.min(jnp.where(vals == vmax, lane, big), axis=1, keepdims=True)

    lg = jnp.where(lane < N_EGROUPS, logits, neg)
    mg = jnp.max(lg, axis=1, keepdims=True)
    sg = jnp.sum(jnp.exp(lg - mg), axis=1, keepdims=True)
    g_w = 1.0 / sg
    g_idx = first_argmax(lg, mg)
    lo = N_EGROUPS + EXPERTS_PER_GROUP * g_idx
    le = jnp.where((lane >= lo) & (lane < lo + EXPERTS_PER_GROUP), logits, neg)
    m1 = jnp.max(le, axis=1, keepdims=True)
    se = jnp.sum(jnp.exp(le - m1), axis=1, keepdims=True)
    i1 = first_argmax(le, m1)
    le2 = jnp.where(lane == i1, neg, le)
    m2 = jnp.max(le2, axis=1, keepdims=True)
    i2 = first_argmax(le2, m2)
    p1 = 1.0 / se
    p2 = jnp.exp(m2 - m1) / se
    tot = p1 + p2
    w1 = g_w * (p1 / tot)
    w2 = g_w * (p2 / tot)
    e_ref[...] = jnp.where(lane == 0, i1 - N_EGROUPS, jnp.where(lane == 1, i2 - N_EGROUPS, 0))
    gate_ref[...] = jnp.where(lane == 0, w1, jnp.where(lane == 1, w2, 0.0))


def _router(x2, g, w_r, b_r, tr):
    tp, d = x2.shape
    return pl.pallas_call(
        _router_kernel,
        grid=(tp // tr,),
        in_specs=[pl.BlockSpec((tr, d), lambda i: (i, 0)),
                  pl.BlockSpec((1, d), lambda i: (0, 0)),
                  pl.BlockSpec((d, LANES), lambda i: (0, 0)),
                  pl.BlockSpec((1, LANES), lambda i: (0, 0))],
        out_specs=[pl.BlockSpec((tr, d), lambda i: (i, 0)),
                   pl.BlockSpec((tr, LANES), lambda i: (i, 0)),
                   pl.BlockSpec((tr, LANES), lambda i: (i, 0))],
        out_shape=[jax.ShapeDtypeStruct((tp, d), F32),
                   jax.ShapeDtypeStruct((tp, LANES), jnp.int32),
                   jax.ShapeDtypeStruct((tp, LANES), F32)],
        compiler_params=_cparams(("parallel",)),
        name="norm2_router",
    )(x2, g.reshape(1, d), w_r, b_r)


def _expert_kernel(be_ref, nu_ref, cnt_ref, tok_ref, tokn_ref, dst_ref, h_hbm, g_ref, w1_ref, w3_ref,
                   w2_ref, y_hbm, xbuf, ybuf, xb16, gsem, ssem, *, nb):
    del be_ref
    b = pl.program_id(0)
    hh = pl.program_id(1)
    last = pl.num_programs(1) - 1
    n_used = nu_ref[0]
    active = b < n_used
    slot = b % 2
    other = 1 - slot

    def gather(idx_ref, r, s):
        return pltpu.make_async_copy(h_hbm.at[pl.ds(idx_ref[0, r], 1), :],
                                     xbuf.at[s, pl.ds(r, 1), :], gsem.at[s])

    def scatter(r, d, s):
        return pltpu.make_async_copy(ybuf.at[s, pl.ds(r, 1), :], y_hbm.at[pl.ds(d, 1), :], ssem.at[s])

    def start_gather(idx_ref, n, s):
        def body(r, c):
            gather(idx_ref, r, s).start()
            return c
        lax.fori_loop(0, n, body, 0)

    def wait_gather(n, s):
        def body(r, c):
            gather(tok_ref, 0, s).wait()
            return c
        lax.fori_loop(0, n, body, 0)

    def wait_scatter(n, s):
        def body(r, c):
            scatter(0, 0, s).wait()
            return c
        lax.fori_loop(0, n, body, 0)

    @pl.when((b == 0) & (hh == 0))
    def _():
        xbuf[...] = jnp.zeros(xbuf.shape, F32)

        @pl.when(active)
        def _():
            start_gather(tok_ref, cnt_ref[0], 0)

    @pl.when(active & (hh == 0))
    def _():
        wait_gather(cnt_ref[b], slot)
        xb16[...] = xbuf[slot].astype(BF16)

    @pl.when(active & (hh == 1) & (b + 1 < n_used))
    def _():
        start_gather(tokn_ref, cnt_ref[jnp.minimum(b + 1, nb - 1)], other)

    @pl.when(active)
    def _():
        x = xb16[...]
        dh = w1_ref.shape[1]
        w13 = jnp.concatenate([w1_ref[...].astype(BF16), w3_ref[...].astype(BF16)], axis=1)
        ac = _dot(x, w13)
        hid = (jax.nn.silu(ac[:, :dh]) * ac[:, dh:]).astype(BF16)
        y = _dot(hid, w2_ref[...].astype(BF16)) * g_ref[...]

        @pl.when(hh == 0)
        def _():
            ybuf[slot] = y

        @pl.when(hh > 0)
        def _():
            ybuf[slot] += y

    @pl.when(active & (hh == last))
    def _():
        def body(r, c):
            scatter(r, dst_ref[0, r], slot).start()
            return c
        lax.fori_loop(0, cnt_ref[b], body, 0)

        @pl.when(b > 0)
        def _():
            wait_scatter(cnt_ref[jnp.maximum(b - 1, 0)], other)

        @pl.when(b + 1 >= n_used)
        def _():
            wait_scatter(cnt_ref[b], slot)


def _experts(h2, plan, t_real, w1, w3, w2):
    slot_tok, slot_dst, slot_gate, block_expert, n_used, cnt = plan
    d = h2.shape[1]
    nb = block_expert.shape[0]
    dh = D_EXPERT // EXPERT_SLICES

    def half(b, hh):
        return jnp.where(b % 2 == 0, hh, EXPERT_SLICES - 1 - hh)

    def eidx(b, be, nu):
        return be[jnp.minimum(b, nu[0] - 1)]

    smem_row = pl.BlockSpec((None, 1, MOE_BLOCK), lambda b, hh, be, nu, cnt: (b, 0, 0),
                            memory_space=pltpu.SMEM)
    smem_next = pl.BlockSpec((None, 1, MOE_BLOCK),
                             lambda b, hh, be, nu, cnt: (jnp.minimum(b + 1, nb - 1), 0, 0),
                             memory_space=pltpu.SMEM)
    tok3 = slot_tok.reshape(nb, 1, MOE_BLOCK)
    grid_spec = pltpu.PrefetchScalarGridSpec(
        num_scalar_prefetch=3,
        grid=(nb, EXPERT_SLICES),
        in_specs=[smem_row, smem_next, smem_row,
                  pl.BlockSpec(memory_space=pl.ANY),
                  pl.BlockSpec((MOE_BLOCK, 1), lambda b, hh, be, nu, cnt: (b, 0)),
                  pl.BlockSpec((None, d, dh), lambda b, hh, be, nu, cnt: (eidx(b, be, nu), 0, half(b, hh))),
                  pl.BlockSpec((None, d, dh), lambda b, hh, be, nu, cnt: (eidx(b, be, nu), 0, half(b, hh))),
                  pl.BlockSpec((None, dh, d), lambda b, hh, be, nu, cnt: (eidx(b, be, nu), half(b, hh), 0))],
        out_specs=pl.BlockSpec(memory_space=pl.ANY),
        scratch_shapes=[pltpu.VMEM((2, MOE_BLOCK, d), F32),
                        pltpu.VMEM((2, MOE_BLOCK, d), F32),
                        pltpu.VMEM((MOE_BLOCK, d), BF16),
                        pltpu.SemaphoreType.DMA((2,)),
                        pltpu.SemaphoreType.DMA((2,))],
    )
    return pl.pallas_call(
        functools.partial(_expert_kernel, nb=nb),
        grid_spec=grid_spec,
        out_shape=jax.ShapeDtypeStruct((TOP_K * t_real, d), F32),
        compiler_params=_cparams(("arbitrary", "arbitrary")),
        name="expert_mlp",
    )(block_expert, n_used, cnt, tok3, tok3, slot_dst.reshape(nb, 1, MOE_BLOCK),
      h2, slot_gate.reshape(-1, 1), w1, w3, w2)


def _route_plan(expert, gate, t_real):
    s = t_real * TOP_K
    n_blocks = -(-(s + N_EXPERTS * (MOE_BLOCK - 1)) // MOE_BLOCK)
    p = n_blocks * MOE_BLOCK
    flat_e = expert.reshape(-1).astype(jnp.int32)
    flat_g = gate.reshape(-1).astype(F32)
    order = jnp.argsort(flat_e).astype(jnp.int32)
    se = flat_e[order]
    bounds = jnp.searchsorted(se, jnp.arange(N_EXPERTS + 1, dtype=jnp.int32)).astype(jnp.int32)
    start = bounds[:-1]
    counts = bounds[1:] - start
    padded = (counts + MOE_BLOCK - 1) // MOE_BLOCK * MOE_BLOCK
    pad_end = jnp.cumsum(padded)
    pad_start = pad_end - padded
    block_start = jnp.arange(n_blocks, dtype=jnp.int32) * MOE_BLOCK
    block_expert = jnp.minimum(jnp.searchsorted(pad_end, block_start, side='right'),
                               N_EXPERTS - 1).astype(jnp.int32)
    n_used = (pad_end[-1] // MOE_BLOCK).astype(jnp.int32).reshape(1)
    cnt = jnp.clip(counts[block_expert] - (block_start - pad_start[block_expert]), 0, MOE_BLOCK)
    off = ((block_start - pad_start[block_expert])[:, None]
           + jnp.arange(MOE_BLOCK, dtype=jnp.int32)[None, :])
    valid = (off < counts[block_expert][:, None]).reshape(p)
    flat = order[jnp.clip(start[block_expert][:, None] + off, 0, s - 1).reshape(p)]
    tok = flat // TOP_K
    slot_tok = jnp.where(valid, tok, 0)
    slot_dst = jnp.where(valid, (flat % TOP_K) * t_real + tok, 0)
    slot_gate = jnp.where(valid, flat_g[flat], 0.0)
    return slot_tok, slot_dst, slot_gate, block_expert, n_used, cnt.astype(jnp.int32)


def _combine_kernel(x_ref, y0_ref, y1_ref, o_ref):
    o_ref[...] = x_ref[...] + y0_ref[...] + y1_ref[...]


def _combine(x2, y2, row_off, n_rows, tile):
    d = x2.shape[1]
    off = row_off // tile
    y3 = y2.reshape(TOP_K, y2.shape[0] // TOP_K, d)
    return pl.pallas_call(
        _combine_kernel,
        grid=(n_rows // tile,),
        in_specs=[pl.BlockSpec((tile, d), lambda i: (off + i, 0)),
                  pl.BlockSpec((None, tile, d), lambda i: (0, off + i, 0)),
                  pl.BlockSpec((None, tile, d), lambda i: (1, off + i, 0))],
        out_specs=pl.BlockSpec((tile, d), lambda i: (i, 0)),
        out_shape=jax.ShapeDtypeStruct((n_rows, d), F32),
        compiler_params=_cparams(("parallel",)),
        name="moe_combine",
    )(x2, y3, y3)


def kernel(x_prompt, x_sample, cache_k, cache_v, state_ssm_re, state_ssm_im, meta_tokens, norm1_g, w_in, b_in, ssm_a_re, ssm_a_im, ssm_log_dt, ssm_b_re, ssm_b_im, ssm_c_re, ssm_c_im, ssm_d, w_glu, b_glu, w_ssm_proj, q_norm_g, k_norm_g, lam_q1, lam_k1, lam_q2, lam_k2, subln_g, w_att_proj, w_o, norm2_g, w_router_group, b_router_group, w_router_expert, b_router_expert, w1_e, w3_e, w2_e):
    assert x_prompt.shape[0] == 1 and w_in.shape[0] == 1
    seq = x_prompt.shape[1]
    nb, nq = x_sample.shape[0], x_sample.shape[1]
    past = cache_k.shape[2]
    n_s = nb * nq
    t_real = seq + N_META + n_s
    tp = -(-t_real // ROW_ALIGN) * ROW_ALIGN
    off_meta, off_s = seq, seq + N_META
    tq = 256
    assert seq % tq == 0 and nq == N_META and past % 512 == 0
    lam_init = 0.8 - 0.6 * math.exp(-0.3 * 0)
    out_scale = 1.0 - lam_init

    x_cat = jnp.concatenate([x_prompt[0], meta_tokens.astype(F32), x_sample.reshape(n_s, D_MODEL),
                             jnp.zeros((tp - t_real, D_MODEL), F32)], axis=0)

    tm = _row_tile(tp, 1088)
    tr = _row_tile(tp, 256)

    h1 = _rmsnorm(x_cat, norm1_g[0], tr)
    z = _inproj(h1, w_in[0], b_in[0], tm, 512)

    pos = jnp.concatenate([N_META + jnp.arange(seq), jnp.arange(N_META),
                           jnp.tile(past + jnp.arange(nq), nb),
                           jnp.zeros((tp - t_real,), jnp.int32)]).astype(F32)
    half = HEAD_DIM // 2
    inv = ROPE_THETA ** (-jnp.arange(half, dtype=F32) / half)
    ang = pos[:, None] * inv[None, :]
    cos_t = jnp.tile(jnp.cos(ang), (1, LANES // half))
    sin_h = jnp.sin(ang)
    sin_t = jnp.tile(jnp.concatenate([-sin_h, sin_h], axis=1), (1, LANES // HEAD_DIM))
    gq = jnp.tile(q_norm_g[0], LANES // HEAD_DIM).reshape(1, LANES)
    gk = jnp.tile(k_norm_g[0], LANES // HEAD_DIM).reshape(1, LANES)
    qb, kf, kb, vf, vb = _qk_rope(z, cos_t, sin_t, gq, gk, tr)

    lam = (jnp.exp(jnp.sum(lam_q1[0] * lam_k1[0])) - jnp.exp(jnp.sum(lam_q2[0] * lam_k2[0])) + lam_init)
    lam_row = jnp.full((1, LANES), lam, F32)
    sg = subln_g[0].reshape(1, LANES)

    q_s = qb[off_s:off_s + n_s].reshape(nb, nq, N_HEADS, LANES).transpose(0, 2, 1, 3)
    lane = jnp.arange(LANES)
    qz = jnp.concatenate([jnp.where(lane < HEAD_DIM, q_s, 0), jnp.where(lane >= HEAD_DIM, q_s, 0)], axis=2)
    pad_new = lambda a: jnp.pad(a[off_s:off_s + n_s].reshape(nb, nq, D_ATT), ((0, 0), (0, LANES - nq), (0, 0)))
    ck = cache_k[0].reshape(nb, past, D_ATT)
    cv = cache_v[0].reshape(nb, past * N_HEADS, V_DIM)

    logit_bound = 8.1 * jnp.max(jnp.abs(q_norm_g[0])) * jnp.max(jnp.abs(k_norm_g[0]))

    def attention(fast):
        def run():
            if fast:
                o = _attn_prompt_fast(qb, kb, vb, lam_row, sg, seq, out_scale, 512)
            else:
                o = _attn_prompt(qb, kb, vb, lam_row, sg, seq, out_scale, tq)
            return _attn_sample(qz, ck, cv, pad_new(kb), pad_new(vb), o, lam_row, sg, off_s, out_scale,
                                512, fast)
        return run

    o_att = lax.cond(logit_bound <= LOGIT_BOUND_MAX, attention(True), attention(False))

    gp = N_SSM_GROUPS * SSM_STATE
    s5w = _s5_weights(ssm_a_re[0], ssm_a_im[0], ssm_log_dt[0], ssm_b_re[0], ssm_b_im[0],
                      ssm_c_re[0], ssm_c_im[0])
    ys, hp_re, hp_im, hs_re, hs_im = _s5(z, s5w, ssm_d[0], state_ssm_re[0].reshape(nb, gp),
                                         state_ssm_im[0].reshape(nb, gp), seq, nb, nq)
    ysg = _glu(ys, w_glu[0], b_glu[0], tm, 512)
    m = _merge(ysg, o_att, w_ssm_proj[0], w_att_proj[0], z, tm, 512)
    x2 = _outproj(m, w_o[0], x_cat, tm, 512)

    w_r = jnp.concatenate([w_router_group[0], w_router_expert[0],
                           jnp.zeros((D_MODEL, LANES - N_EGROUPS - N_EXPERTS), F32)], axis=1).astype(BF16)
    b_r = jnp.concatenate([b_router_group[0], b_router_expert[0],
                           jnp.zeros((LANES - N_EGROUPS - N_EXPERTS,), F32)]).reshape(1, LANES)
    h2, e_sel, g_sel = _router(x2, norm2_g[0], w_r, b_r, tr)

    plan = _route_plan(e_sel[:t_real, :TOP_K], g_sel[:t_real, :TOP_K], t_real)
    y2 = _experts(h2, plan, t_real, w1_e[0], w3_e[0], w2_e[0])

    def heads(a, lead):
        return a.reshape(lead + (N_HEADS, 2, HEAD_DIM))

    y_prompt = _combine(x2, y2, 0, seq, tr).reshape(1, seq, D_MODEL)
    y_sample = _combine(x2, y2, off_s, n_s, nq).reshape(nb, nq, D_MODEL)
    k_p = jnp.concatenate([kf[off_meta:off_meta + N_META], kf[:seq]], axis=0)
    v_p = jnp.concatenate([vf[off_meta:off_meta + N_META], vf[:seq]], axis=0)
    k_prompt = heads(k_p, (1, 1, seq + N_META))
    v_prompt = v_p.reshape(1, 1, seq + N_META, N_HEADS, V_DIM)
    k_sample = heads(kf[off_s:off_s + n_s], (1, nb, nq))
    v_sample = vf[off_s:off_s + n_s].reshape(1, nb, nq, N_HEADS, V_DIM)
    st = lambda a, lead: a.reshape(lead + (N_SSM_GROUPS, SSM_STATE))
    return (y_prompt, y_sample, k_prompt, v_prompt, st(hp_re, (1, 1)), st(hp_im, (1, 1)),
            k_sample, v_sample, st(hs_re, (1, nb)), st(hs_im, (1, nb)))
```

```python
import functools
import math

import jax
import jax.numpy as jnp
from jax import lax
from jax.experimental import pallas as pl
from jax.experimental.pallas import tpu as pltpu

F32 = jnp.float32
BF16 = jnp.bfloat16

D_MODEL = 4096
N_META = 16
CHUNK = 64
N_HEADS = 16
HEAD_DIM = 64
V_DIM = 128
D_ATT = N_HEADS * V_DIM
D_SSM = 2048
SSM_GROUP = 16
N_SSM_GROUPS = D_SSM // SSM_GROUP
SSM_STATE = 64
IN_WIDTH = D_SSM + 3 * D_ATT + 2 * D_MODEL
ROPE_THETA = 10000.0
N_EGROUPS = 8
EXPERTS_PER_GROUP = 8
N_EXPERTS = N_EGROUPS * EXPERTS_PER_GROUP
TOP_K = 2
D_EXPERT = 512
MOE_BLOCK = 320
EXPERT_SLICES = 4
EPS = 1e-6

LANES = 128
ROW_ALIGN = 512
S5_CHUNK = 8
S5_LANE_GROUPS = LANES // SSM_GROUP
S5_TILES = D_SSM // LANES
VMEM_LIMIT = 56 * 1024 * 1024


def _cparams(sem, vmem=VMEM_LIMIT):
    return pltpu.CompilerParams(dimension_semantics=sem, vmem_limit_bytes=vmem)


def _row_tile(tp, cap):
    best = 16
    for t in range(16, cap + 1, 16):
        if tp % t == 0:
            best = t
    return best


def _dot(a, b):
    return jnp.dot(a, b, preferred_element_type=F32)


def _dot_nt(a, b):
    return lax.dot_general(a, b, (((1,), (1,)), ((), ())), preferred_element_type=F32)


def _rmsnorm_kernel(x_ref, g_ref, o_ref):
    x = x_ref[...]
    ms = jnp.mean(x * x, axis=-1, keepdims=True)
    o_ref[...] = (x * lax.rsqrt(ms + EPS) * g_ref[...]).astype(o_ref.dtype)


def _rmsnorm(x, g, tr):
    tp, d = x.shape
    return pl.pallas_call(
        _rmsnorm_kernel,
        grid=(tp // tr,),
        in_specs=[pl.BlockSpec((tr, d), lambda i: (i, 0)),
                  pl.BlockSpec((1, d), lambda i: (0, 0))],
        out_specs=pl.BlockSpec((tr, d), lambda i: (i, 0)),
        out_shape=jax.ShapeDtypeStruct((tp, d), BF16),
        compiler_params=_cparams(("parallel",)),
        name="rmsnorm1",
    )(x, g.reshape(1, d))


def _inproj_kernel(x_ref, w_ref, b_ref, o_ref):
    o_ref[...] = _dot(x_ref[...], w_ref[...].astype(BF16)) + b_ref[...]


def _inproj(h, w, b, tm, tn):
    tp, k = h.shape
    n = w.shape[1]
    return pl.pallas_call(
        _inproj_kernel,
        grid=(n // tn, tp // tm),
        in_specs=[pl.BlockSpec((tm, k), lambda j, i: (i, 0)),
                  pl.BlockSpec((k, tn), lambda j, i: (0, j)),
                  pl.BlockSpec((1, tn), lambda j, i: (0, j))],
        out_specs=pl.BlockSpec((tm, tn), lambda j, i: (i, j)),
        out_shape=jax.ShapeDtypeStruct((tp, n), F32),
        compiler_params=_cparams(("parallel", "parallel")),
        name="in_proj",
    )(h, w, b.reshape(1, n))


def _glu_kernel(x_ref, w_ref, b_ref, xe_ref, o_ref):
    a = _dot(x_ref[...].astype(BF16), w_ref[...].astype(BF16)) + b_ref[...]
    o_ref[...] = (xe_ref[...] * jax.nn.sigmoid(a)).astype(o_ref.dtype)


def _glu(ys, w, b, tm, tn):
    tp, k = ys.shape
    n = w.shape[1]
    return pl.pallas_call(
        _glu_kernel,
        grid=(n // tn, tp // tm),
        in_specs=[pl.BlockSpec((tm, k), lambda j, i: (i, 0)),
                  pl.BlockSpec((k, tn), lambda j, i: (0, j)),
                  pl.BlockSpec((1, tn), lambda j, i: (0, j)),
                  pl.BlockSpec((tm, tn), lambda j, i: (i, j))],
        out_specs=pl.BlockSpec((tm, tn), lambda j, i: (i, j)),
        out_shape=jax.ShapeDtypeStruct((tp, n), BF16),
        compiler_params=_cparams(("parallel", "parallel")),
        name="glu",
    )(ys, w, b.reshape(1, n), ys)


def _merge_kernel(ys_ref, oa_ref, ws_ref, wa_ref, gs_ref, ga_ref, o_ref):
    a = _dot(ys_ref[...], ws_ref[...].astype(BF16))
    b = _dot(oa_ref[...], wa_ref[...].astype(BF16))
    m = jax.nn.sigmoid(gs_ref[...]) * a + jax.nn.sigmoid(ga_ref[...]) * b
    o_ref[...] = m.astype(o_ref.dtype)


def _merge(ysg, oatt, w_ssm, w_att, z, tm, tn):
    tp, k = ysg.shape
    n = w_ssm.shape[1]
    gs_blk = (D_SSM + 3 * D_ATT) // tn
    ga_blk = (D_SSM + 3 * D_ATT + D_MODEL) // tn
    return pl.pallas_call(
        _merge_kernel,
        grid=(n // tn, tp // tm),
        in_specs=[pl.BlockSpec((tm, k), lambda j, i: (i, 0)),
                  pl.BlockSpec((tm, k), lambda j, i: (i, 0)),
                  pl.BlockSpec((k, tn), lambda j, i: (0, j)),
                  pl.BlockSpec((k, tn), lambda j, i: (0, j)),
                  pl.BlockSpec((tm, tn), lambda j, i: (i, gs_blk + j)),
                  pl.BlockSpec((tm, tn), lambda j, i: (i, ga_blk + j))],
        out_specs=pl.BlockSpec((tm, tn), lambda j, i: (i, j)),
        out_shape=jax.ShapeDtypeStruct((tp, n), BF16),
        compiler_params=_cparams(("parallel", "parallel")),
        name="merge_proj",
    )(ysg, oatt, w_ssm, w_att, z, z)


def _outproj_kernel(m_ref, w_ref, x_ref, o_ref):
    o_ref[...] = x_ref[...] + _dot(m_ref[...], w_ref[...].astype(BF16))


def _outproj(m, w, x, tm, tn):
    tp, k = m.shape
    n = w.shape[1]
    return pl.pallas_call(
        _outproj_kernel,
        grid=(n // tn, tp // tm),
        in_specs=[pl.BlockSpec((tm, k), lambda j, i: (i, 0)),
                  pl.BlockSpec((k, tn), lambda j, i: (0, j)),
                  pl.BlockSpec((tm, tn), lambda j, i: (i, j))],
        out_specs=pl.BlockSpec((tm, tn), lambda j, i: (i, j)),
        out_shape=jax.ShapeDtypeStruct((tp, n), F32),
        compiler_params=_cparams(("parallel", "parallel")),
        name="out_proj",
    )(m, w, x)


def _segment_sumsq(x, ones_bd):
    x2 = x * x
    hi = x2.astype(BF16)
    lo = (x2 - hi.astype(F32)).astype(BF16)
    return _dot(hi, ones_bd) + _dot(lo, ones_bd)


def _qk_rope_kernel(zq_ref, zk_ref, zv_ref, cos_ref, sin_ref, gq_ref, gk_ref, ones_ref,
                    qb_ref, kf_ref, kb_ref, vf_ref, vb_ref):
    cos = cos_ref[...]
    sin = sin_ref[...]
    ones_bd = ones_ref[...]
    lane = lax.broadcasted_iota(jnp.int32, cos.shape, 1)
    first_half = (lane % HEAD_DIM) < (HEAD_DIM // 2)

    def norm_rope(x, g):
        ss = _segment_sumsq(x, ones_bd)
        xn = x * lax.rsqrt(ss * (1.0 / HEAD_DIM) + EPS) * g
        partner = jnp.where(first_half,
                            pltpu.roll(xn, LANES - HEAD_DIM // 2, 1),
                            pltpu.roll(xn, HEAD_DIM // 2, 1))
        return xn * cos + partner * sin

    for h in range(N_HEADS):
        sl = slice(h * LANES, (h + 1) * LANES)
        q = norm_rope(zq_ref[:, sl], gq_ref[...])
        qb_ref[:, sl] = (q * (HEAD_DIM ** -0.5)).astype(BF16)
        k = norm_rope(zk_ref[:, sl], gk_ref[...])
        kf_ref[:, sl] = k
        kb_ref[:, sl] = k.astype(BF16)
    v = zv_ref[...]
    vf_ref[...] = v
    vb_ref[...] = v.astype(BF16)


def _qk_rope(z, cos_t, sin_t, gq, gk, tr):
    tp = z.shape[0]
    ones_bd = jnp.kron(jnp.eye(LANES // HEAD_DIM, dtype=F32),
                       jnp.ones((HEAD_DIM, HEAD_DIM), F32)).astype(BF16)
    zspec = lambda c: pl.BlockSpec((tr, D_ATT), lambda i: (i, c))
    row = pl.BlockSpec((tr, LANES), lambda i: (i, 0))
    const = pl.BlockSpec((1, LANES), lambda i: (0, 0))
    out = pl.BlockSpec((tr, D_ATT), lambda i: (i, 0))
    q_blk = D_SSM // D_ATT
    return pl.pallas_call(
        _qk_rope_kernel,
        grid=(tp // tr,),
        in_specs=[zspec(q_blk), zspec(q_blk + 1), zspec(q_blk + 2), row, row, const, const,
                  pl.BlockSpec((LANES, LANES), lambda i: (0, 0))],
        out_specs=[out, out, out, out, out],
        out_shape=[jax.ShapeDtypeStruct((tp, D_ATT), BF16),
                   jax.ShapeDtypeStruct((tp, D_ATT), F32),
                   jax.ShapeDtypeStruct((tp, D_ATT), BF16),
                   jax.ShapeDtypeStruct((tp, D_ATT), F32),
                   jax.ShapeDtypeStruct((tp, D_ATT), BF16)],
        compiler_params=_cparams(("parallel",)),
        name="qk_norm_rope",
    )(z, z, z, cos_t, sin_t, gq, gk, ones_bd)


def _softmax_step(c, qc, kt, vt, mask, m_ref, l_ref, acc_ref):
    s = _dot_nt(qc, kt)
    if mask is not None:
        s = jnp.where(mask, s, -jnp.inf)
    m_prev = m_ref[c]
    m_new = jnp.maximum(m_prev, jnp.max(s, axis=1, keepdims=True))
    alpha = jnp.exp(m_prev - m_new)
    p = jnp.exp(s - m_new[:, :1])
    l_ref[c] = alpha * l_ref[c] + jnp.sum(p, axis=1, keepdims=True)
    acc_ref[c] = alpha * acc_ref[c] + _dot(p.astype(BF16), vt)
    m_ref[c] = m_new


def _diff_finish(o0, o1, lam, g, out_scale):
    o = o0 - lam * o1
    ms = jnp.mean(o * o, axis=-1, keepdims=True)
    return o * lax.rsqrt(ms + EPS) * g * out_scale


def _split_components(q):
    lane = lax.broadcasted_iota(jnp.int32, q.shape, 1)
    zero = jnp.zeros_like(q)
    return jnp.where(lane < HEAD_DIM, q, zero), jnp.where(lane >= HEAD_DIM, q, zero)


def _attn_prompt_kernel(lam_ref, g_ref, q_ref, k_ref, v_ref, o_ref, m_ref, l_ref, acc_ref,
                        *, tq, nq_main, seq, out_scale):
    i = pl.program_id(1)
    q0, q1 = _split_components(q_ref[...])
    m_ref[...] = jnp.full(m_ref.shape, -jnp.inf, F32)
    l_ref[...] = jnp.zeros(l_ref.shape, F32)
    acc_ref[...] = jnp.zeros(acc_ref.shape, F32)

    def update(kt, vt, mask):
        _softmax_step(0, q0, kt, vt, mask, m_ref, l_ref, acc_ref)
        _softmax_step(1, q1, kt, vt, mask, m_ref, l_ref, acc_ref)

    col = lax.broadcasted_iota(jnp.int32, (tq, LANES), 1)
    update(k_ref[pl.ds(seq, LANES), :], v_ref[pl.ds(seq, LANES), :], col < N_META)

    is_main = i < nq_main

    def body(j, carry):
        start = pl.multiple_of(j * tq, tq)
        update(k_ref[pl.ds(start, tq), :], v_ref[pl.ds(start, tq), :], None)
        return carry

    lax.fori_loop(0, jnp.where(is_main, i, 0), body, 0)

    @pl.when(is_main)
    def _():
        start = pl.multiple_of(i * tq, tq)
        r = lax.broadcasted_iota(jnp.int32, (tq, tq), 0) // CHUNK
        c = lax.broadcasted_iota(jnp.int32, (tq, tq), 1) // CHUNK
        update(k_ref[pl.ds(start, tq), :], v_ref[pl.ds(start, tq), :], c <= r)

    o0 = acc_ref[0] / l_ref[0]
    o1 = acc_ref[1] / l_ref[1]
    o_ref[...] = _diff_finish(o0, o1, lam_ref[...], g_ref[...], out_scale).astype(o_ref.dtype)


def _attn_prompt(qb, kb, vb, lam_row, subln_g, seq, out_scale, tq):
    tp = qb.shape[0]
    kern = functools.partial(_attn_prompt_kernel, tq=tq, nq_main=seq // tq, seq=seq,
                             out_scale=out_scale)
    const = pl.BlockSpec((1, LANES), lambda h, i: (0, 0))
    return pl.pallas_call(
        kern,
        grid=(N_HEADS, tp // tq),
        in_specs=[const, const,
                  pl.BlockSpec((tq, LANES), lambda h, i: (i, h)),
                  pl.BlockSpec((tp, LANES), lambda h, i: (0, h)),
                  pl.BlockSpec((tp, LANES), lambda h, i: (0, h))],
        out_specs=pl.BlockSpec((tq, LANES), lambda h, i: (i, h)),
        out_shape=jax.ShapeDtypeStruct((tp, D_ATT), BF16),
        scratch_shapes=[pltpu.VMEM((2, tq, LANES), F32),
                        pltpu.VMEM((2, tq, LANES), F32),
                        pltpu.VMEM((2, tq, LANES), F32)],
        compiler_params=_cparams(("parallel", "parallel")),
        name="attn_prompt",
    )(lam_row, subln_g, qb, kb, vb)


LOGIT_BOUND_MAX = 40.0
ATTN_WIDE = 4


def _with_ones(vt):
    return jnp.concatenate([vt, jnp.ones(vt.shape, vt.dtype)], axis=1)


def _cache_v_head(vc_ref, h):
    tk = vc_ref.shape[0] // N_HEADS
    return vc_ref[pl.ds(h, tk, stride=N_HEADS), :].astype(BF16)


def _attn_prompt_fast_kernel(lam_ref, g_ref, q_ref, k_ref, v_ref, o_ref, acc_ref,
                             *, tq, nq_main, seq, out_scale):
    i = pl.program_id(1)
    q0, q1 = _split_components(q_ref[...])
    qq = jnp.concatenate([q0, q1], axis=0)

    def scores(start, rows):
        return _dot_nt(qq, k_ref[pl.ds(start, rows), :])

    def weighted(s, start, rows, mask):
        p = jnp.exp(s)
        if mask is not None:
            p = jnp.where(mask, p, 0.0)
        return _dot(p.astype(BF16), _with_ones(v_ref[pl.ds(start, rows), :]))

    def tile_pv(start, rows, mask):
        return weighted(scores(start, rows), start, rows, mask)

    col = lax.broadcasted_iota(jnp.int32, (2 * tq, LANES), 1)
    acc_ref[...] = tile_pv(seq, LANES, col < N_META)

    is_main = i < nq_main
    n_full = jnp.where(is_main, i, 0)

    wide = ATTN_WIDE

    def body(j, carry):
        start = pl.multiple_of(wide * j * tq, tq)
        acc_ref[...] += tile_pv(start, wide * tq, None)
        return carry

    n_wide = n_full // wide
    lax.fori_loop(0, n_wide, body, 0)
    rem = n_full - wide * n_wide
    base = wide * n_wide

    @pl.when(rem >= 2)
    def _():
        acc_ref[...] += tile_pv(pl.multiple_of(base * tq, tq), 2 * tq, None)

    @pl.when(rem % 2 == 1)
    def _():
        acc_ref[...] += tile_pv(pl.multiple_of((n_full - 1) * tq, tq), tq, None)

    @pl.when(is_main)
    def _():
        r = (lax.broadcasted_iota(jnp.int32, (2 * tq, tq), 0) % tq) // CHUNK
        c = lax.broadcasted_iota(jnp.int32, (2 * tq, tq), 1) // CHUNK
        acc_ref[...] += tile_pv(pl.multiple_of(i * tq, tq), tq, c <= r)

    acc = acc_ref[...]
    o0 = acc[:tq, :LANES] / acc[:tq, LANES:]
    o1 = acc[tq:, :LANES] / acc[tq:, LANES:]
    o_ref[...] = _diff_finish(o0, o1, lam_ref[...], g_ref[...], out_scale).astype(o_ref.dtype)


def _attn_prompt_fast(qb, kb, vb, lam_row, subln_g, seq, out_scale, tq):
    tp = qb.shape[0]
    kern = functools.partial(_attn_prompt_fast_kernel, tq=tq, nq_main=seq // tq, seq=seq,
                             out_scale=out_scale)
    const = pl.BlockSpec((1, LANES), lambda h, i: (0, 0))
    return pl.pallas_call(
        kern,
        grid=(N_HEADS, tp // tq),
        in_specs=[const, const,
                  pl.BlockSpec((tq, LANES), lambda h, i: (i, h)),
                  pl.BlockSpec((tp, LANES), lambda h, i: (0, h)),
                  pl.BlockSpec((tp, LANES), lambda h, i: (0, h))],
        out_specs=pl.BlockSpec((tq, LANES), lambda h, i: (i, h)),
        out_shape=jax.ShapeDtypeStruct((tp, D_ATT), BF16),
        scratch_shapes=[pltpu.VMEM((2 * tq, 2 * LANES), F32)],
        compiler_params=_cparams(("parallel", "parallel")),
        name="attn_prompt_fast",
    )(lam_row, subln_g, qb, kb, vb)


def _attn_sample_fast_kernel(lam_ref, g_ref, q_ref, kc_ref, vc_ref, kn_ref, vn_ref, o_in_ref, o_ref,
                             acc_ref, *, nq, out_scale):
    del o_in_ref
    j = pl.program_id(1)
    rows = 2 * nq

    def head_update(h, kt, vt, mask):
        p = jnp.exp(_dot_nt(q_ref[h], kt))
        if mask is not None:
            p = jnp.where(mask, p, 0.0)
        acc_ref[h] += _dot(p.astype(BF16), _with_ones(vt))

    @pl.when(j == 0)
    def _():
        acc_ref[...] = jnp.zeros(acc_ref.shape, F32)
        col = lax.broadcasted_iota(jnp.int32, (rows, LANES), 1)
        for h in range(N_HEADS):
            sl = slice(h * LANES, (h + 1) * LANES)
            head_update(h, kn_ref[:, sl], vn_ref[:, sl], col < nq)

    for h in range(N_HEADS):
        sl = slice(h * LANES, (h + 1) * LANES)
        head_update(h, kc_ref[:, sl].astype(BF16), _cache_v_head(vc_ref, h), None)

    @pl.when(j == pl.num_programs(1) - 1)
    def _():
        for h in range(N_HEADS):
            acc = acc_ref[h]
            o = acc[:, :LANES] / acc[:, LANES:]
            res = _diff_finish(o[:nq], o[nq:], lam_ref[...], g_ref[...], out_scale)
            o_ref[:, h * LANES:(h + 1) * LANES] = res.astype(o_ref.dtype)


def _attn_sample_kernel(lam_ref, g_ref, q_ref, kc_ref, vc_ref, kn_ref, vn_ref, o_in_ref, o_ref,
                        m_ref, l_ref, acc_ref, *, nq, out_scale):
    del o_in_ref
    j = pl.program_id(1)
    rows = 2 * nq

    @pl.when(j == 0)
    def _():
        m_ref[...] = jnp.full(m_ref.shape, -jnp.inf, F32)
        l_ref[...] = jnp.zeros(l_ref.shape, F32)
        acc_ref[...] = jnp.zeros(acc_ref.shape, F32)
        col = lax.broadcasted_iota(jnp.int32, (rows, LANES), 1)
        for h in range(N_HEADS):
            sl = slice(h * LANES, (h + 1) * LANES)
            _softmax_step(h, q_ref[h], kn_ref[:, sl], vn_ref[:, sl], col < nq,
                          m_ref, l_ref, acc_ref)

    for h in range(N_HEADS):
        sl = slice(h * LANES, (h + 1) * LANES)
        _softmax_step(h, q_ref[h], kc_ref[:, sl].astype(BF16), _cache_v_head(vc_ref, h), None,
                      m_ref, l_ref, acc_ref)

    @pl.when(j == pl.num_programs(1) - 1)
    def _():
        for h in range(N_HEADS):
            o = acc_ref[h] / l_ref[h]
            res = _diff_finish(o[:nq], o[nq:], lam_ref[...], g_ref[...], out_scale)
            o_ref[:, h * LANES:(h + 1) * LANES] = res.astype(o_ref.dtype)


def _attn_sample(qz, cache_k, cache_v, k_new, v_new, o_buf, lam_row, subln_g, row_off, out_scale, tk, fast):
    nb, past = cache_k.shape[0], cache_k.shape[1]
    nq = qz.shape[2] // 2
    if fast:
        kern = functools.partial(_attn_sample_fast_kernel, nq=nq, out_scale=out_scale)
        scratch = [pltpu.VMEM((N_HEADS, 2 * nq, 2 * LANES), F32)]
    else:
        kern = functools.partial(_attn_sample_kernel, nq=nq, out_scale=out_scale)
        scratch = [pltpu.VMEM((N_HEADS, 2 * nq, LANES), F32)] * 3
    const = pl.BlockSpec((1, LANES), lambda b, j: (0, 0))
    blk_off = row_off // nq
    return pl.pallas_call(
        kern,
        grid=(nb, past // tk),
        in_specs=[const, const,
                  pl.BlockSpec((None, N_HEADS, 2 * nq, LANES), lambda b, j: (b, 0, 0, 0)),
                  pl.BlockSpec((None, tk, D_ATT), lambda b, j: (b, j, 0)),
                  pl.BlockSpec((None, tk * N_HEADS, V_DIM), lambda b, j: (b, j, 0)),
                  pl.BlockSpec((None, LANES, D_ATT), lambda b, j: (b, 0, 0)),
                  pl.BlockSpec((None, LANES, D_ATT), lambda b, j: (b, 0, 0)),
                  pl.BlockSpec(memory_space=pl.ANY)],
        out_specs=pl.BlockSpec((nq, D_ATT), lambda b, j: (blk_off + b, 0)),
        out_shape=jax.ShapeDtypeStruct(o_buf.shape, o_buf.dtype),
        scratch_shapes=scratch,
        input_output_aliases={7: 0},
        compiler_params=_cparams(("parallel", "arbitrary")),
        name="attn_sample_fast" if fast else "attn_sample",
    )(lam_row, subln_g, qz, cache_k, cache_v, k_new, v_new, o_buf)


def _s5_weights(a_re, a_im, log_dt, b_re, b_im, c_re, c_im):
    hp = lax.Precision.HIGHEST
    n_t, gl, tc = S5_TILES, S5_LANE_GROUPS, S5_CHUNK
    dt = jnp.exp(log_dt)[:, None]
    mag = jnp.exp(a_re * dt)
    abar_re = mag * jnp.cos(a_im * dt)
    abar_im = mag * jnp.sin(a_im * dt)
    nr, ni = abar_re - 1.0, abar_im
    den = a_re * a_re + a_im * a_im
    coef_re = (nr * a_re + ni * a_im) / den
    coef_im = (ni * a_re - nr * a_im) / den
    bbar_re = coef_re[..., None] * b_re - coef_im[..., None] * b_im
    bbar_im = coef_re[..., None] * b_im + coef_im[..., None] * b_re
    n = jnp.arange(tc + 1, dtype=F32)[:, None, None]
    pw_mag = jnp.exp(n * (a_re * dt))
    pw_re = pw_mag * jnp.cos(n * (a_im * dt))
    pw_im = pw_mag * jnp.sin(n * (a_im * dt))
    e_re = pw_re[:tc, :, :, None] * bbar_re - pw_im[:tc, :, :, None] * bbar_im
    e_im = pw_re[:tc, :, :, None] * bbar_im + pw_im[:tc, :, :, None] * bbar_re
    kern = (jnp.einsum('gcp,lgpd->glcd', c_re, e_re, precision=hp)
            - jnp.einsum('gcp,lgpd->glcd', c_im, e_im, precision=hp))
    eye = jnp.eye(gl, dtype=F32)
    w_intra = jnp.einsum('jglcd,gh->jlgdhc', kern.reshape(n_t, gl, tc, SSM_GROUP, SSM_GROUP), eye)
    w_intra = w_intra.reshape(n_t, tc, LANES, LANES)
    eb = jnp.stack([e_re[::-1], e_im[::-1]], 0)
    eb = eb.reshape(2, tc, n_t, gl, SSM_STATE, SSM_GROUP)
    w_state = eb.transpose(2, 1, 3, 5, 0, 4).reshape(n_t, tc, LANES, 2 * SSM_STATE)
    cp_re = c_re[None] * pw_re[1:, :, None, :] - c_im[None] * pw_im[1:, :, None, :]
    cp_im = c_re[None] * pw_im[1:, :, None, :] + c_im[None] * pw_re[1:, :, None, :]
    cp = jnp.stack([cp_re, -cp_im], 0).reshape(2, tc, n_t, gl, SSM_GROUP, SSM_STATE)
    w_read = cp.transpose(2, 1, 0, 5, 3, 4).reshape(n_t, tc, 2 * SSM_STATE, LANES)
    half = gl * SSM_STATE
    a_pow = jnp.concatenate([pw_re[tc].reshape(n_t, 1, half), pw_im[tc].reshape(n_t, 1, half)], -1)
    rp = jnp.arange(2 * SSM_STATE)
    col = jnp.arange(2 * half)
    spread = ((rp[:, None] // SSM_STATE == col[None, :] // half)
              & (rp[:, None] % SSM_STATE == col[None, :] % SSM_STATE)).astype(BF16)
    return (w_intra.astype(BF16), w_state.astype(BF16), w_read.astype(BF16), a_pow, spread, spread.T)


def _s5_kernel(u_ref, wi_ref, wsc_ref, wrc_ref, ap_ref, sp_ref, spt_ref, d_ref, h0r_ref, h0i_ref,
               y_ref, hpr_ref, hpi_ref, hsr_ref, hsi_ref,
               y_acc, v_ref, hs_ref, wt_ref, ws_ref, wr_ref,
               *, nc, n_main, n_meta_chunks, n_seq, seq_chunks):
    tc = S5_CHUNK
    half = hs_ref.shape[1] // 2
    grp_r = lax.broadcasted_iota(jnp.int32, (LANES, 2 * half), 0) // SSM_GROUP
    grp_c = (lax.broadcasted_iota(jnp.int32, (LANES, 2 * half), 1) % half) // SSM_STATE
    for s in range(tc):
        full = _dot(wsc_ref[s], sp_ref[...])
        ws_ref[s * LANES:(s + 1) * LANES, :] = jnp.where(grp_r == grp_c, full, 0.0).astype(BF16)
    grp_r = (lax.broadcasted_iota(jnp.int32, (2 * half, LANES), 0) % half) // SSM_STATE
    grp_c = lax.broadcasted_iota(jnp.int32, (2 * half, LANES), 1) // SSM_GROUP
    for t in range(tc):
        full = _dot(spt_ref[...], wrc_ref[t])
        wr_ref[:, t * LANES:(t + 1) * LANES] = jnp.where(grp_r == grp_c, full, 0.0).astype(BF16)
    for s in range(tc):
        for t in range(tc):
            blk = wi_ref[t - s] if t >= s else jnp.zeros((LANES, LANES), BF16)
            wt_ref[s * LANES:(s + 1) * LANES, t * LANES:(t + 1) * LANES] = blk
    lhs = jnp.concatenate(
        [u_ref[pl.ds(s, nc, stride=tc), :].astype(BF16) for s in range(tc)], axis=1)
    y_acc[...] = _dot(lhs, wt_ref[...])
    v_ref[...] = _dot(lhs, ws_ref[...])
    a_re = ap_ref[:, :half]
    a_im = ap_ref[:, half:]

    def advance(h_re, h_im, v):
        return (a_re * h_re - a_im * h_im + v[:, :half],
                a_re * h_im + a_im * h_re + v[:, half:])

    hs_ref[...] = jnp.zeros(hs_ref.shape, F32)

    h_re = jnp.zeros((1, half), F32)
    h_im = jnp.zeros((1, half), F32)
    for c in range(n_main, n_main + n_meta_chunks):
        hs_ref[pl.ds(c, 1), :] = jnp.concatenate([h_re, h_im], axis=1)
        h_re, h_im = advance(h_re, h_im, v_ref[pl.ds(c, 1), :])

    def body(c, carry):
        h_re, h_im = carry
        hs_ref[pl.ds(c, 1), :] = jnp.concatenate([h_re, h_im], axis=1)
        return advance(h_re, h_im, v_ref[pl.ds(c, 1), :])

    h_re, h_im = lax.fori_loop(0, n_main, body, (h_re, h_im))
    hpr_ref[...] = h_re
    hpi_ref[...] = h_im

    base = n_main + n_meta_chunks
    for b in range(n_seq):
        s_re = h0r_ref[pl.ds(b, 1), :]
        s_im = h0i_ref[pl.ds(b, 1), :]
        for c in range(base + b * seq_chunks, base + (b + 1) * seq_chunks):
            hs_ref[pl.ds(c, 1), :] = jnp.concatenate([s_re, s_im], axis=1)
            s_re, s_im = advance(s_re, s_im, v_ref[pl.ds(c, 1), :])
        hsr_ref[pl.ds(b, 1), :] = s_re
        hsi_ref[pl.ds(b, 1), :] = s_im

    y_acc[...] += _dot(hs_ref[...].astype(BF16), wr_ref[...])
    d = d_ref[...]
    for t in range(tc):
        rows = pl.ds(t, nc, stride=tc)
        y = y_acc[:, t * LANES:(t + 1) * LANES] + d * u_ref[rows, :]
        y_ref[rows, :] = jax.nn.gelu(y)


def _s5(z, weights, d_skip, h0_re, h0_im, seq, n_seq, seq_len):
    tp = z.shape[0]
    tc = S5_CHUNK
    nc = tp // tc
    w_intra, w_state, w_read, a_pow, spread, spread_t = weights
    half = S5_LANE_GROUPS * SSM_STATE
    whole = lambda a: pl.BlockSpec(a.shape, lambda j: (0,) * a.ndim)
    kern = functools.partial(_s5_kernel, nc=nc, n_main=seq // tc, n_meta_chunks=N_META // tc,
                             n_seq=n_seq, seq_chunks=seq_len // tc)
    wspec = lambda a: pl.BlockSpec((None,) + a.shape[1:], lambda j: (j,) + (0,) * (a.ndim - 1))
    col = pl.BlockSpec((tp, LANES), lambda j: (0, j))
    st = lambda r: pl.BlockSpec((r, half), lambda j: (0, j))
    gp = N_SSM_GROUPS * SSM_STATE
    return pl.pallas_call(
        kern,
        grid=(S5_TILES,),
        in_specs=[col, wspec(w_intra), wspec(w_state), wspec(w_read), wspec(a_pow),
                  whole(spread), whole(spread_t),
                  pl.BlockSpec((1, LANES), lambda j: (0, j)), st(n_seq), st(n_seq)],
        out_specs=[col, st(1), st(1), st(n_seq), st(n_seq)],
        out_shape=[jax.ShapeDtypeStruct((tp, D_SSM), F32),
                   jax.ShapeDtypeStruct((1, gp), F32),
                   jax.ShapeDtypeStruct((1, gp), F32),
                   jax.ShapeDtypeStruct((n_seq, gp), F32),
                   jax.ShapeDtypeStruct((n_seq, gp), F32)],
        scratch_shapes=[pltpu.VMEM((nc, tc * LANES), F32),
                        pltpu.VMEM((nc, 2 * half), F32),
                        pltpu.VMEM((nc, 2 * half), F32),
                        pltpu.VMEM((tc * LANES, tc * LANES), BF16),
                        pltpu.VMEM((tc * LANES, 2 * half), BF16),
                        pltpu.VMEM((2 * half, tc * LANES), BF16)],
        compiler_params=_cparams(("parallel",)),
        name="s5_scan",
    )(z, w_intra, w_state, w_read, a_pow, spread, spread_t, d_skip.reshape(1, D_SSM), h0_re, h0_im)


def _router_kernel(x_ref, g_ref, w_ref, b_ref, h_ref, e_ref, gate_ref):
    x = x_ref[...]
    ms = jnp.mean(x * x, axis=-1, keepdims=True)
    h = x * lax.rsqrt(ms + EPS) * g_ref[...]
    h_ref[...] = h.astype(h_ref.dtype)
    logits = _dot(h.astype(BF16), w_ref[...]) + b_ref[...]
    lane = lax.broadcasted_iota(jnp.int32, logits.shape, 1)
    neg = -jnp.inf
    big = jnp.int32(LANES)

    def first_argmax(vals, vmax):
        return jnp.min(jnp.where(vals == vmax, lane, big), axis=1, keepdims=True)

    lg = jnp.where(lane < N_EGROUPS, logits, neg)
    mg = jnp.max(lg, axis=1, keepdims=True)
    sg = jnp.sum(jnp.exp(lg - mg), axis=1, keepdims=True)
    g_w = 1.0 / sg
    g_idx = first_argmax(lg, mg)
    lo = N_EGROUPS + EXPERTS_PER_GROUP * g_idx
    le = jnp.where((lane >= lo) & (lane < lo + EXPERTS_PER_GROUP), logits, neg)
    m1 = jnp.max(le, axis=1, keepdims=True)
    se = jnp.sum(jnp.exp(le - m1), axis=1, keepdims=True)
    i1 = first_argmax(le, m1)
    le2 = jnp.where(lane == i1, neg, le)
    m2 = jnp.max(le2, axis=1, keepdims=True)
    i2 = first_argmax(le2, m2)
    p1 = 1.0 / se
    p2 = jnp.exp(m2 - m1) / se
    tot = p1 + p2
    w1 = g_w * (p1 / tot)
    w2 = g_w * (p2 / tot)
    e_ref[...] = jnp.where(lane == 0, i1 - N_EGROUPS, jnp.where(lane == 1, i2 - N_EGROUPS, 0))
    gate_ref[...] = jnp.where(lane == 0, w1, jnp.where(lane == 1, w2, 0.0))


def _router(x2, g, w_r, b_r, tr):
    tp, d = x2.shape
    return pl.pallas_call(
        _router_kernel,
        grid=(tp // tr,),
        in_specs=[pl.BlockSpec((tr, d), lambda i: (i, 0)),
                  pl.BlockSpec((1, d), lambda i: (0, 0)),
                  pl.BlockSpec((d, LANES), lambda i: (0, 0)),
                  pl.BlockSpec((1, LANES), lambda i: (0, 0))],
        out_specs=[pl.BlockSpec((tr, d), lambda i: (i, 0)),
                   pl.BlockSpec((tr, LANES), lambda i: (i, 0)),
                   pl.BlockSpec((tr, LANES), lambda i: (i, 0))],
        out_shape=[jax.ShapeDtypeStruct((tp, d), F32),
                   jax.ShapeDtypeStruct((tp, LANES), jnp.int32),
                   jax.ShapeDtypeStruct((tp, LANES), F32)],
        compiler_params=_cparams(("parallel",)),
        name="norm2_router",
    )(x2, g.reshape(1, d), w_r, b_r)


def _expert_kernel(be_ref, nu_ref, tok_ref, tokn_ref, dst_ref, h_hbm, g_ref, w1_ref, w3_ref, w2_ref,
                   y_hbm, xbuf, xb16, hid_ref, ybuf, gsem, ssem, *, nb, plane_rows, plane_pad):
    b = pl.program_id(0)
    hh = pl.program_id(1)
    last = pl.num_programs(1) - 1
    n_used = nu_ref[0]
    active = b < n_used
    slot = b % 2
    rows = xb16.shape[0]

    def start_gather(idx_ref, s):
        for r in range(rows):
            pltpu.make_async_copy(h_hbm.at[pl.ds(idx_ref[0, r], 1), :],
                                  xbuf.at[s, pl.ds(r, 1), :], gsem.at[s]).start()

    def all_rows_gathered(s):
        return pltpu.make_async_copy(h_hbm.at[pl.ds(0, rows), :], xbuf.at[s], gsem.at[s])

    def all_rows_scattered():
        return pltpu.make_async_copy(ybuf, y_hbm.at[pl.ds(TOP_K * plane_rows, rows), :], ssem)

    @pl.when((b == 0) & (hh == 0))
    def _():
        ybuf[...] = jnp.zeros(ybuf.shape, F32)
        fills = [all_rows_scattered()]
        if plane_pad:
            fills += [pltpu.make_async_copy(ybuf.at[pl.ds(0, plane_pad), :],
                                            y_hbm.at[pl.ds((k + 1) * plane_rows - plane_pad, plane_pad), :], ssem)
                      for k in range(TOP_K)]
        for f in fills:
            f.start()
        for f in fills:
            f.wait()

        @pl.when(active)
        def _():
            start_gather(tok_ref, 0)

    @pl.when(active & (hh == 0))
    def _():
        all_rows_gathered(slot).wait()
        xb16[...] = xbuf[slot].astype(BF16)

    @pl.when(active & (hh == 1) & (b + 1 < n_used))
    def _():
        start_gather(tokn_ref, 1 - slot)

    @pl.when(active)
    def _():
        dh = w1_ref.shape[1]
        w13 = jnp.concatenate([w1_ref[...].astype(BF16), w3_ref[...].astype(BF16)], axis=1)
        ac = _dot(xb16[...], w13)
        hid_ref[_expert_slice(b, hh)] = (jax.nn.silu(ac[:, :dh]) * ac[:, dh:]).astype(BF16)

    @pl.when(active & (hh == last))
    def _():
        @pl.when(b > 0)
        def _():
            all_rows_scattered().wait()

        hid = jnp.concatenate([hid_ref[q] for q in range(EXPERT_SLICES)], axis=1)
        ybuf[...] = _dot(hid, w2_ref[...].astype(BF16)) * g_ref[...]
        for r in range(rows):
            pltpu.make_async_copy(ybuf.at[pl.ds(r, 1), :], y_hbm.at[pl.ds(dst_ref[0, r], 1), :],
                                  ssem).start()

        @pl.when(b + 1 >= n_used)
        def _():
            all_rows_scattered().wait()


def _expert_slice(b, hh):
    return jnp.where(b % 2 == 0, hh, EXPERT_SLICES - 1 - hh)


def _experts(h2, plan, t_real, w1, w3, w2):
    tp = h2.shape[0]
    assert tp - t_real <= MOE_BLOCK
    slot_tok, slot_dst, slot_gate, block_expert, n_used = plan
    d = h2.shape[1]
    nb = block_expert.shape[0]
    dh = D_EXPERT // EXPERT_SLICES

    def eidx(b, be, nu):
        return be[jnp.minimum(b, nu[0] - 1)]

    smem_row = pl.BlockSpec((None, 1, MOE_BLOCK), lambda b, hh, be, nu: (b, 0, 0),
                            memory_space=pltpu.SMEM)
    smem_next = pl.BlockSpec((None, 1, MOE_BLOCK),
                             lambda b, hh, be, nu: (jnp.minimum(b + 1, nb - 1), 0, 0),
                             memory_space=pltpu.SMEM)
    grid_spec = pltpu.PrefetchScalarGridSpec(
        num_scalar_prefetch=2,
        grid=(nb, EXPERT_SLICES),
        in_specs=[smem_row, smem_next, smem_row,
                  pl.BlockSpec(memory_space=pl.ANY),
                  pl.BlockSpec((MOE_BLOCK, 1), lambda b, hh, be, nu: (b, 0)),
                  pl.BlockSpec((None, d, dh), lambda b, hh, be, nu: (eidx(b, be, nu), 0, _expert_slice(b, hh))),
                  pl.BlockSpec((None, d, dh), lambda b, hh, be, nu: (eidx(b, be, nu), 0, _expert_slice(b, hh))),
                  pl.BlockSpec((None, D_EXPERT, d), lambda b, hh, be, nu: (eidx(b, be, nu), 0, 0))],
        out_specs=pl.BlockSpec(memory_space=pl.ANY),
        scratch_shapes=[pltpu.VMEM((2, MOE_BLOCK, d), F32),
                        pltpu.VMEM((MOE_BLOCK, d), BF16),
                        pltpu.VMEM((EXPERT_SLICES, MOE_BLOCK, dh), BF16),
                        pltpu.VMEM((MOE_BLOCK, d), F32),
                        pltpu.SemaphoreType.DMA((2,)),
                        pltpu.SemaphoreType.DMA(())],
    )
    return pl.pallas_call(
        functools.partial(_expert_kernel, nb=nb, plane_rows=tp, plane_pad=tp - t_real),
        grid_spec=grid_spec,
        out_shape=jax.ShapeDtypeStruct((TOP_K * tp + MOE_BLOCK, d), F32),
        compiler_params=_cparams(("arbitrary", "arbitrary"), 60 * 1024 * 1024),
        name="expert_mlp",
    )(block_expert, n_used, slot_tok.reshape(nb, 1, MOE_BLOCK), slot_tok.reshape(nb, 1, MOE_BLOCK),
      slot_dst.reshape(nb, 1, MOE_BLOCK), h2, slot_gate.reshape(-1, 1), w1, w3, w2)


def _route_plan(expert, gate, t_real, tp):
    s = t_real * TOP_K
    n_blocks = -(-(s + N_EXPERTS * (MOE_BLOCK - 1)) // MOE_BLOCK)
    p = n_blocks * MOE_BLOCK
    flat_e = expert.reshape(-1).astype(jnp.int32)
    flat_g = gate.reshape(-1).astype(F32)
    se, order, g_sorted = lax.sort((flat_e, jnp.arange(s, dtype=jnp.int32), flat_g), num_keys=1,
                                   is_stable=True)
    bounds = jnp.searchsorted(se, jnp.arange(N_EXPERTS + 1, dtype=jnp.int32)).astype(jnp.int32)
    start = bounds[:-1]
    counts = bounds[1:] - start
    padded = (counts + MOE_BLOCK - 1) // MOE_BLOCK * MOE_BLOCK
    pad_end = jnp.cumsum(padded)
    pad_start = pad_end - padded
    block_start = jnp.arange(n_blocks, dtype=jnp.int32) * MOE_BLOCK
    block_expert = jnp.minimum(jnp.searchsorted(pad_end, block_start, side='right'),
                               N_EXPERTS - 1).astype(jnp.int32)
    n_used = (pad_end[-1] // MOE_BLOCK).astype(jnp.int32).reshape(1)
    cnt = jnp.clip(counts[block_expert] - (block_start - pad_start[block_expert]), 0, MOE_BLOCK)
    first = jnp.clip(start[block_expert] + block_start - pad_start[block_expert], 0, s)
    run = lambda a: jax.vmap(lambda f: lax.dynamic_slice(a, (f,), (MOE_BLOCK,)))(first)
    flat = run(jnp.concatenate([order, jnp.zeros((MOE_BLOCK,), jnp.int32)]))
    g_run = run(jnp.concatenate([g_sorted, jnp.zeros((MOE_BLOCK,), F32)]))
    lane = jnp.arange(MOE_BLOCK, dtype=jnp.int32)[None, :]
    valid = lane < cnt[:, None]
    tok = flat // TOP_K
    slot_tok = jnp.where(valid, tok, 0)
    slot_dst = jnp.where(valid, (flat % TOP_K) * tp + tok, TOP_K * tp + lane)
    slot_gate = jnp.where(valid, g_run, 0.0)
    return (slot_tok.reshape(p), slot_dst.reshape(p), slot_gate.reshape(p), block_expert, n_used)


def _combine_kernel(x_ref, y0_ref, y1_ref, o_ref):
    o_ref[...] = x_ref[...] + y0_ref[...] + y1_ref[...]


def _combine(x2, y2, row_off, n_rows, tile):
    tp, d = x2.shape
    off = row_off // tile
    plane = tp // tile
    return pl.pallas_call(
        _combine_kernel,
        grid=(n_rows // tile,),
        in_specs=[pl.BlockSpec((tile, d), lambda i: (off + i, 0)),
                  pl.BlockSpec((tile, d), lambda i: (off + i, 0)),
                  pl.BlockSpec((tile, d), lambda i: (plane + off + i, 0))],
        out_specs=pl.BlockSpec((tile, d), lambda i: (i, 0)),
        out_shape=jax.ShapeDtypeStruct((n_rows, d), F32),
        compiler_params=_cparams(("parallel",)),
        name="moe_combine",
    )(x2, y2, y2)


def kernel(x_prompt, x_sample, cache_k, cache_v, state_ssm_re, state_ssm_im, meta_tokens, norm1_g, w_in, b_in, ssm_a_re, ssm_a_im, ssm_log_dt, ssm_b_re, ssm_b_im, ssm_c_re, ssm_c_im, ssm_d, w_glu, b_glu, w_ssm_proj, q_norm_g, k_norm_g, lam_q1, lam_k1, lam_q2, lam_k2, subln_g, w_att_proj, w_o, norm2_g, w_router_group, b_router_group, w_router_expert, b_router_expert, w1_e, w3_e, w2_e):
    assert x_prompt.shape[0] == 1 and w_in.shape[0] == 1
    seq = x_prompt.shape[1]
    nb, nq = x_sample.shape[0], x_sample.shape[1]
    past = cache_k.shape[2]
    n_s = nb * nq
    t_real = seq + N_META + n_s
    tp = -(-t_real // ROW_ALIGN) * ROW_ALIGN
    off_meta, off_s = seq, seq + N_META
    tq = 256
    assert seq % tq == 0 and nq == N_META and past % 512 == 0
    lam_init = 0.8 - 0.6 * math.exp(-0.3 * 0)
    out_scale = 1.0 - lam_init

    x_cat = jnp.concatenate([x_prompt[0], meta_tokens.astype(F32), x_sample.reshape(n_s, D_MODEL),
                             jnp.zeros((tp - t_real, D_MODEL), F32)], axis=0)

    tm = _row_tile(tp, 1088)
    tr = _row_tile(tp, 256)

    h1 = _rmsnorm(x_cat, norm1_g[0], tr)
    z = _inproj(h1, w_in[0], b_in[0], tm, 512)

    pos = jnp.concatenate([N_META + jnp.arange(seq), jnp.arange(N_META),
                           jnp.tile(past + jnp.arange(nq), nb),
                           jnp.zeros((tp - t_real,), jnp.int32)]).astype(F32)
    half = HEAD_DIM // 2
    inv = ROPE_THETA ** (-jnp.arange(half, dtype=F32) / half)
    ang = pos[:, None] * inv[None, :]
    cos_t = jnp.tile(jnp.cos(ang), (1, LANES // half))
    sin_h = jnp.sin(ang)
    sin_t = jnp.tile(jnp.concatenate([-sin_h, sin_h], axis=1), (1, LANES // HEAD_DIM))
    gq = jnp.tile(q_norm_g[0], LANES // HEAD_DIM).reshape(1, LANES)
    gk = jnp.tile(k_norm_g[0], LANES // HEAD_DIM).reshape(1, LANES)
    qb, kf, kb, vf, vb = _qk_rope(z, cos_t, sin_t, gq, gk, tr)

    lam = (jnp.exp(jnp.sum(lam_q1[0] * lam_k1[0])) - jnp.exp(jnp.sum(lam_q2[0] * lam_k2[0])) + lam_init)
    lam_row = jnp.full((1, LANES), lam, F32)
    sg = subln_g[0].reshape(1, LANES)

    q_s = qb[off_s:off_s + n_s].reshape(nb, nq, N_HEADS, LANES).transpose(0, 2, 1, 3)
    lane = jnp.arange(LANES)
    qz = jnp.concatenate([jnp.where(lane < HEAD_DIM, q_s, 0), jnp.where(lane >= HEAD_DIM, q_s, 0)], axis=2)
    pad_new = lambda a: jnp.pad(a[off_s:off_s + n_s].reshape(nb, nq, D_ATT), ((0, 0), (0, LANES - nq), (0, 0)))
    ck = cache_k[0].reshape(nb, past, D_ATT)
    cv = cache_v[0].reshape(nb, past * N_HEADS, V_DIM)

    logit_bound = 8.1 * jnp.max(jnp.abs(q_norm_g[0])) * jnp.max(jnp.abs(k_norm_g[0]))

    def attention(fast):
        def run():
            if fast:
                o = _attn_prompt_fast(qb, kb, vb, lam_row, sg, seq, out_scale, 512)
            else:
                o = _attn_prompt(qb, kb, vb, lam_row, sg, seq, out_scale, tq)
            return _attn_sample(qz, ck, cv, pad_new(kb), pad_new(vb), o, lam_row, sg, off_s, out_scale,
                                512, fast)
        return run

    o_att = lax.cond(logit_bound <= LOGIT_BOUND_MAX, attention(True), attention(False))

    gp = N_SSM_GROUPS * SSM_STATE
    s5w = _s5_weights(ssm_a_re[0], ssm_a_im[0], ssm_log_dt[0], ssm_b_re[0], ssm_b_im[0],
                      ssm_c_re[0], ssm_c_im[0])
    ys, hp_re, hp_im, hs_re, hs_im = _s5(z, s5w, ssm_d[0], state_ssm_re[0].reshape(nb, gp),
                                         state_ssm_im[0].reshape(nb, gp), seq, nb, nq)
    ysg = _glu(ys, w_glu[0], b_glu[0], tm, 512)
    m = _merge(ysg, o_att, w_ssm_proj[0], w_att_proj[0], z, tm, 512)
    x2 = _outproj(m, w_o[0], x_cat, tm, 512)

    w_r = jnp.concatenate([w_router_group[0], w_router_expert[0],
                           jnp.zeros((D_MODEL, LANES - N_EGROUPS - N_EXPERTS), F32)], axis=1).astype(BF16)
    b_r = jnp.concatenate([b_router_group[0], b_router_expert[0],
                           jnp.zeros((LANES - N_EGROUPS - N_EXPERTS,), F32)]).reshape(1, LANES)
    h2, e_sel, g_sel = _router(x2, norm2_g[0], w_r, b_r, tr)

    plan = _route_plan(e_sel[:t_real, :TOP_K], g_sel[:t_real, :TOP_K], t_real, tp)
    y2 = _experts(h2, plan, t_real, w1_e[0], w3_e[0], w2_e[0])

    def heads(a, lead):
        return a.reshape(lead + (N_HEADS, 2, HEAD_DIM))

    y_prompt = _combine(x2, y2, 0, seq, tr).reshape(1, seq, D_MODEL)
    y_sample = _combine(x2, y2, off_s, n_s, nq).reshape(nb, nq, D_MODEL)
    k_p = jnp.concatenate([kf[off_meta:off_meta + N_META], kf[:seq]], axis=0)
    v_p = jnp.concatenate([vf[off_meta:off_meta + N_META], vf[:seq]], axis=0)
    k_prompt = heads(k_p, (1, 1, seq + N_META))
    v_prompt = v_p.reshape(1, 1, seq + N_META, N_HEADS, V_DIM)
    k_sample = heads(kf[off_s:off_s + n_s], (1, nb, nq))
    v_sample = vf[off_s:off_s + n_s].reshape(1, nb, nq, N_HEADS, V_DIM)
    st = lambda a, lead: a.reshape(lead + (N_SSM_GROUPS, SSM_STATE))
    return (y_prompt, y_sample, k_prompt, v_prompt, st(hp_re, (1, 1)), st(hp_im, (1, 1)),
            k_sample, v_sample, st(hs_re, (1, nb)), st(hs_im, (1, nb)))
```

```python
import functools
import math

import jax
import jax.numpy as jnp
from jax import lax
from jax.experimental import pallas as pl
from jax.experimental.pallas import tpu as pltpu

F32 = jnp.float32
BF16 = jnp.bfloat16

D_MODEL = 4096
N_META = 16
CHUNK = 64
N_HEADS = 16
HEAD_DIM = 64
V_DIM = 128
D_ATT = N_HEADS * V_DIM
D_SSM = 2048
SSM_GROUP = 16
N_SSM_GROUPS = D_SSM // SSM_GROUP
SSM_STATE = 64
IN_WIDTH = D_SSM + 3 * D_ATT + 2 * D_MODEL
ROPE_THETA = 10000.0
N_EGROUPS = 8
EXPERTS_PER_GROUP = 8
N_EXPERTS = N_EGROUPS * EXPERTS_PER_GROUP
TOP_K = 2
D_EXPERT = 512
MOE_BLOCK = 320
EXPERT_SLICES = 4
EPS = 1e-6

LANES = 128
ROW_ALIGN = 512
S5_CHUNK = 8
S5_LANE_GROUPS = LANES // SSM_GROUP
S5_TILES = D_SSM // LANES
VMEM_LIMIT = 56 * 1024 * 1024


def _cparams(sem, vmem=VMEM_LIMIT):
    return pltpu.CompilerParams(dimension_semantics=sem, vmem_limit_bytes=vmem)


def _row_tile(tp, cap):
    best = 16
    for t in range(16, cap + 1, 16):
        if tp % t == 0:
            best = t
    return best


def _dot(a, b):
    return jnp.dot(a, b, preferred_element_type=F32)


def _dot_nt(a, b):
    return lax.dot_general(a, b, (((1,), (1,)), ((), ())), preferred_element_type=F32)


def _rmsnorm_kernel(x_ref, g_ref, o_ref):
    x = x_ref[...]
    ms = jnp.mean(x * x, axis=-1, keepdims=True)
    o_ref[...] = (x * lax.rsqrt(ms + EPS) * g_ref[...]).astype(o_ref.dtype)


def _rmsnorm(x, g, tr):
    tp, d = x.shape
    return pl.pallas_call(
        _rmsnorm_kernel,
        grid=(tp // tr,),
        in_specs=[pl.BlockSpec((tr, d), lambda i: (i, 0)),
                  pl.BlockSpec((1, d), lambda i: (0, 0))],
        out_specs=pl.BlockSpec((tr, d), lambda i: (i, 0)),
        out_shape=jax.ShapeDtypeStruct((tp, d), BF16),
        compiler_params=_cparams(("parallel",)),
        name="rmsnorm1",
    )(x, g.reshape(1, d))


def _inproj_kernel(x_ref, w_ref, b_ref, o_ref):
    o_ref[...] = _dot(x_ref[...], w_ref[...].astype(BF16)) + b_ref[...]


def _inproj(h, w, b, tm, tn):
    tp, k = h.shape
    n = w.shape[1]
    return pl.pallas_call(
        _inproj_kernel,
        grid=(n // tn, tp // tm),
        in_specs=[pl.BlockSpec((tm, k), lambda j, i: (i, 0)),
                  pl.BlockSpec((k, tn), lambda j, i: (0, j)),
                  pl.BlockSpec((1, tn), lambda j, i: (0, j))],
        out_specs=pl.BlockSpec((tm, tn), lambda j, i: (i, j)),
        out_shape=jax.ShapeDtypeStruct((tp, n), F32),
        compiler_params=_cparams(("parallel", "parallel")),
        name="in_proj",
    )(h, w, b.reshape(1, n))


def _glu_kernel(x_ref, w_ref, b_ref, xe_ref, o_ref):
    a = _dot(x_ref[...].astype(BF16), w_ref[...].astype(BF16)) + b_ref[...]
    o_ref[...] = (xe_ref[...] * jax.nn.sigmoid(a)).astype(o_ref.dtype)


def _glu(ys, w, b, tm, tn):
    tp, k = ys.shape
    n = w.shape[1]
    return pl.pallas_call(
        _glu_kernel,
        grid=(n // tn, tp // tm),
        in_specs=[pl.BlockSpec((tm, k), lambda j, i: (i, 0)),
                  pl.BlockSpec((k, tn), lambda j, i: (0, j)),
                  pl.BlockSpec((1, tn), lambda j, i: (0, j)),
                  pl.BlockSpec((tm, tn), lambda j, i: (i, j))],
        out_specs=pl.BlockSpec((tm, tn), lambda j, i: (i, j)),
        out_shape=jax.ShapeDtypeStruct((tp, n), BF16),
        compiler_params=_cparams(("parallel", "parallel")),
        name="glu",
    )(ys, w, b.reshape(1, n), ys)


def _merge_kernel(ys_ref, oa_ref, ws_ref, wa_ref, gs_ref, ga_ref, o_ref):
    a = _dot(ys_ref[...], ws_ref[...].astype(BF16))
    b = _dot(oa_ref[...], wa_ref[...].astype(BF16))
    m = jax.nn.sigmoid(gs_ref[...]) * a + jax.nn.sigmoid(ga_ref[...]) * b
    o_ref[...] = m.astype(o_ref.dtype)


def _merge(ysg, oatt, w_ssm, w_att, z, tm, tn):
    tp, k = ysg.shape
    n = w_ssm.shape[1]
    gs_blk = (D_SSM + 3 * D_ATT) // tn
    ga_blk = (D_SSM + 3 * D_ATT + D_MODEL) // tn
    return pl.pallas_call(
        _merge_kernel,
        grid=(n // tn, tp // tm),
        in_specs=[pl.BlockSpec((tm, k), lambda j, i: (i, 0)),
                  pl.BlockSpec((tm, k), lambda j, i: (i, 0)),
                  pl.BlockSpec((k, tn), lambda j, i: (0, j)),
                  pl.BlockSpec((k, tn), lambda j, i: (0, j)),
                  pl.BlockSpec((tm, tn), lambda j, i: (i, gs_blk + j)),
                  pl.BlockSpec((tm, tn), lambda j, i: (i, ga_blk + j))],
        out_specs=pl.BlockSpec((tm, tn), lambda j, i: (i, j)),
        out_shape=jax.ShapeDtypeStruct((tp, n), BF16),
        compiler_params=_cparams(("parallel", "parallel")),
        name="merge_proj",
    )(ysg, oatt, w_ssm, w_att, z, z)


def _outproj_kernel(m_ref, w_ref, x_ref, o_ref):
    o_ref[...] = x_ref[...] + _dot(m_ref[...], w_ref[...].astype(BF16))


def _outproj(m, w, x, tm, tn):
    tp, k = m.shape
    n = w.shape[1]
    return pl.pallas_call(
        _outproj_kernel,
        grid=(n // tn, tp // tm),
        in_specs=[pl.BlockSpec((tm, k), lambda j, i: (i, 0)),
                  pl.BlockSpec((k, tn), lambda j, i: (0, j)),
                  pl.BlockSpec((tm, tn), lambda j, i: (i, j))],
        out_specs=pl.BlockSpec((tm, tn), lambda j, i: (i, j)),
        out_shape=jax.ShapeDtypeStruct((tp, n), F32),
        compiler_params=_cparams(("parallel", "parallel")),
        name="out_proj",
    )(m, w, x)


def _segment_sumsq(x, ones_bd):
    x2 = x * x
    hi = x2.astype(BF16)
    lo = (x2 - hi.astype(F32)).astype(BF16)
    return _dot(hi, ones_bd) + _dot(lo, ones_bd)


def _qk_rope_kernel(zq_ref, zk_ref, zv_ref, cos_ref, sin_ref, gq_ref, gk_ref, ones_ref,
                    qb_ref, kf_ref, kb_ref, vf_ref, vb_ref):
    cos = cos_ref[...]
    sin = sin_ref[...]
    ones_bd = ones_ref[...]
    lane = lax.broadcasted_iota(jnp.int32, cos.shape, 1)
    first_half = (lane % HEAD_DIM) < (HEAD_DIM // 2)

    def norm_rope(x, g):
        ss = _segment_sumsq(x, ones_bd)
        xn = x * lax.rsqrt(ss * (1.0 / HEAD_DIM) + EPS) * g
        partner = jnp.where(first_half,
                            pltpu.roll(xn, LANES - HEAD_DIM // 2, 1),
                            pltpu.roll(xn, HEAD_DIM // 2, 1))
        return xn * cos + partner * sin

    for h in range(N_HEADS):
        sl = slice(h * LANES, (h + 1) * LANES)
        q = norm_rope(zq_ref[:, sl], gq_ref[...])
        qb_ref[:, sl] = (q * (HEAD_DIM ** -0.5)).astype(BF16)
        k = norm_rope(zk_ref[:, sl], gk_ref[...])
        kf_ref[:, sl] = k
        kb_ref[:, sl] = k.astype(BF16)
    v = zv_ref[...]
    vf_ref[...] = v
    vb_ref[...] = v.astype(BF16)


def _qk_rope(z, cos_t, sin_t, gq, gk, tr):
    tp = z.shape[0]
    ones_bd = jnp.kron(jnp.eye(LANES // HEAD_DIM, dtype=F32),
                       jnp.ones((HEAD_DIM, HEAD_DIM), F32)).astype(BF16)
    zspec = lambda c: pl.BlockSpec((tr, D_ATT), lambda i: (i, c))
    row = pl.BlockSpec((tr, LANES), lambda i: (i, 0))
    const = pl.BlockSpec((1, LANES), lambda i: (0, 0))
    out = pl.BlockSpec((tr, D_ATT), lambda i: (i, 0))
    q_blk = D_SSM // D_ATT
    return pl.pallas_call(
        _qk_rope_kernel,
        grid=(tp // tr,),
        in_specs=[zspec(q_blk), zspec(q_blk + 1), zspec(q_blk + 2), row, row, const, const,
                  pl.BlockSpec((LANES, LANES), lambda i: (0, 0))],
        out_specs=[out, out, out, out, out],
        out_shape=[jax.ShapeDtypeStruct((tp, D_ATT), BF16),
                   jax.ShapeDtypeStruct((tp, D_ATT), F32),
                   jax.ShapeDtypeStruct((tp, D_ATT), BF16),
                   jax.ShapeDtypeStruct((tp, D_ATT), F32),
                   jax.ShapeDtypeStruct((tp, D_ATT), BF16)],
        compiler_params=_cparams(("parallel",)),
        name="qk_norm_rope",
    )(z, z, z, cos_t, sin_t, gq, gk, ones_bd)


def _softmax_step(c, qc, kt, vt, mask, m_ref, l_ref, acc_ref):
    s = _dot_nt(qc, kt)
    if mask is not None:
        s = jnp.where(mask, s, -jnp.inf)
    m_prev = m_ref[c]
    m_new = jnp.maximum(m_prev, jnp.max(s, axis=1, keepdims=True))
    alpha = jnp.exp(m_prev - m_new)
    p = jnp.exp(s - m_new[:, :1])
    l_ref[c] = alpha * l_ref[c] + jnp.sum(p, axis=1, keepdims=True)
    acc_ref[c] = alpha * acc_ref[c] + _dot(p.astype(BF16), vt)
    m_ref[c] = m_new


def _diff_finish(o0, o1, lam, g, out_scale):
    o = o0 - lam * o1
    ms = jnp.mean(o * o, axis=-1, keepdims=True)
    return o * lax.rsqrt(ms + EPS) * g * out_scale


def _split_components(q):
    lane = lax.broadcasted_iota(jnp.int32, q.shape, 1)
    zero = jnp.zeros_like(q)
    return jnp.where(lane < HEAD_DIM, q, zero), jnp.where(lane >= HEAD_DIM, q, zero)


def _attn_prompt_kernel(lam_ref, g_ref, q_ref, k_ref, v_ref, o_ref, m_ref, l_ref, acc_ref,
                        *, tq, nq_main, seq, out_scale):
    i = pl.program_id(1)
    q0, q1 = _split_components(q_ref[...])
    m_ref[...] = jnp.full(m_ref.shape, -jnp.inf, F32)
    l_ref[...] = jnp.zeros(l_ref.shape, F32)
    acc_ref[...] = jnp.zeros(acc_ref.shape, F32)

    def update(kt, vt, mask):
        _softmax_step(0, q0, kt, vt, mask, m_ref, l_ref, acc_ref)
        _softmax_step(1, q1, kt, vt, mask, m_ref, l_ref, acc_ref)

    col = lax.broadcasted_iota(jnp.int32, (tq, LANES), 1)
    update(k_ref[pl.ds(seq, LANES), :], v_ref[pl.ds(seq, LANES), :], col < N_META)

    is_main = i < nq_main

    def body(j, carry):
        start = pl.multiple_of(j * tq, tq)
        update(k_ref[pl.ds(start, tq), :], v_ref[pl.ds(start, tq), :], None)
        return carry

    lax.fori_loop(0, jnp.where(is_main, i, 0), body, 0)

    @pl.when(is_main)
    def _():
        start = pl.multiple_of(i * tq, tq)
        r = lax.broadcasted_iota(jnp.int32, (tq, tq), 0) // CHUNK
        c = lax.broadcasted_iota(jnp.int32, (tq, tq), 1) // CHUNK
        update(k_ref[pl.ds(start, tq), :], v_ref[pl.ds(start, tq), :], c <= r)

    o0 = acc_ref[0] / l_ref[0]
    o1 = acc_ref[1] / l_ref[1]
    o_ref[...] = _diff_finish(o0, o1, lam_ref[...], g_ref[...], out_scale).astype(o_ref.dtype)


def _attn_prompt(qb, kb, vb, lam_row, subln_g, seq, out_scale, tq):
    tp = qb.shape[0]
    kern = functools.partial(_attn_prompt_kernel, tq=tq, nq_main=seq // tq, seq=seq,
                             out_scale=out_scale)
    const = pl.BlockSpec((1, LANES), lambda h, i: (0, 0))
    return pl.pallas_call(
        kern,
        grid=(N_HEADS, tp // tq),
        in_specs=[const, const,
                  pl.BlockSpec((tq, LANES), lambda h, i: (i, h)),
                  pl.BlockSpec((tp, LANES), lambda h, i: (0, h)),
                  pl.BlockSpec((tp, LANES), lambda h, i: (0, h))],
        out_specs=pl.BlockSpec((tq, LANES), lambda h, i: (i, h)),
        out_shape=jax.ShapeDtypeStruct((tp, D_ATT), BF16),
        scratch_shapes=[pltpu.VMEM((2, tq, LANES), F32),
                        pltpu.VMEM((2, tq, LANES), F32),
                        pltpu.VMEM((2, tq, LANES), F32)],
        compiler_params=_cparams(("parallel", "parallel")),
        name="attn_prompt",
    )(lam_row, subln_g, qb, kb, vb)


LOGIT_BOUND_MAX = 40.0
ATTN_WIDE = 4


def _with_ones(vt):
    return jnp.concatenate([vt, jnp.ones(vt.shape, vt.dtype)], axis=1)


def _cache_v_head(vc_ref, h):
    tk = vc_ref.shape[0] // N_HEADS
    return vc_ref[pl.ds(h, tk, stride=N_HEADS), :].astype(BF16)


def _attn_prompt_fast_kernel(lam_ref, g_ref, q_ref, k_ref, v_ref, o_ref, acc_ref,
                             *, tq, nq_main, seq, out_scale):
    i = pl.program_id(1)
    q0, q1 = _split_components(q_ref[...])
    qq = jnp.concatenate([q0, q1], axis=0)

    def scores(start, rows):
        return _dot_nt(qq, k_ref[pl.ds(start, rows), :])

    def weighted(s, start, rows, mask):
        p = jnp.exp(s)
        if mask is not None:
            p = jnp.where(mask, p, 0.0)
        return _dot(p.astype(BF16), _with_ones(v_ref[pl.ds(start, rows), :]))

    def tile_pv(start, rows, mask):
        return weighted(scores(start, rows), start, rows, mask)

    col = lax.broadcasted_iota(jnp.int32, (2 * tq, LANES), 1)
    acc_ref[...] = tile_pv(seq, LANES, col < N_META)

    is_main = i < nq_main
    n_full = jnp.where(is_main, i, 0)

    wide = ATTN_WIDE

    def body(j, carry):
        start = pl.multiple_of(wide * j * tq, tq)
        acc_ref[...] += tile_pv(start, wide * tq, None)
        return carry

    n_wide = n_full // wide
    lax.fori_loop(0, n_wide, body, 0)
    rem = n_full - wide * n_wide
    base = wide * n_wide

    @pl.when(rem >= 2)
    def _():
        acc_ref[...] += tile_pv(pl.multiple_of(base * tq, tq), 2 * tq, None)

    @pl.when(rem % 2 == 1)
    def _():
        acc_ref[...] += tile_pv(pl.multiple_of((n_full - 1) * tq, tq), tq, None)

    @pl.when(is_main)
    def _():
        r = (lax.broadcasted_iota(jnp.int32, (2 * tq, tq), 0) % tq) // CHUNK
        c = lax.broadcasted_iota(jnp.int32, (2 * tq, tq), 1) // CHUNK
        acc_ref[...] += tile_pv(pl.multiple_of(i * tq, tq), tq, c <= r)

    acc = acc_ref[...]
    o0 = acc[:tq, :LANES] / acc[:tq, LANES:]
    o1 = acc[tq:, :LANES] / acc[tq:, LANES:]
    o_ref[...] = _diff_finish(o0, o1, lam_ref[...], g_ref[...], out_scale).astype(o_ref.dtype)


def _attn_prompt_fast(qb, kb, vb, lam_row, subln_g, seq, out_scale, tq):
    tp = qb.shape[0]
    kern = functools.partial(_attn_prompt_fast_kernel, tq=tq, nq_main=seq // tq, seq=seq,
                             out_scale=out_scale)
    const = pl.BlockSpec((1, LANES), lambda h, i: (0, 0))
    return pl.pallas_call(
        kern,
        grid=(N_HEADS, tp // tq),
        in_specs=[const, const,
                  pl.BlockSpec((tq, LANES), lambda h, i: (i, h)),
                  pl.BlockSpec((tp, LANES), lambda h, i: (0, h)),
                  pl.BlockSpec((tp, LANES), lambda h, i: (0, h))],
        out_specs=pl.BlockSpec((tq, LANES), lambda h, i: (i, h)),
        out_shape=jax.ShapeDtypeStruct((tp, D_ATT), BF16),
        scratch_shapes=[pltpu.VMEM((2 * tq, 2 * LANES), F32)],
        compiler_params=_cparams(("parallel", "parallel")),
        name="attn_prompt_fast",
    )(lam_row, subln_g, qb, kb, vb)


def _attn_sample_fast_kernel(lam_ref, g_ref, q_ref, kc_ref, vc_ref, kn_ref, vn_ref, o_in_ref, o_ref,
                             acc_ref, *, nq, out_scale):
    del o_in_ref
    j = pl.program_id(1)
    rows = 2 * nq

    def head_update(h, kt, vt, mask):
        p = jnp.exp(_dot_nt(q_ref[h], kt))
        if mask is not None:
            p = jnp.where(mask, p, 0.0)
        acc_ref[h] += _dot(p.astype(BF16), _with_ones(vt))

    @pl.when(j == 0)
    def _():
        acc_ref[...] = jnp.zeros(acc_ref.shape, F32)
        col = lax.broadcasted_iota(jnp.int32, (rows, LANES), 1)
        for h in range(N_HEADS):
            sl = slice(h * LANES, (h + 1) * LANES)
            head_update(h, kn_ref[:, sl], vn_ref[:, sl], col < nq)

    for h in range(N_HEADS):
        sl = slice(h * LANES, (h + 1) * LANES)
        head_update(h, kc_ref[:, sl].astype(BF16), _cache_v_head(vc_ref, h), None)

    @pl.when(j == pl.num_programs(1) - 1)
    def _():
        for h in range(N_HEADS):
            acc = acc_ref[h]
            o = acc[:, :LANES] / acc[:, LANES:]
            res = _diff_finish(o[:nq], o[nq:], lam_ref[...], g_ref[...], out_scale)
            o_ref[:, h * LANES:(h + 1) * LANES] = res.astype(o_ref.dtype)


def _attn_sample_kernel(lam_ref, g_ref, q_ref, kc_ref, vc_ref, kn_ref, vn_ref, o_in_ref, o_ref,
                        m_ref, l_ref, acc_ref, *, nq, out_scale):
    del o_in_ref
    j = pl.program_id(1)
    rows = 2 * nq

    @pl.when(j == 0)
    def _():
        m_ref[...] = jnp.full(m_ref.shape, -jnp.inf, F32)
        l_ref[...] = jnp.zeros(l_ref.shape, F32)
        acc_ref[...] = jnp.zeros(acc_ref.shape, F32)
        col = lax.broadcasted_iota(jnp.int32, (rows, LANES), 1)
        for h in range(N_HEADS):
            sl = slice(h * LANES, (h + 1) * LANES)
            _softmax_step(h, q_ref[h], kn_ref[:, sl], vn_ref[:, sl], col < nq,
                          m_ref, l_ref, acc_ref)

    for h in range(N_HEADS):
        sl = slice(h * LANES, (h + 1) * LANES)
        _softmax_step(h, q_ref[h], kc_ref[:, sl].astype(BF16), _cache_v_head(vc_ref, h), None,
                      m_ref, l_ref, acc_ref)

    @pl.when(j == pl.num_programs(1) - 1)
    def _():
        for h in range(N_HEADS):
            o = acc_ref[h] / l_ref[h]
            res = _diff_finish(o[:nq], o[nq:], lam_ref[...], g_ref[...], out_scale)
            o_ref[:, h * LANES:(h + 1) * LANES] = res.astype(o_ref.dtype)


def _attn_sample(qz, cache_k, cache_v, k_new, v_new, o_buf, lam_row, subln_g, row_off, out_scale, tk, fast):
    nb, past = cache_k.shape[0], cache_k.shape[1]
    nq = qz.shape[2] // 2
    if fast:
        kern = functools.partial(_attn_sample_fast_kernel, nq=nq, out_scale=out_scale)
        scratch = [pltpu.VMEM((N_HEADS, 2 * nq, 2 * LANES), F32)]
    else:
        kern = functools.partial(_attn_sample_kernel, nq=nq, out_scale=out_scale)
        scratch = [pltpu.VMEM((N_HEADS, 2 * nq, LANES), F32)] * 3
    const = pl.BlockSpec((1, LANES), lambda b, j: (0, 0))
    blk_off = row_off // nq
    return pl.pallas_call(
        kern,
        grid=(nb, past // tk),
        in_specs=[const, const,
                  pl.BlockSpec((None, N_HEADS, 2 * nq, LANES), lambda b, j: (b, 0, 0, 0)),
                  pl.BlockSpec((None, tk, D_ATT), lambda b, j: (b, j, 0)),
                  pl.BlockSpec((None, tk * N_HEADS, V_DIM), lambda b, j: (b, j, 0)),
                  pl.BlockSpec((None, LANES, D_ATT), lambda b, j: (b, 0, 0)),
                  pl.BlockSpec((None, LANES, D_ATT), lambda b, j: (b, 0, 0)),
                  pl.BlockSpec(memory_space=pl.ANY)],
        out_specs=pl.BlockSpec((nq, D_ATT), lambda b, j: (blk_off + b, 0)),
        out_shape=jax.ShapeDtypeStruct(o_buf.shape, o_buf.dtype),
        scratch_shapes=scratch,
        input_output_aliases={7: 0},
        compiler_params=_cparams(("parallel", "arbitrary")),
        name="attn_sample_fast" if fast else "attn_sample",
    )(lam_row, subln_g, qz, cache_k, cache_v, k_new, v_new, o_buf)


def _s5_weights(a_re, a_im, log_dt, b_re, b_im, c_re, c_im):
    hp = lax.Precision.HIGHEST
    n_t, gl, tc = S5_TILES, S5_LANE_GROUPS, S5_CHUNK
    dt = jnp.exp(log_dt)[:, None]
    mag = jnp.exp(a_re * dt)
    abar_re = mag * jnp.cos(a_im * dt)
    abar_im = mag * jnp.sin(a_im * dt)
    nr, ni = abar_re - 1.0, abar_im
    den = a_re * a_re + a_im * a_im
    coef_re = (nr * a_re + ni * a_im) / den
    coef_im = (ni * a_re - nr * a_im) / den
    bbar_re = coef_re[..., None] * b_re - coef_im[..., None] * b_im
    bbar_im = coef_re[..., None] * b_im + coef_im[..., None] * b_re
    n = jnp.arange(tc + 1, dtype=F32)[:, None, None]
    pw_mag = jnp.exp(n * (a_re * dt))
    pw_re = pw_mag * jnp.cos(n * (a_im * dt))
    pw_im = pw_mag * jnp.sin(n * (a_im * dt))
    e_re = pw_re[:tc, :, :, None] * bbar_re - pw_im[:tc, :, :, None] * bbar_im
    e_im = pw_re[:tc, :, :, None] * bbar_im + pw_im[:tc, :, :, None] * bbar_re
    kern = (jnp.einsum('gcp,lgpd->glcd', c_re, e_re, precision=hp)
            - jnp.einsum('gcp,lgpd->glcd', c_im, e_im, precision=hp))
    eye = jnp.eye(gl, dtype=F32)
    w_intra = jnp.einsum('jglcd,gh->jlgdhc', kern.reshape(n_t, gl, tc, SSM_GROUP, SSM_GROUP), eye)
    w_intra = w_intra.reshape(n_t, tc, LANES, LANES)
    eb = jnp.stack([e_re[::-1], e_im[::-1]], 0)
    eb = eb.reshape(2, tc, n_t, gl, SSM_STATE, SSM_GROUP)
    w_state = eb.transpose(2, 1, 3, 5, 0, 4).reshape(n_t, tc, LANES, 2 * SSM_STATE)
    cp_re = c_re[None] * pw_re[1:, :, None, :] - c_im[None] * pw_im[1:, :, None, :]
    cp_im = c_re[None] * pw_im[1:, :, None, :] + c_im[None] * pw_re[1:, :, None, :]
    cp = jnp.stack([cp_re, -cp_im], 0).reshape(2, tc, n_t, gl, SSM_GROUP, SSM_STATE)
    w_read = cp.transpose(2, 1, 0, 5, 3, 4).reshape(n_t, tc, 2 * SSM_STATE, LANES)
    half = gl * SSM_STATE
    a_pow = jnp.concatenate([pw_re[tc].reshape(n_t, 1, half), pw_im[tc].reshape(n_t, 1, half)], -1)
    rp = jnp.arange(2 * SSM_STATE)
    col = jnp.arange(2 * half)
    spread = ((rp[:, None] // SSM_STATE == col[None, :] // half)
              & (rp[:, None] % SSM_STATE == col[None, :] % SSM_STATE)).astype(BF16)
    return (w_intra.astype(BF16), w_state.astype(BF16), w_read.astype(BF16), a_pow, spread, spread.T)


def _s5_kernel(u_ref, wi_ref, wsc_ref, wrc_ref, ap_ref, sp_ref, spt_ref, d_ref, h0r_ref, h0i_ref,
               y_ref, hpr_ref, hpi_ref, hsr_ref, hsi_ref,
               y_acc, v_ref, hs_ref, wt_ref, ws_ref, wr_ref,
               *, nc, n_main, n_meta_chunks, n_seq, seq_chunks):
    tc = S5_CHUNK
    half = hs_ref.shape[1] // 2
    grp_r = lax.broadcasted_iota(jnp.int32, (LANES, 2 * half), 0) // SSM_GROUP
    grp_c = (lax.broadcasted_iota(jnp.int32, (LANES, 2 * half), 1) % half) // SSM_STATE
    for s in range(tc):
        full = _dot(wsc_ref[s], sp_ref[...])
        ws_ref[s * LANES:(s + 1) * LANES, :] = jnp.where(grp_r == grp_c, full, 0.0).astype(BF16)
    grp_r = (lax.broadcasted_iota(jnp.int32, (2 * half, LANES), 0) % half) // SSM_STATE
    grp_c = lax.broadcasted_iota(jnp.int32, (2 * half, LANES), 1) // SSM_GROUP
    for t in range(tc):
        full = _dot(spt_ref[...], wrc_ref[t])
        wr_ref[:, t * LANES:(t + 1) * LANES] = jnp.where(grp_r == grp_c, full, 0.0).astype(BF16)
    for s in range(tc):
        for t in range(tc):
            blk = wi_ref[t - s] if t >= s else jnp.zeros((LANES, LANES), BF16)
            wt_ref[s * LANES:(s + 1) * LANES, t * LANES:(t + 1) * LANES] = blk
    lhs = jnp.concatenate(
        [u_ref[pl.ds(s, nc, stride=tc), :].astype(BF16) for s in range(tc)], axis=1)
    y_acc[...] = _dot(lhs, wt_ref[...])
    v_ref[...] = _dot(lhs, ws_ref[...])
    a_re = ap_ref[:, :half]
    a_im = ap_ref[:, half:]

    def advance(h_re, h_im, v):
        return (a_re * h_re - a_im * h_im + v[:, :half],
                a_re * h_im + a_im * h_re + v[:, half:])

    hs_ref[...] = jnp.zeros(hs_ref.shape, F32)

    h_re = jnp.zeros((1, half), F32)
    h_im = jnp.zeros((1, half), F32)
    for c in range(n_main, n_main + n_meta_chunks):
        hs_ref[pl.ds(c, 1), :] = jnp.concatenate([h_re, h_im], axis=1)
        h_re, h_im = advance(h_re, h_im, v_ref[pl.ds(c, 1), :])

    def body(c, carry):
        h_re, h_im = carry
        hs_ref[pl.ds(c, 1), :] = jnp.concatenate([h_re, h_im], axis=1)
        return advance(h_re, h_im, v_ref[pl.ds(c, 1), :])

    h_re, h_im = lax.fori_loop(0, n_main, body, (h_re, h_im))
    hpr_ref[...] = h_re
    hpi_ref[...] = h_im

    base = n_main + n_meta_chunks
    for b in range(n_seq):
        s_re = h0r_ref[pl.ds(b, 1), :]
        s_im = h0i_ref[pl.ds(b, 1), :]
        for c in range(base + b * seq_chunks, base + (b + 1) * seq_chunks):
            hs_ref[pl.ds(c, 1), :] = jnp.concatenate([s_re, s_im], axis=1)
            s_re, s_im = advance(s_re, s_im, v_ref[pl.ds(c, 1), :])
        hsr_ref[pl.ds(b, 1), :] = s_re
        hsi_ref[pl.ds(b, 1), :] = s_im

    y_acc[...] += _dot(hs_ref[...].astype(BF16), wr_ref[...])
    d = d_ref[...]
    for t in range(tc):
        rows = pl.ds(t, nc, stride=tc)
        y = y_acc[:, t * LANES:(t + 1) * LANES] + d * u_ref[rows, :]
        y_ref[rows, :] = jax.nn.gelu(y)


def _s5(z, weights, d_skip, h0_re, h0_im, seq, n_seq, seq_len):
    tp = z.shape[0]
    tc = S5_CHUNK
    nc = tp // tc
    w_intra, w_state, w_read, a_pow, spread, spread_t = weights
    half = S5_LANE_GROUPS * SSM_STATE
    whole = lambda a: pl.BlockSpec(a.shape, lambda j: (0,) * a.ndim)
    kern = functools.partial(_s5_kernel, nc=nc, n_main=seq // tc, n_meta_chunks=N_META // tc,
                             n_seq=n_seq, seq_chunks=seq_len // tc)
    wspec = lambda a: pl.BlockSpec((None,) + a.shape[1:], lambda j: (j,) + (0,) * (a.ndim - 1))
    col = pl.BlockSpec((tp, LANES), lambda j: (0, j))
    st = lambda r: pl.BlockSpec((r, half), lambda j: (0, j))
    gp = N_SSM_GROUPS * SSM_STATE
    return pl.pallas_call(
        kern,
        grid=(S5_TILES,),
        in_specs=[col, wspec(w_intra), wspec(w_state), wspec(w_read), wspec(a_pow),
                  whole(spread), whole(spread_t),
                  pl.BlockSpec((1, LANES), lambda j: (0, j)), st(n_seq), st(n_seq)],
        out_specs=[col, st(1), st(1), st(n_seq), st(n_seq)],
        out_shape=[jax.ShapeDtypeStruct((tp, D_SSM), F32),
                   jax.ShapeDtypeStruct((1, gp), F32),
                   jax.ShapeDtypeStruct((1, gp), F32),
                   jax.ShapeDtypeStruct((n_seq, gp), F32),
                   jax.ShapeDtypeStruct((n_seq, gp), F32)],
        scratch_shapes=[pltpu.VMEM((nc, tc * LANES), F32),
                        pltpu.VMEM((nc, 2 * half), F32),
                        pltpu.VMEM((nc, 2 * half), F32),
                        pltpu.VMEM((tc * LANES, tc * LANES), BF16),
                        pltpu.VMEM((tc * LANES, 2 * half), BF16),
                        pltpu.VMEM((2 * half, tc * LANES), BF16)],
        compiler_params=_cparams(("parallel",)),
        name="s5_scan",
    )(z, w_intra, w_state, w_read, a_pow, spread, spread_t, d_skip.reshape(1, D_SSM), h0_re, h0_im)


ROW_CHUNKS = D_MODEL // LANES


def _store_token_major(ref, x):
    n = x.shape[0]
    for c in range(ROW_CHUNKS):
        ref[pl.ds(c, n, stride=ROW_CHUNKS), :] = x[:, c * LANES:(c + 1) * LANES].astype(ref.dtype)


def _load_token_major(ref, n, dtype):
    return jnp.concatenate([ref[pl.ds(c, n, stride=ROW_CHUNKS), :].astype(dtype)
                            for c in range(ROW_CHUNKS)], axis=1)


def _router_kernel(x_ref, g_ref, w_ref, b_ref, h_ref, e_ref, gate_ref):
    x = x_ref[...]
    ms = jnp.mean(x * x, axis=-1, keepdims=True)
    h = x * lax.rsqrt(ms + EPS) * g_ref[...]
    _store_token_major(h_ref, h)
    logits = _dot(h.astype(BF16), w_ref[...]) + b_ref[...]
    lane = lax.broadcasted_iota(jnp.int32, logits.shape, 1)
    neg = -jnp.inf
    big = jnp.int32(LANES)

    def first_argmax(vals, vmax):
        return jnp.min(jnp.where(vals == vmax, lane, big), axis=1, keepdims=True)

    lg = jnp.where(lane < N_EGROUPS, logits, neg)
    mg = jnp.max(lg, axis=1, keepdims=True)
    sg = jnp.sum(jnp.exp(lg - mg), axis=1, keepdims=True)
    g_w = 1.0 / sg
    g_idx = first_argmax(lg, mg)
    lo = N_EGROUPS + EXPERTS_PER_GROUP * g_idx
    le = jnp.where((lane >= lo) & (lane < lo + EXPERTS_PER_GROUP), logits, neg)
    m1 = jnp.max(le, axis=1, keepdims=True)
    se = jnp.sum(jnp.exp(le - m1), axis=1, keepdims=True)
    i1 = first_argmax(le, m1)
    le2 = jnp.where(lane == i1, neg, le)
    m2 = jnp.max(le2, axis=1, keepdims=True)
    i2 = first_argmax(le2, m2)
    p1 = 1.0 / se
    p2 = jnp.exp(m2 - m1) / se
    tot = p1 + p2
    w1 = g_w * (p1 / tot)
    w2 = g_w * (p2 / tot)
    e_ref[...] = jnp.where(lane == 0, i1 - N_EGROUPS, jnp.where(lane == 1, i2 - N_EGROUPS, 0))
    gate_ref[...] = jnp.where(lane == 0, w1, jnp.where(lane == 1, w2, 0.0))


def _router(x2, g, w_r, b_r, tr):
    tp, d = x2.shape
    return pl.pallas_call(
        _router_kernel,
        grid=(tp // tr,),
        in_specs=[pl.BlockSpec((tr, d), lambda i: (i, 0)),
                  pl.BlockSpec((1, d), lambda i: (0, 0)),
                  pl.BlockSpec((d, LANES), lambda i: (0, 0)),
                  pl.BlockSpec((1, LANES), lambda i: (0, 0))],
        out_specs=[pl.BlockSpec((tr * ROW_CHUNKS, LANES), lambda i: (i, 0)),
                   pl.BlockSpec((tr, LANES), lambda i: (i, 0)),
                   pl.BlockSpec((tr, LANES), lambda i: (i, 0))],
        out_shape=[jax.ShapeDtypeStruct((tp * ROW_CHUNKS, LANES), F32),
                   jax.ShapeDtypeStruct((tp, LANES), jnp.int32),
                   jax.ShapeDtypeStruct((tp, LANES), F32)],
        compiler_params=_cparams(("parallel",)),
        name="norm2_router",
    )(x2, g.reshape(1, d), w_r, b_r)


def _expert_kernel(be_ref, nu_ref, first_ref, cnt_ref, tok_ref, dst_ref, h_hbm, w1_ref, w3_ref, w2_ref,
                   y_hbm, xbuf, xb16, hid_ref, ybuf, gsem, ssem, *, nb, plane_rows, plane_pad):
    del be_ref
    b = pl.program_id(0)
    hh = pl.program_id(1)
    last = pl.num_programs(1) - 1
    n_used = nu_ref[0]
    active = b < n_used
    slot = b % 2
    rows = xb16.shape[0]
    rc = ROW_CHUNKS
    spare_row = TOP_K * plane_rows

    def token_rows(ref, t):
        return ref.at[pl.ds(pl.multiple_of(t * rc, rc), rc), :]

    def start_gather(blk, s):
        base = first_ref[blk]
        for r in range(rows):
            pltpu.make_async_copy(token_rows(h_hbm, tok_ref[base + r]),
                                  xbuf.at[s, pl.ds(r * rc, rc), :], gsem.at[s]).start()

    def all_rows_gathered(s):
        return pltpu.make_async_copy(h_hbm.at[pl.ds(0, rows * rc), :], xbuf.at[s], gsem.at[s])

    def all_rows_scattered():
        return pltpu.make_async_copy(ybuf, y_hbm.at[pl.ds(spare_row * rc, rows * rc), :], ssem)

    @pl.when((b == 0) & (hh == 0))
    def _():
        ybuf[...] = jnp.zeros(ybuf.shape, F32)
        fills = [all_rows_scattered()]
        if plane_pad:
            fills += [pltpu.make_async_copy(
                ybuf.at[pl.ds(0, plane_pad * rc), :],
                y_hbm.at[pl.ds(((k + 1) * plane_rows - plane_pad) * rc, plane_pad * rc), :], ssem)
                for k in range(TOP_K)]
        for f in fills:
            f.start()
        for f in fills:
            f.wait()

        @pl.when(active)
        def _():
            start_gather(0, 0)

    @pl.when(active & (hh == 0))
    def _():
        all_rows_gathered(slot).wait()
        xb16[...] = _load_token_major(xbuf.at[slot], rows, BF16)

    @pl.when(active & (hh == 1) & (b + 1 < n_used))
    def _():
        start_gather(jnp.minimum(b + 1, nb - 1), 1 - slot)

    @pl.when(active)
    def _():
        dh = w1_ref.shape[1]
        w13 = jnp.concatenate([w1_ref[...].astype(BF16), w3_ref[...].astype(BF16)], axis=1)
        ac = _dot(xb16[...], w13)
        hid_ref[_expert_slice(b, hh)] = (jax.nn.silu(ac[:, :dh]) * ac[:, dh:]).astype(BF16)

    @pl.when(active & (hh == last))
    def _():
        @pl.when(b > 0)
        def _():
            all_rows_scattered().wait()

        hid = jnp.concatenate([hid_ref[q] for q in range(EXPERT_SLICES)], axis=1)
        _store_token_major(ybuf, _dot(hid, w2_ref[...].astype(BF16)))
        base = first_ref[b]
        n_real = cnt_ref[b]
        for r in range(rows):
            dst = jnp.where(r < n_real, dst_ref[base + r], spare_row + r)
            pltpu.make_async_copy(ybuf.at[pl.ds(r * rc, rc), :], token_rows(y_hbm, dst), ssem).start()

        @pl.when(b + 1 >= n_used)
        def _():
            all_rows_scattered().wait()


def _expert_slice(b, hh):
    return jnp.where(b % 2 == 0, hh, EXPERT_SLICES - 1 - hh)


def _experts(h2, plan, t_real, w1, w3, w2):
    tp = h2.shape[0] // ROW_CHUNKS
    assert tp - t_real <= MOE_BLOCK
    block_expert, n_used, first, cnt, tok_sorted, dst_sorted = plan
    d = D_MODEL
    nb = block_expert.shape[0]
    dh = D_EXPERT // EXPERT_SLICES

    def eidx(b, be, nu):
        return be[jnp.minimum(b, nu[0] - 1)]

    def sidx(b, hh, nu):
        live = b < nu[0]
        return _expert_slice(jnp.minimum(b, nu[0] - 1), jnp.where(live, hh, EXPERT_SLICES - 1))

    grid_spec = pltpu.PrefetchScalarGridSpec(
        num_scalar_prefetch=6,
        grid=(nb, EXPERT_SLICES),
        in_specs=[pl.BlockSpec(memory_space=pl.ANY),
                  pl.BlockSpec((None, d, dh), lambda b, hh, be, nu, *_: (eidx(b, be, nu), 0, sidx(b, hh, nu))),
                  pl.BlockSpec((None, d, dh), lambda b, hh, be, nu, *_: (eidx(b, be, nu), 0, sidx(b, hh, nu))),
                  pl.BlockSpec((None, D_EXPERT, d), lambda b, hh, be, nu, *_: (eidx(b, be, nu), 0, 0))],
        out_specs=pl.BlockSpec(memory_space=pl.ANY),
        scratch_shapes=[pltpu.VMEM((2, MOE_BLOCK * ROW_CHUNKS, LANES), F32),
                        pltpu.VMEM((MOE_BLOCK, d), BF16),
                        pltpu.VMEM((EXPERT_SLICES, MOE_BLOCK, dh), BF16),
                        pltpu.VMEM((MOE_BLOCK * ROW_CHUNKS, LANES), F32),
                        pltpu.SemaphoreType.DMA((2,)),
                        pltpu.SemaphoreType.DMA(())],
    )
    return pl.pallas_call(
        functools.partial(_expert_kernel, nb=nb, plane_rows=tp, plane_pad=tp - t_real),
        grid_spec=grid_spec,
        out_shape=jax.ShapeDtypeStruct(((TOP_K * tp + MOE_BLOCK) * ROW_CHUNKS, LANES), F32),
        compiler_params=_cparams(("arbitrary", "arbitrary"), 60 * 1024 * 1024),
        name="expert_mlp",
    )(block_expert, n_used, first, cnt, tok_sorted, dst_sorted, h2, w1, w3, w2)


def _route_plan(expert, t_real, tp):
    s = t_real * TOP_K
    n_blocks = -(-(s + N_EXPERTS * (MOE_BLOCK - 1)) // MOE_BLOCK)
    flat_e = expert.reshape(-1).astype(jnp.int32)
    se, order = lax.sort((flat_e, jnp.arange(s, dtype=jnp.int32)), num_keys=1, is_stable=True)
    bounds = jnp.searchsorted(se, jnp.arange(N_EXPERTS + 1, dtype=jnp.int32)).astype(jnp.int32)
    start = bounds[:-1]
    counts = bounds[1:] - start
    padded = (counts + MOE_BLOCK - 1) // MOE_BLOCK * MOE_BLOCK
    pad_end = jnp.cumsum(padded)
    pad_start = pad_end - padded
    block_start = jnp.arange(n_blocks, dtype=jnp.int32) * MOE_BLOCK
    block_expert = jnp.minimum(jnp.searchsorted(pad_end, block_start, side='right'),
                               N_EXPERTS - 1).astype(jnp.int32)
    n_used = (pad_end[-1] // MOE_BLOCK).astype(jnp.int32).reshape(1)
    cnt = jnp.clip(counts[block_expert] - (block_start - pad_start[block_expert]), 0, MOE_BLOCK)
    first = jnp.clip(start[block_expert] + block_start - pad_start[block_expert], 0, s).astype(jnp.int32)
    tail = jnp.zeros((MOE_BLOCK,), jnp.int32)
    tok = order // TOP_K
    tok_sorted = jnp.concatenate([tok, tail])
    dst_sorted = jnp.concatenate([(order % TOP_K) * tp + tok, tail])
    return block_expert, n_used, first, cnt.astype(jnp.int32), tok_sorted, dst_sorted


def _combine_kernel(x_ref, g_ref, y0_ref, y1_ref, o_ref):
    n = x_ref.shape[0]
    g = g_ref[...]
    y = (g[:, 0:1] * _load_token_major(y0_ref, n, F32)
         + g[:, 1:2] * _load_token_major(y1_ref, n, F32))
    o_ref[...] = x_ref[...] + y


def _combine(x2, gates, y2, row_off, n_rows, tile):
    tp, d = x2.shape
    off = row_off // tile
    plane = tp // tile
    return pl.pallas_call(
        _combine_kernel,
        grid=(n_rows // tile,),
        in_specs=[pl.BlockSpec((tile, d), lambda i: (off + i, 0)),
                  pl.BlockSpec((tile, LANES), lambda i: (off + i, 0)),
                  pl.BlockSpec((tile * ROW_CHUNKS, LANES), lambda i: (off + i, 0)),
                  pl.BlockSpec((tile * ROW_CHUNKS, LANES), lambda i: (plane + off + i, 0))],
        out_specs=pl.BlockSpec((tile, d), lambda i: (i, 0)),
        out_shape=jax.ShapeDtypeStruct((n_rows, d), F32),
        compiler_params=_cparams(("parallel",)),
        name="moe_combine",
    )(x2, gates, y2, y2)


def kernel(x_prompt, x_sample, cache_k, cache_v, state_ssm_re, state_ssm_im, meta_tokens, norm1_g, w_in, b_in, ssm_a_re, ssm_a_im, ssm_log_dt, ssm_b_re, ssm_b_im, ssm_c_re, ssm_c_im, ssm_d, w_glu, b_glu, w_ssm_proj, q_norm_g, k_norm_g, lam_q1, lam_k1, lam_q2, lam_k2, subln_g, w_att_proj, w_o, norm2_g, w_router_group, b_router_group, w_router_expert, b_router_expert, w1_e, w3_e, w2_e):
    assert x_prompt.shape[0] == 1 and w_in.shape[0] == 1
    seq = x_prompt.shape[1]
    nb, nq = x_sample.shape[0], x_sample.shape[1]
    past = cache_k.shape[2]
    n_s = nb * nq
    t_real = seq + N_META + n_s
    tp = -(-t_real // ROW_ALIGN) * ROW_ALIGN
    off_meta, off_s = seq, seq + N_META
    tq = 256
    assert seq % tq == 0 and nq == N_META and past % 512 == 0
    lam_init = 0.8 - 0.6 * math.exp(-0.3 * 0)
    out_scale = 1.0 - lam_init

    x_cat = jnp.concatenate([x_prompt[0], meta_tokens.astype(F32), x_sample.reshape(n_s, D_MODEL),
                             jnp.zeros((tp - t_real, D_MODEL), F32)], axis=0)

    tm = _row_tile(tp, 1088)
    tr = _row_tile(tp, 256)

    h1 = _rmsnorm(x_cat, norm1_g[0], tr)
    z = _inproj(h1, w_in[0], b_in[0], tm, 512)

    pos = jnp.concatenate([N_META + jnp.arange(seq), jnp.arange(N_META),
                           jnp.tile(past + jnp.arange(nq), nb),
                           jnp.zeros((tp - t_real,), jnp.int32)]).astype(F32)
    half = HEAD_DIM // 2
    inv = ROPE_THETA ** (-jnp.arange(half, dtype=F32) / half)
    ang = pos[:, None] * inv[None, :]
    cos_t = jnp.tile(jnp.cos(ang), (1, LANES // half))
    sin_h = jnp.sin(ang)
    sin_t = jnp.tile(jnp.concatenate([-sin_h, sin_h], axis=1), (1, LANES // HEAD_DIM))
    gq = jnp.tile(q_norm_g[0], LANES // HEAD_DIM).reshape(1, LANES)
    gk = jnp.tile(k_norm_g[0], LANES // HEAD_DIM).reshape(1, LANES)
    qb, kf, kb, vf, vb = _qk_rope(z, cos_t, sin_t, gq, gk, tr)

    lam = (jnp.exp(jnp.sum(lam_q1[0] * lam_k1[0])) - jnp.exp(jnp.sum(lam_q2[0] * lam_k2[0])) + lam_init)
    lam_row = jnp.full((1, LANES), lam, F32)
    sg = subln_g[0].reshape(1, LANES)

    q_s = qb[off_s:off_s + n_s].reshape(nb, nq, N_HEADS, LANES).transpose(0, 2, 1, 3)
    lane = jnp.arange(LANES)
    qz = jnp.concatenate([jnp.where(lane < HEAD_DIM, q_s, 0), jnp.where(lane >= HEAD_DIM, q_s, 0)], axis=2)
    pad_new = lambda a: jnp.pad(a[off_s:off_s + n_s].reshape(nb, nq, D_ATT), ((0, 0), (0, LANES - nq), (0, 0)))
    ck = cache_k[0].reshape(nb, past, D_ATT)
    cv = cache_v[0].reshape(nb, past * N_HEADS, V_DIM)

    logit_bound = 8.1 * jnp.max(jnp.abs(q_norm_g[0])) * jnp.max(jnp.abs(k_norm_g[0]))

    def attention(fast):
        def run():
            if fast:
                o = _attn_prompt_fast(qb, kb, vb, lam_row, sg, seq, out_scale, 512)
            else:
                o = _attn_prompt(qb, kb, vb, lam_row, sg, seq, out_scale, tq)
            return _attn_sample(qz, ck, cv, pad_new(kb), pad_new(vb), o, lam_row, sg, off_s, out_scale,
                                512, fast)
        return run

    o_att = lax.cond(logit_bound <= LOGIT_BOUND_MAX, attention(True), attention(False))

    gp = N_SSM_GROUPS * SSM_STATE
    s5w = _s5_weights(ssm_a_re[0], ssm_a_im[0], ssm_log_dt[0], ssm_b_re[0], ssm_b_im[0],
                      ssm_c_re[0], ssm_c_im[0])
    ys, hp_re, hp_im, hs_re, hs_im = _s5(z, s5w, ssm_d[0], state_ssm_re[0].reshape(nb, gp),
                                         state_ssm_im[0].reshape(nb, gp), seq, nb, nq)
    ysg = _glu(ys, w_glu[0], b_glu[0], tm, 512)
    m = _merge(ysg, o_att, w_ssm_proj[0], w_att_proj[0], z, tm, 512)
    x2 = _outproj(m, w_o[0], x_cat, tm, 512)

    w_r = jnp.concatenate([w_router_group[0], w_router_expert[0],
                           jnp.zeros((D_MODEL, LANES - N_EGROUPS - N_EXPERTS), F32)], axis=1).astype(BF16)
    b_r = jnp.concatenate([b_router_group[0], b_router_expert[0],
                           jnp.zeros((LANES - N_EGROUPS - N_EXPERTS,), F32)]).reshape(1, LANES)
    h2, e_sel, g_sel = _router(x2, norm2_g[0], w_r, b_r, tr)

    plan = _route_plan(e_sel[:t_real, :TOP_K], t_real, tp)
    y2 = _experts(h2, plan, t_real, w1_e[0], w3_e[0], w2_e[0])

    def heads(a, lead):
        return a.reshape(lead + (N_HEADS, 2, HEAD_DIM))

    y_prompt = _combine(x2, g_sel, y2, 0, seq, tr).reshape(1, seq, D_MODEL)
    y_sample = _combine(x2, g_sel, y2, off_s, n_s, nq).reshape(nb, nq, D_MODEL)
    k_p = jnp.concatenate([kf[off_meta:off_meta + N_META], kf[:seq]], axis=0)
    v_p = jnp.concatenate([vf[off_meta:off_meta + N_META], vf[:seq]], axis=0)
    k_prompt = heads(k_p, (1, 1, seq + N_META))
    v_prompt = v_p.reshape(1, 1, seq + N_META, N_HEADS, V_DIM)
    k_sample = heads(kf[off_s:off_s + n_s], (1, nb, nq))
    v_sample = vf[off_s:off_s + n_s].reshape(1, nb, nq, N_HEADS, V_DIM)
    st = lambda a, lead: a.reshape(lead + (N_SSM_GROUPS, SSM_STATE))
    return (y_prompt, y_sample, k_prompt, v_prompt, st(hp_re, (1, 1)), st(hp_im, (1, 1)),
            k_sample, v_sample, st(hs_re, (1, nb)), st(hs_im, (1, nb)))
```

```python
import functools
import math

import jax
import jax.numpy as jnp
from jax import lax
from jax.experimental import pallas as pl
from jax.experimental.pallas import tpu as pltpu

F32 = jnp.float32
BF16 = jnp.bfloat16

D_MODEL = 4096
N_META = 16
CHUNK = 64
N_HEADS = 16
HEAD_DIM = 64
V_DIM = 128
D_ATT = N_HEADS * V_DIM
D_SSM = 2048
SSM_GROUP = 16
N_SSM_GROUPS = D_SSM // SSM_GROUP
SSM_STATE = 64
IN_WIDTH = D_SSM + 3 * D_ATT + 2 * D_MODEL
ROPE_THETA = 10000.0
N_EGROUPS = 8
EXPERTS_PER_GROUP = 8
N_EXPERTS = N_EGROUPS * EXPERTS_PER_GROUP
TOP_K = 2
D_EXPERT = 512
MOE_BLOCK = 320
EXPERT_SLICES = 4
EPS = 1e-6

LANES = 128
ROW_ALIGN = 512
S5_CHUNK = 8
S5_LANE_GROUPS = LANES // SSM_GROUP
S5_TILES = D_SSM // LANES
VMEM_LIMIT = 56 * 1024 * 1024


def _cparams(sem, vmem=VMEM_LIMIT):
    return pltpu.CompilerParams(dimension_semantics=sem, vmem_limit_bytes=vmem)


def _row_tile(tp, cap):
    best = 16
    for t in range(16, cap + 1, 16):
        if tp % t == 0:
            best = t
    return best


def _dot(a, b):
    return jnp.dot(a, b, preferred_element_type=F32)


def _dot_nt(a, b):
    return lax.dot_general(a, b, (((1,), (1,)), ((), ())), preferred_element_type=F32)


def _rmsnorm_kernel(x_ref, g_ref, o_ref):
    x = x_ref[...]
    ms = jnp.mean(x * x, axis=-1, keepdims=True)
    o_ref[...] = (x * lax.rsqrt(ms + EPS) * g_ref[...]).astype(o_ref.dtype)


def _rmsnorm(x, g, tr):
    tp, d = x.shape
    return pl.pallas_call(
        _rmsnorm_kernel,
        grid=(tp // tr,),
        in_specs=[pl.BlockSpec((tr, d), lambda i: (i, 0)),
                  pl.BlockSpec((1, d), lambda i: (0, 0))],
        out_specs=pl.BlockSpec((tr, d), lambda i: (i, 0)),
        out_shape=jax.ShapeDtypeStruct((tp, d), BF16),
        compiler_params=_cparams(("parallel",)),
        name="rmsnorm1",
    )(x, g.reshape(1, d))


def _inproj_kernel(x_ref, w_ref, b_ref, o_ref):
    o_ref[...] = _dot(x_ref[...], w_ref[...].astype(BF16)) + b_ref[...]


def _inproj(h, w, b, tm, tn):
    tp, k = h.shape
    n = w.shape[1]
    return pl.pallas_call(
        _inproj_kernel,
        grid=(n // tn, tp // tm),
        in_specs=[pl.BlockSpec((tm, k), lambda j, i: (i, 0)),
                  pl.BlockSpec((k, tn), lambda j, i: (0, j)),
                  pl.BlockSpec((1, tn), lambda j, i: (0, j))],
        out_specs=pl.BlockSpec((tm, tn), lambda j, i: (i, j)),
        out_shape=jax.ShapeDtypeStruct((tp, n), F32),
        compiler_params=_cparams(("parallel", "parallel")),
        name="in_proj",
    )(h, w, b.reshape(1, n))


def _glu_kernel(x_ref, w_ref, b_ref, xe_ref, o_ref):
    a = _dot(x_ref[...].astype(BF16), w_ref[...].astype(BF16)) + b_ref[...]
    o_ref[...] = (xe_ref[...] * jax.nn.sigmoid(a)).astype(o_ref.dtype)


def _glu(ys, w, b, tm, tn):
    tp, k = ys.shape
    n = w.shape[1]
    return pl.pallas_call(
        _glu_kernel,
        grid=(n // tn, tp // tm),
        in_specs=[pl.BlockSpec((tm, k), lambda j, i: (i, 0)),
                  pl.BlockSpec((k, tn), lambda j, i: (0, j)),
                  pl.BlockSpec((1, tn), lambda j, i: (0, j)),
                  pl.BlockSpec((tm, tn), lambda j, i: (i, j))],
        out_specs=pl.BlockSpec((tm, tn), lambda j, i: (i, j)),
        out_shape=jax.ShapeDtypeStruct((tp, n), BF16),
        compiler_params=_cparams(("parallel", "parallel")),
        name="glu",
    )(ys, w, b.reshape(1, n), ys)


def _merge_kernel(ys_ref, oa_ref, ws_ref, wa_ref, gs_ref, ga_ref, o_ref):
    a = _dot(ys_ref[...], ws_ref[...].astype(BF16))
    b = _dot(oa_ref[...], wa_ref[...].astype(BF16))
    m = jax.nn.sigmoid(gs_ref[...]) * a + jax.nn.sigmoid(ga_ref[...]) * b
    o_ref[...] = m.astype(o_ref.dtype)


def _merge(ysg, oatt, w_ssm, w_att, z, tm, tn):
    tp, k = ysg.shape
    n = w_ssm.shape[1]
    gs_blk = (D_SSM + 3 * D_ATT) // tn
    ga_blk = (D_SSM + 3 * D_ATT + D_MODEL) // tn
    return pl.pallas_call(
        _merge_kernel,
        grid=(n // tn, tp // tm),
        in_specs=[pl.BlockSpec((tm, k), lambda j, i: (i, 0)),
                  pl.BlockSpec((tm, k), lambda j, i: (i, 0)),
                  pl.BlockSpec((k, tn), lambda j, i: (0, j)),
                  pl.BlockSpec((k, tn), lambda j, i: (0, j)),
                  pl.BlockSpec((tm, tn), lambda j, i: (i, gs_blk + j)),
                  pl.BlockSpec((tm, tn), lambda j, i: (i, ga_blk + j))],
        out_specs=pl.BlockSpec((tm, tn), lambda j, i: (i, j)),
        out_shape=jax.ShapeDtypeStruct((tp, n), BF16),
        compiler_params=_cparams(("parallel", "parallel")),
        name="merge_proj",
    )(ysg, oatt, w_ssm, w_att, z, z)


def _outproj_kernel(m_ref, w_ref, x_ref, o_ref):
    o_ref[...] = x_ref[...] + _dot(m_ref[...], w_ref[...].astype(BF16))


def _outproj(m, w, x, tm, tn):
    tp, k = m.shape
    n = w.shape[1]
    return pl.pallas_call(
        _outproj_kernel,
        grid=(n // tn, tp // tm),
        in_specs=[pl.BlockSpec((tm, k), lambda j, i: (i, 0)),
                  pl.BlockSpec((k, tn), lambda j, i: (0, j)),
                  pl.BlockSpec((tm, tn), lambda j, i: (i, j))],
        out_specs=pl.BlockSpec((tm, tn), lambda j, i: (i, j)),
        out_shape=jax.ShapeDtypeStruct((tp, n), F32),
        compiler_params=_cparams(("parallel", "parallel")),
        name="out_proj",
    )(m, w, x)


def _segment_sumsq(x, ones_bd):
    x2 = x * x
    hi = x2.astype(BF16)
    lo = (x2 - hi.astype(F32)).astype(BF16)
    return _dot(hi, ones_bd) + _dot(lo, ones_bd)


def _qk_rope_kernel(zq_ref, zk_ref, zv_ref, cos_ref, sin_ref, gq_ref, gk_ref, ones_ref,
                    qb_ref, kf_ref, kb_ref, vf_ref, vb_ref):
    cos = cos_ref[...]
    sin = sin_ref[...]
    ones_bd = ones_ref[...]
    lane = lax.broadcasted_iota(jnp.int32, cos.shape, 1)
    first_half = (lane % HEAD_DIM) < (HEAD_DIM // 2)

    def norm_rope(x, g):
        ss = _segment_sumsq(x, ones_bd)
        xn = x * lax.rsqrt(ss * (1.0 / HEAD_DIM) + EPS) * g
        partner = jnp.where(first_half,
                            pltpu.roll(xn, LANES - HEAD_DIM // 2, 1),
                            pltpu.roll(xn, HEAD_DIM // 2, 1))
        return xn * cos + partner * sin

    for h in range(N_HEADS):
        sl = slice(h * LANES, (h + 1) * LANES)
        q = norm_rope(zq_ref[:, sl], gq_ref[...])
        qb_ref[:, sl] = (q * (HEAD_DIM ** -0.5)).astype(BF16)
        k = norm_rope(zk_ref[:, sl], gk_ref[...])
        kf_ref[:, sl] = k
        kb_ref[:, sl] = k.astype(BF16)
    v = zv_ref[...]
    vf_ref[...] = v
    vb_ref[...] = v.astype(BF16)


def _qk_rope(z, cos_t, sin_t, gq, gk, tr):
    tp = z.shape[0]
    ones_bd = jnp.kron(jnp.eye(LANES // HEAD_DIM, dtype=F32),
                       jnp.ones((HEAD_DIM, HEAD_DIM), F32)).astype(BF16)
    zspec = lambda c: pl.BlockSpec((tr, D_ATT), lambda i: (i, c))
    row = pl.BlockSpec((tr, LANES), lambda i: (i, 0))
    const = pl.BlockSpec((1, LANES), lambda i: (0, 0))
    out = pl.BlockSpec((tr, D_ATT), lambda i: (i, 0))
    q_blk = D_SSM // D_ATT
    return pl.pallas_call(
        _qk_rope_kernel,
        grid=(tp // tr,),
        in_specs=[zspec(q_blk), zspec(q_blk + 1), zspec(q_blk + 2), row, row, const, const,
                  pl.BlockSpec((LANES, LANES), lambda i: (0, 0))],
        out_specs=[out, out, out, out, out],
        out_shape=[jax.ShapeDtypeStruct((tp, D_ATT), BF16),
                   jax.ShapeDtypeStruct((tp, D_ATT), F32),
                   jax.ShapeDtypeStruct((tp, D_ATT), BF16),
                   jax.ShapeDtypeStruct((tp, D_ATT), F32),
                   jax.ShapeDtypeStruct((tp, D_ATT), BF16)],
        compiler_params=_cparams(("parallel",)),
        name="qk_norm_rope",
    )(z, z, z, cos_t, sin_t, gq, gk, ones_bd)


def _softmax_step(c, qc, kt, vt, mask, m_ref, l_ref, acc_ref):
    s = _dot_nt(qc, kt)
    if mask is not None:
        s = jnp.where(mask, s, -jnp.inf)
    m_prev = m_ref[c]
    m_new = jnp.maximum(m_prev, jnp.max(s, axis=1, keepdims=True))
    alpha = jnp.exp(m_prev - m_new)
    p = jnp.exp(s - m_new[:, :1])
    l_ref[c] = alpha * l_ref[c] + jnp.sum(p, axis=1, keepdims=True)
    acc_ref[c] = alpha * acc_ref[c] + _dot(p.astype(BF16), vt)
    m_ref[c] = m_new


def _diff_finish(o0, o1, lam, g, out_scale):
    o = o0 - lam * o1
    ms = jnp.mean(o * o, axis=-1, keepdims=True)
    return o * lax.rsqrt(ms + EPS) * g * out_scale


def _split_components(q):
    lane = lax.broadcasted_iota(jnp.int32, q.shape, 1)
    zero = jnp.zeros_like(q)
    return jnp.where(lane < HEAD_DIM, q, zero), jnp.where(lane >= HEAD_DIM, q, zero)


def _attn_prompt_kernel(lam_ref, g_ref, q_ref, k_ref, v_ref, o_ref, m_ref, l_ref, acc_ref,
                        *, tq, nq_main, seq, out_scale):
    i = pl.program_id(1)
    q0, q1 = _split_components(q_ref[...])
    m_ref[...] = jnp.full(m_ref.shape, -jnp.inf, F32)
    l_ref[...] = jnp.zeros(l_ref.shape, F32)
    acc_ref[...] = jnp.zeros(acc_ref.shape, F32)

    def update(kt, vt, mask):
        _softmax_step(0, q0, kt, vt, mask, m_ref, l_ref, acc_ref)
        _softmax_step(1, q1, kt, vt, mask, m_ref, l_ref, acc_ref)

    col = lax.broadcasted_iota(jnp.int32, (tq, LANES), 1)
    update(k_ref[pl.ds(seq, LANES), :], v_ref[pl.ds(seq, LANES), :], col < N_META)

    is_main = i < nq_main

    def body(j, carry):
        start = pl.multiple_of(j * tq, tq)
        update(k_ref[pl.ds(start, tq), :], v_ref[pl.ds(start, tq), :], None)
        return carry

    lax.fori_loop(0, jnp.where(is_main, i, 0), body, 0)

    @pl.when(is_main)
    def _():
        start = pl.multiple_of(i * tq, tq)
        r = lax.broadcasted_iota(jnp.int32, (tq, tq), 0) // CHUNK
        c = lax.broadcasted_iota(jnp.int32, (tq, tq), 1) // CHUNK
        update(k_ref[pl.ds(start, tq), :], v_ref[pl.ds(start, tq), :], c <= r)

    o0 = acc_ref[0] / l_ref[0]
    o1 = acc_ref[1] / l_ref[1]
    o_ref[...] = _diff_finish(o0, o1, lam_ref[...], g_ref[...], out_scale).astype(o_ref.dtype)


def _attn_prompt(qb, kb, vb, lam_row, subln_g, seq, out_scale, tq):
    tp = qb.shape[0]
    kern = functools.partial(_attn_prompt_kernel, tq=tq, nq_main=seq // tq, seq=seq,
                             out_scale=out_scale)
    const = pl.BlockSpec((1, LANES), lambda h, i: (0, 0))
    return pl.pallas_call(
        kern,
        grid=(N_HEADS, tp // tq),
        in_specs=[const, const,
                  pl.BlockSpec((tq, LANES), lambda h, i: (i, h)),
                  pl.BlockSpec((tp, LANES), lambda h, i: (0, h)),
                  pl.BlockSpec((tp, LANES), lambda h, i: (0, h))],
        out_specs=pl.BlockSpec((tq, LANES), lambda h, i: (i, h)),
        out_shape=jax.ShapeDtypeStruct((tp, D_ATT), BF16),
        scratch_shapes=[pltpu.VMEM((2, tq, LANES), F32),
                        pltpu.VMEM((2, tq, LANES), F32),
                        pltpu.VMEM((2, tq, LANES), F32)],
        compiler_params=_cparams(("parallel", "parallel")),
        name="attn_prompt",
    )(lam_row, subln_g, qb, kb, vb)


LOGIT_BOUND_MAX = 40.0
ATTN_WIDE = 4


def _with_ones(vt):
    return jnp.concatenate([vt, jnp.ones(vt.shape, vt.dtype)], axis=1)


def _cache_v_head(vc_ref, h):
    tk = vc_ref.shape[0] // N_HEADS
    return vc_ref[pl.ds(h, tk, stride=N_HEADS), :].astype(BF16)


def _attn_prompt_fast_kernel(lam_ref, g_ref, q_ref, k_ref, v_ref, o_ref, acc_ref,
                             *, tq, nq_main, seq, out_scale):
    i = pl.program_id(1)
    q0, q1 = _split_components(q_ref[...])
    qq = jnp.concatenate([q0, q1], axis=0)

    def scores(start, rows):
        return _dot_nt(qq, k_ref[pl.ds(start, rows), :])

    def weighted(s, start, rows, mask):
        p = jnp.exp(s)
        if mask is not None:
            p = jnp.where(mask, p, 0.0)
        return _dot(p.astype(BF16), _with_ones(v_ref[pl.ds(start, rows), :]))

    def tile_pv(start, rows, mask):
        return weighted(scores(start, rows), start, rows, mask)

    col = lax.broadcasted_iota(jnp.int32, (2 * tq, LANES), 1)
    acc_ref[...] = tile_pv(seq, LANES, col < N_META)

    is_main = i < nq_main
    n_full = jnp.where(is_main, i, 0)

    wide = ATTN_WIDE

    def body(j, carry):
        start = pl.multiple_of(wide * j * tq, tq)
        acc_ref[...] += tile_pv(start, wide * tq, None)
        return carry

    n_wide = n_full // wide
    lax.fori_loop(0, n_wide, body, 0)
    rem = n_full - wide * n_wide
    base = wide * n_wide

    @pl.when(rem >= 2)
    def _():
        acc_ref[...] += tile_pv(pl.multiple_of(base * tq, tq), 2 * tq, None)

    @pl.when(rem % 2 == 1)
    def _():
        acc_ref[...] += tile_pv(pl.multiple_of((n_full - 1) * tq, tq), tq, None)

    @pl.when(is_main)
    def _():
        r = (lax.broadcasted_iota(jnp.int32, (2 * tq, tq), 0) % tq) // CHUNK
        c = lax.broadcasted_iota(jnp.int32, (2 * tq, tq), 1) // CHUNK
        acc_ref[...] += tile_pv(pl.multiple_of(i * tq, tq), tq, c <= r)

    acc = acc_ref[...]
    o0 = acc[:tq, :LANES] / acc[:tq, LANES:]
    o1 = acc[tq:, :LANES] / acc[tq:, LANES:]
    o_ref[...] = _diff_finish(o0, o1, lam_ref[...], g_ref[...], out_scale).astype(o_ref.dtype)


def _attn_prompt_fast(qb, kb, vb, lam_row, subln_g, seq, out_scale, tq):
    tp = qb.shape[0]
    kern = functools.partial(_attn_prompt_fast_kernel, tq=tq, nq_main=seq // tq, seq=seq,
                             out_scale=out_scale)
    const = pl.BlockSpec((1, LANES), lambda h, i: (0, 0))
    return pl.pallas_call(
        kern,
        grid=(N_HEADS, tp // tq),
        in_specs=[const, const,
                  pl.BlockSpec((tq, LANES), lambda h, i: (i, h)),
                  pl.BlockSpec((tp, LANES), lambda h, i: (0, h)),
                  pl.BlockSpec((tp, LANES), lambda h, i: (0, h))],
        out_specs=pl.BlockSpec((tq, LANES), lambda h, i: (i, h)),
        out_shape=jax.ShapeDtypeStruct((tp, D_ATT), BF16),
        scratch_shapes=[pltpu.VMEM((2 * tq, 2 * LANES), F32)],
        compiler_params=_cparams(("parallel", "parallel")),
        name="attn_prompt_fast",
    )(lam_row, subln_g, qb, kb, vb)


def _attn_sample_fast_kernel(lam_ref, g_ref, q_ref, kc_ref, vc_ref, kn_ref, vn_ref, o_in_ref, o_ref,
                             acc_ref, *, nq, out_scale):
    del o_in_ref
    j = pl.program_id(1)
    rows = 2 * nq

    def head_update(h, kt, vt, mask):
        p = jnp.exp(_dot_nt(q_ref[h], kt))
        if mask is not None:
            p = jnp.where(mask, p, 0.0)
        acc_ref[h] += _dot(p.astype(BF16), _with_ones(vt))

    @pl.when(j == 0)
    def _():
        acc_ref[...] = jnp.zeros(acc_ref.shape, F32)
        col = lax.broadcasted_iota(jnp.int32, (rows, LANES), 1)
        for h in range(N_HEADS):
            sl = slice(h * LANES, (h + 1) * LANES)
            head_update(h, kn_ref[:, sl], vn_ref[:, sl], col < nq)

    for h in range(N_HEADS):
        sl = slice(h * LANES, (h + 1) * LANES)
        head_update(h, kc_ref[:, sl].astype(BF16), _cache_v_head(vc_ref, h), None)

    @pl.when(j == pl.num_programs(1) - 1)
    def _():
        for h in range(N_HEADS):
            acc = acc_ref[h]
            o = acc[:, :LANES] / acc[:, LANES:]
            res = _diff_finish(o[:nq], o[nq:], lam_ref[...], g_ref[...], out_scale)
            o_ref[:, h * LANES:(h + 1) * LANES] = res.astype(o_ref.dtype)


def _attn_sample_kernel(lam_ref, g_ref, q_ref, kc_ref, vc_ref, kn_ref, vn_ref, o_in_ref, o_ref,
                        m_ref, l_ref, acc_ref, *, nq, out_scale):
    del o_in_ref
    j = pl.program_id(1)
    rows = 2 * nq

    @pl.when(j == 0)
    def _():
        m_ref[...] = jnp.full(m_ref.shape, -jnp.inf, F32)
        l_ref[...] = jnp.zeros(l_ref.shape, F32)
        acc_ref[...] = jnp.zeros(acc_ref.shape, F32)
        col = lax.broadcasted_iota(jnp.int32, (rows, LANES), 1)
        for h in range(N_HEADS):
            sl = slice(h * LANES, (h + 1) * LANES)
            _softmax_step(h, q_ref[h], kn_ref[:, sl], vn_ref[:, sl], col < nq,
                          m_ref, l_ref, acc_ref)

    for h in range(N_HEADS):
        sl = slice(h * LANES, (h + 1) * LANES)
        _softmax_step(h, q_ref[h], kc_ref[:, sl].astype(BF16), _cache_v_head(vc_ref, h), None,
                      m_ref, l_ref, acc_ref)

    @pl.when(j == pl.num_programs(1) - 1)
    def _():
        for h in range(N_HEADS):
            o = acc_ref[h] / l_ref[h]
            res = _diff_finish(o[:nq], o[nq:], lam_ref[...], g_ref[...], out_scale)
            o_ref[:, h * LANES:(h + 1) * LANES] = res.astype(o_ref.dtype)


def _attn_sample(qz, cache_k, cache_v, k_new, v_new, o_buf, lam_row, subln_g, row_off, out_scale, tk, fast):
    nb, past = cache_k.shape[0], cache_k.shape[1]
    nq = qz.shape[2] // 2
    if fast:
        kern = functools.partial(_attn_sample_fast_kernel, nq=nq, out_scale=out_scale)
        scratch = [pltpu.VMEM((N_HEADS, 2 * nq, 2 * LANES), F32)]
    else:
        kern = functools.partial(_attn_sample_kernel, nq=nq, out_scale=out_scale)
        scratch = [pltpu.VMEM((N_HEADS, 2 * nq, LANES), F32)] * 3
    const = pl.BlockSpec((1, LANES), lambda b, j: (0, 0))
    blk_off = row_off // nq
    return pl.pallas_call(
        kern,
        grid=(nb, past // tk),
        in_specs=[const, const,
                  pl.BlockSpec((None, N_HEADS, 2 * nq, LANES), lambda b, j: (b, 0, 0, 0)),
                  pl.BlockSpec((None, tk, D_ATT), lambda b, j: (b, j, 0)),
                  pl.BlockSpec((None, tk * N_HEADS, V_DIM), lambda b, j: (b, j, 0)),
                  pl.BlockSpec((None, LANES, D_ATT), lambda b, j: (b, 0, 0)),
                  pl.BlockSpec((None, LANES, D_ATT), lambda b, j: (b, 0, 0)),
                  pl.BlockSpec(memory_space=pl.ANY)],
        out_specs=pl.BlockSpec((nq, D_ATT), lambda b, j: (blk_off + b, 0)),
        out_shape=jax.ShapeDtypeStruct(o_buf.shape, o_buf.dtype),
        scratch_shapes=scratch,
        input_output_aliases={7: 0},
        compiler_params=_cparams(("parallel", "arbitrary")),
        name="attn_sample_fast" if fast else "attn_sample",
    )(lam_row, subln_g, qz, cache_k, cache_v, k_new, v_new, o_buf)


def _s5_weights(a_re, a_im, log_dt, b_re, b_im, c_re, c_im):
    hp = lax.Precision.HIGHEST
    n_t, gl, tc = S5_TILES, S5_LANE_GROUPS, S5_CHUNK
    dt = jnp.exp(log_dt)[:, None]
    mag = jnp.exp(a_re * dt)
    abar_re = mag * jnp.cos(a_im * dt)
    abar_im = mag * jnp.sin(a_im * dt)
    nr, ni = abar_re - 1.0, abar_im
    den = a_re * a_re + a_im * a_im
    coef_re = (nr * a_re + ni * a_im) / den
    coef_im = (ni * a_re - nr * a_im) / den
    bbar_re = coef_re[..., None] * b_re - coef_im[..., None] * b_im
    bbar_im = coef_re[..., None] * b_im + coef_im[..., None] * b_re
    n = jnp.arange(tc + 1, dtype=F32)[:, None, None]
    pw_mag = jnp.exp(n * (a_re * dt))
    pw_re = pw_mag * jnp.cos(n * (a_im * dt))
    pw_im = pw_mag * jnp.sin(n * (a_im * dt))
    e_re = pw_re[:tc, :, :, None] * bbar_re - pw_im[:tc, :, :, None] * bbar_im
    e_im = pw_re[:tc, :, :, None] * bbar_im + pw_im[:tc, :, :, None] * bbar_re
    kern = (jnp.einsum('gcp,lgpd->glcd', c_re, e_re, precision=hp)
            - jnp.einsum('gcp,lgpd->glcd', c_im, e_im, precision=hp))
    eye = jnp.eye(gl, dtype=F32)
    w_intra = jnp.einsum('jglcd,gh->jlgdhc', kern.reshape(n_t, gl, tc, SSM_GROUP, SSM_GROUP), eye)
    w_intra = w_intra.reshape(n_t, tc, LANES, LANES)
    eb = jnp.stack([e_re[::-1], e_im[::-1]], 0)
    eb = eb.reshape(2, tc, n_t, gl, SSM_STATE, SSM_GROUP)
    w_state = eb.transpose(2, 1, 3, 5, 0, 4).reshape(n_t, tc, LANES, 2 * SSM_STATE)
    cp_re = c_re[None] * pw_re[1:, :, None, :] - c_im[None] * pw_im[1:, :, None, :]
    cp_im = c_re[None] * pw_im[1:, :, None, :] + c_im[None] * pw_re[1:, :, None, :]
    cp = jnp.stack([cp_re, -cp_im], 0).reshape(2, tc, n_t, gl, SSM_GROUP, SSM_STATE)
    w_read = cp.transpose(2, 1, 0, 5, 3, 4).reshape(n_t, tc, 2 * SSM_STATE, LANES)
    half = gl * SSM_STATE
    a_pow = jnp.concatenate([pw_re[tc].reshape(n_t, 1, half), pw_im[tc].reshape(n_t, 1, half)], -1)
    rp = jnp.arange(2 * SSM_STATE)
    col = jnp.arange(2 * half)
    spread = ((rp[:, None] // SSM_STATE == col[None, :] // half)
              & (rp[:, None] % SSM_STATE == col[None, :] % SSM_STATE)).astype(BF16)
    return (w_intra.astype(BF16), w_state.astype(BF16), w_read.astype(BF16), a_pow, spread, spread.T)


def _s5_kernel(u_ref, wi_ref, wsc_ref, wrc_ref, ap_ref, sp_ref, spt_ref, d_ref, h0r_ref, h0i_ref,
               y_ref, hpr_ref, hpi_ref, hsr_ref, hsi_ref,
               y_acc, v_ref, hs_ref, wt_ref, ws_ref, wr_ref,
               *, nc, n_main, n_meta_chunks, n_seq, seq_chunks):
    tc = S5_CHUNK
    half = hs_ref.shape[1] // 2
    grp_r = lax.broadcasted_iota(jnp.int32, (LANES, 2 * half), 0) // SSM_GROUP
    grp_c = (lax.broadcasted_iota(jnp.int32, (LANES, 2 * half), 1) % half) // SSM_STATE
    for s in range(tc):
        full = _dot(wsc_ref[s], sp_ref[...])
        ws_ref[s * LANES:(s + 1) * LANES, :] = jnp.where(grp_r == grp_c, full, 0.0).astype(BF16)
    grp_r = (lax.broadcasted_iota(jnp.int32, (2 * half, LANES), 0) % half) // SSM_STATE
    grp_c = lax.broadcasted_iota(jnp.int32, (2 * half, LANES), 1) // SSM_GROUP
    for t in range(tc):
        full = _dot(spt_ref[...], wrc_ref[t])
        wr_ref[:, t * LANES:(t + 1) * LANES] = jnp.where(grp_r == grp_c, full, 0.0).astype(BF16)
    for s in range(tc):
        for t in range(tc):
            blk = wi_ref[t - s] if t >= s else jnp.zeros((LANES, LANES), BF16)
            wt_ref[s * LANES:(s + 1) * LANES, t * LANES:(t + 1) * LANES] = blk
    lhs = jnp.concatenate(
        [u_ref[pl.ds(s, nc, stride=tc), :].astype(BF16) for s in range(tc)], axis=1)
    y_acc[...] = _dot(lhs, wt_ref[...])
    v_ref[...] = _dot(lhs, ws_ref[...])
    a_re = ap_ref[:, :half]
    a_im = ap_ref[:, half:]

    def advance(h_re, h_im, v):
        return (a_re * h_re - a_im * h_im + v[:, :half],
                a_re * h_im + a_im * h_re + v[:, half:])

    hs_ref[...] = jnp.zeros(hs_ref.shape, F32)

    h_re = jnp.zeros((1, half), F32)
    h_im = jnp.zeros((1, half), F32)
    for c in range(n_main, n_main + n_meta_chunks):
        hs_ref[pl.ds(c, 1), :] = jnp.concatenate([h_re, h_im], axis=1)
        h_re, h_im = advance(h_re, h_im, v_ref[pl.ds(c, 1), :])

    def body(c, carry):
        h_re, h_im = carry
        hs_ref[pl.ds(c, 1), :] = jnp.concatenate([h_re, h_im], axis=1)
        return advance(h_re, h_im, v_ref[pl.ds(c, 1), :])

    h_re, h_im = lax.fori_loop(0, n_main, body, (h_re, h_im))
    hpr_ref[...] = h_re
    hpi_ref[...] = h_im

    base = n_main + n_meta_chunks
    for b in range(n_seq):
        s_re = h0r_ref[pl.ds(b, 1), :]
        s_im = h0i_ref[pl.ds(b, 1), :]
        for c in range(base + b * seq_chunks, base + (b + 1) * seq_chunks):
            hs_ref[pl.ds(c, 1), :] = jnp.concatenate([s_re, s_im], axis=1)
            s_re, s_im = advance(s_re, s_im, v_ref[pl.ds(c, 1), :])
        hsr_ref[pl.ds(b, 1), :] = s_re
        hsi_ref[pl.ds(b, 1), :] = s_im

    y_acc[...] += _dot(hs_ref[...].astype(BF16), wr_ref[...])
    d = d_ref[...]
    for t in range(tc):
        rows = pl.ds(t, nc, stride=tc)
        y = y_acc[:, t * LANES:(t + 1) * LANES] + d * u_ref[rows, :]
        y_ref[rows, :] = jax.nn.gelu(y)


def _s5(z, weights, d_skip, h0_re, h0_im, seq, n_seq, seq_len):
    tp = z.shape[0]
    tc = S5_CHUNK
    nc = tp // tc
    w_intra, w_state, w_read, a_pow, spread, spread_t = weights
    half = S5_LANE_GROUPS * SSM_STATE
    whole = lambda a: pl.BlockSpec(a.shape, lambda j: (0,) * a.ndim)
    kern = functools.partial(_s5_kernel, nc=nc, n_main=seq // tc, n_meta_chunks=N_META // tc,
                             n_seq=n_seq, seq_chunks=seq_len // tc)
    wspec = lambda a: pl.BlockSpec((None,) + a.shape[1:], lambda j: (j,) + (0,) * (a.ndim - 1))
    col = pl.BlockSpec((tp, LANES), lambda j: (0, j))
    st = lambda r: pl.BlockSpec((r, half), lambda j: (0, j))
    gp = N_SSM_GROUPS * SSM_STATE
    return pl.pallas_call(
        kern,
        grid=(S5_TILES,),
        in_specs=[col, wspec(w_intra), wspec(w_state), wspec(w_read), wspec(a_pow),
                  whole(spread), whole(spread_t),
                  pl.BlockSpec((1, LANES), lambda j: (0, j)), st(n_seq), st(n_seq)],
        out_specs=[col, st(1), st(1), st(n_seq), st(n_seq)],
        out_shape=[jax.ShapeDtypeStruct((tp, D_SSM), F32),
                   jax.ShapeDtypeStruct((1, gp), F32),
                   jax.ShapeDtypeStruct((1, gp), F32),
                   jax.ShapeDtypeStruct((n_seq, gp), F32),
                   jax.ShapeDtypeStruct((n_seq, gp), F32)],
        scratch_shapes=[pltpu.VMEM((nc, tc * LANES), F32),
                        pltpu.VMEM((nc, 2 * half), F32),
                        pltpu.VMEM((nc, 2 * half), F32),
                        pltpu.VMEM((tc * LANES, tc * LANES), BF16),
                        pltpu.VMEM((tc * LANES, 2 * half), BF16),
                        pltpu.VMEM((2 * half, tc * LANES), BF16)],
        compiler_params=_cparams(("parallel",)),
        name="s5_scan",
    )(z, w_intra, w_state, w_read, a_pow, spread, spread_t, d_skip.reshape(1, D_SSM), h0_re, h0_im)


ROW_CHUNKS = D_MODEL // LANES
TOKEN_PITCH = ROW_CHUNKS + 1


def _store_token_major(ref, x, pitch=ROW_CHUNKS):
    n = x.shape[0]
    for c in range(ROW_CHUNKS):
        ref[pl.ds(c, n, stride=pitch), :] = x[:, c * LANES:(c + 1) * LANES].astype(ref.dtype)


def _load_token_major(ref, n, dtype, pitch=ROW_CHUNKS, chunks=range(ROW_CHUNKS)):
    return jnp.concatenate([ref[pl.ds(c, n, stride=pitch), :].astype(dtype) for c in chunks], axis=1)


def _router_kernel(x_ref, g_ref, w_ref, b_ref, h_ref, e_ref, gate_ref):
    x = x_ref[...]
    ms = jnp.mean(x * x, axis=-1, keepdims=True)
    h = x * lax.rsqrt(ms + EPS) * g_ref[...]
    _store_token_major(h_ref, h)
    logits = _dot(h.astype(BF16), w_ref[...]) + b_ref[...]
    lane = lax.broadcasted_iota(jnp.int32, logits.shape, 1)
    neg = -jnp.inf
    big = jnp.int32(LANES)

    def first_argmax(vals, vmax):
        return jnp.min(jnp.where(vals == vmax, lane, big), axis=1, keepdims=True)

    lg = jnp.where(lane < N_EGROUPS, logits, neg)
    mg = jnp.max(lg, axis=1, keepdims=True)
    sg = jnp.sum(jnp.exp(lg - mg), axis=1, keepdims=True)
    g_w = 1.0 / sg
    g_idx = first_argmax(lg, mg)
    lo = N_EGROUPS + EXPERTS_PER_GROUP * g_idx
    le = jnp.where((lane >= lo) & (lane < lo + EXPERTS_PER_GROUP), logits, neg)
    m1 = jnp.max(le, axis=1, keepdims=True)
    se = jnp.sum(jnp.exp(le - m1), axis=1, keepdims=True)
    i1 = first_argmax(le, m1)
    le2 = jnp.where(lane == i1, neg, le)
    m2 = jnp.max(le2, axis=1, keepdims=True)
    i2 = first_argmax(le2, m2)
    p1 = 1.0 / se
    p2 = jnp.exp(m2 - m1) / se
    tot = p1 + p2
    w1 = g_w * (p1 / tot)
    w2 = g_w * (p2 / tot)
    e_ref[...] = jnp.where(lane == 0, i1 - N_EGROUPS, jnp.where(lane == 1, i2 - N_EGROUPS, 0))
    gate_ref[...] = jnp.where(lane == 0, w1, jnp.where(lane == 1, w2, 0.0))


def _router(x2, g, w_r, b_r, tr):
    tp, d = x2.shape
    return pl.pallas_call(
        _router_kernel,
        grid=(tp // tr,),
        in_specs=[pl.BlockSpec((tr, d), lambda i: (i, 0)),
                  pl.BlockSpec((1, d), lambda i: (0, 0)),
                  pl.BlockSpec((d, LANES), lambda i: (0, 0)),
                  pl.BlockSpec((1, LANES), lambda i: (0, 0))],
        out_specs=[pl.BlockSpec((tr * ROW_CHUNKS, LANES), lambda i: (i, 0)),
                   pl.BlockSpec((tr, LANES), lambda i: (i, 0)),
                   pl.BlockSpec((tr, LANES), lambda i: (i, 0))],
        out_shape=[jax.ShapeDtypeStruct((tp * ROW_CHUNKS, LANES), F32),
                   jax.ShapeDtypeStruct((tp, LANES), jnp.int32),
                   jax.ShapeDtypeStruct((tp, LANES), F32)],
        compiler_params=_cparams(("parallel",)),
        name="norm2_router",
    )(x2, g.reshape(1, d), w_r, b_r)


def _expert_kernel(be_ref, nu_ref, first_ref, cnt_ref, tok_ref, dst_ref, h_hbm, w1_ref, w3_ref, w2_ref,
                   y_hbm, xbuf, xb16, acc_ref, ybuf, gsem, ssem, *, nb, plane_rows, plane_pad):
    del be_ref
    b = pl.program_id(0)
    hh = pl.program_id(1)
    last = pl.num_programs(1) - 1
    n_used = nu_ref[0]
    active = b < n_used
    slot = b % 2
    rows = xb16.shape[1]
    rc = ROW_CHUNKS
    pitch = TOKEN_PITCH
    spare_row = TOP_K * plane_rows

    def token_rows(ref, t):
        return ref.at[pl.ds(pl.multiple_of(t * rc, rc), rc), :]

    def start_gather(blk, s):
        base = first_ref[blk]
        for r in range(rows):
            pltpu.make_async_copy(token_rows(h_hbm, tok_ref[base + r]),
                                  xbuf.at[s, pl.ds(r * pitch, rc), :], gsem.at[s]).start()

    def all_rows_gathered(s):
        return pltpu.make_async_copy(h_hbm.at[pl.ds(0, rows * rc), :],
                                     xbuf.at[s, pl.ds(0, rows * rc), :], gsem.at[s])

    def all_rows_scattered():
        return pltpu.make_async_copy(ybuf.at[pl.ds(0, rows * rc), :],
                                     y_hbm.at[pl.ds(spare_row * rc, rows * rc), :], ssem)

    @pl.when((b == 0) & (hh == 0))
    def _():
        ybuf[...] = jnp.zeros(ybuf.shape, F32)
        fills = [all_rows_scattered()]
        if plane_pad:
            fills += [pltpu.make_async_copy(
                ybuf.at[pl.ds(0, plane_pad * rc), :],
                y_hbm.at[pl.ds(((k + 1) * plane_rows - plane_pad) * rc, plane_pad * rc), :], ssem)
                for k in range(TOP_K)]
        for f in fills:
            f.start()
        for f in fills:
            f.wait()

        @pl.when(active)
        def _():
            start_gather(0, 0)

    @pl.when(active & (hh == 0))
    def _():
        all_rows_gathered(slot).wait()
        per = rc // EXPERT_SLICES
        for q in range(EXPERT_SLICES):
            xb16[q] = _load_token_major(xbuf.at[slot], rows, BF16, pitch, range(q * per, (q + 1) * per))

    @pl.when(active & (hh == 1) & (b + 1 < n_used))
    def _():
        start_gather(jnp.minimum(b + 1, nb - 1), 1 - slot)

    @pl.when(active)
    def _():
        w13 = jnp.concatenate([w1_ref[...].astype(BF16), w3_ref[...].astype(BF16)], axis=1)
        part = _dot(xb16[_expert_slice(b, hh)], w13)

        @pl.when(hh == 0)
        def _():
            acc_ref[...] = part

        @pl.when(hh > 0)
        def _():
            acc_ref[...] += part

    @pl.when(active & (hh == last))
    def _():
        @pl.when(b > 0)
        def _():
            all_rows_scattered().wait()

        ac = acc_ref[...]
        hid = (jax.nn.silu(ac[:, :D_EXPERT]) * ac[:, D_EXPERT:]).astype(BF16)
        _store_token_major(ybuf, _dot(hid, w2_ref[...].astype(BF16)), pitch)
        base = first_ref[b]
        n_real = cnt_ref[b]
        for r in range(rows):
            dst = jnp.where(r < n_real, dst_ref[base + r], spare_row + r)
            pltpu.make_async_copy(ybuf.at[pl.ds(r * pitch, rc), :], token_rows(y_hbm, dst), ssem).start()

        @pl.when(b + 1 >= n_used)
        def _():
            all_rows_scattered().wait()


def _expert_slice(b, hh):
    return jnp.where(b % 2 == 0, hh, EXPERT_SLICES - 1 - hh)


def _experts(h2, plan, t_real, w1, w3, w2):
    tp = h2.shape[0] // ROW_CHUNKS
    assert tp - t_real <= MOE_BLOCK
    block_expert, n_used, first, cnt, tok_sorted, dst_sorted = plan
    d = D_MODEL
    nb = block_expert.shape[0]
    dk = d // EXPERT_SLICES

    def eidx(b, be, nu):
        return be[jnp.minimum(b, nu[0] - 1)]

    def sidx(b, hh, nu):
        live = b < nu[0]
        return _expert_slice(jnp.minimum(b, nu[0] - 1), jnp.where(live, hh, EXPERT_SLICES - 1))

    grid_spec = pltpu.PrefetchScalarGridSpec(
        num_scalar_prefetch=6,
        grid=(nb, EXPERT_SLICES),
        in_specs=[pl.BlockSpec(memory_space=pl.ANY),
                  pl.BlockSpec((None, dk, D_EXPERT), lambda b, hh, be, nu, *_: (eidx(b, be, nu), sidx(b, hh, nu), 0)),
                  pl.BlockSpec((None, dk, D_EXPERT), lambda b, hh, be, nu, *_: (eidx(b, be, nu), sidx(b, hh, nu), 0)),
                  pl.BlockSpec((None, D_EXPERT, d), lambda b, hh, be, nu, *_: (eidx(b, be, nu), 0, 0))],
        out_specs=pl.BlockSpec(memory_space=pl.ANY),
        scratch_shapes=[pltpu.VMEM((2, MOE_BLOCK * TOKEN_PITCH, LANES), F32),
                        pltpu.VMEM((EXPERT_SLICES, MOE_BLOCK, dk), BF16),
                        pltpu.VMEM((MOE_BLOCK, 2 * D_EXPERT), F32),
                        pltpu.VMEM((MOE_BLOCK * TOKEN_PITCH, LANES), F32),
                        pltpu.SemaphoreType.DMA((2,)),
                        pltpu.SemaphoreType.DMA(())],
    )
    return pl.pallas_call(
        functools.partial(_expert_kernel, nb=nb, plane_rows=tp, plane_pad=tp - t_real),
        grid_spec=grid_spec,
        out_shape=jax.ShapeDtypeStruct(((TOP_K * tp + MOE_BLOCK) * ROW_CHUNKS, LANES), F32),
        compiler_params=_cparams(("arbitrary", "arbitrary"), 60 * 1024 * 1024),
        name="expert_mlp",
    )(block_expert, n_used, first, cnt, tok_sorted, dst_sorted, h2, w1, w3, w2)


def _route_plan(expert, t_real, tp):
    s = t_real * TOP_K
    n_blocks = -(-(s + N_EXPERTS * (MOE_BLOCK - 1)) // MOE_BLOCK)
    flat_e = expert.reshape(-1).astype(jnp.int32)
    se, order = lax.sort((flat_e, jnp.arange(s, dtype=jnp.int32)), num_keys=1, is_stable=True)
    bounds = jnp.searchsorted(se, jnp.arange(N_EXPERTS + 1, dtype=jnp.int32)).astype(jnp.int32)
    start = bounds[:-1]
    counts = bounds[1:] - start
    padded = (counts + MOE_BLOCK - 1) // MOE_BLOCK * MOE_BLOCK
    pad_end = jnp.cumsum(padded)
    pad_start = pad_end - padded
    block_start = jnp.arange(n_blocks, dtype=jnp.int32) * MOE_BLOCK
    block_expert = jnp.minimum(jnp.searchsorted(pad_end, block_start, side='right'),
                               N_EXPERTS - 1).astype(jnp.int32)
    n_used = (pad_end[-1] // MOE_BLOCK).astype(jnp.int32).reshape(1)
    cnt = jnp.clip(counts[block_expert] - (block_start - pad_start[block_expert]), 0, MOE_BLOCK)
    first = jnp.clip(start[block_expert] + block_start - pad_start[block_expert], 0, s).astype(jnp.int32)
    tail = jnp.zeros((MOE_BLOCK,), jnp.int32)
    tok = order // TOP_K
    tok_sorted = jnp.concatenate([tok, tail])
    dst_sorted = jnp.concatenate([(order % TOP_K) * tp + tok, tail])
    return block_expert, n_used, first, cnt.astype(jnp.int32), tok_sorted, dst_sorted


def _combine_kernel(x_ref, g_ref, y0_ref, y1_ref, o_ref):
    n = x_ref.shape[0]
    g = g_ref[...]
    y = (g[:, 0:1] * _load_token_major(y0_ref, n, F32)
         + g[:, 1:2] * _load_token_major(y1_ref, n, F32))
    o_ref[...] = x_ref[...] + y


def _combine(x2, gates, y2, row_off, n_rows, tile):
    tp, d = x2.shape
    off = row_off // tile
    plane = tp // tile
    return pl.pallas_call(
        _combine_kernel,
        grid=(n_rows // tile,),
        in_specs=[pl.BlockSpec((tile, d), lambda i: (off + i, 0)),
                  pl.BlockSpec((tile, LANES), lambda i: (off + i, 0)),
                  pl.BlockSpec((tile * ROW_CHUNKS, LANES), lambda i: (off + i, 0)),
                  pl.BlockSpec((tile * ROW_CHUNKS, LANES), lambda i: (plane + off + i, 0))],
        out_specs=pl.BlockSpec((tile, d), lambda i: (i, 0)),
        out_shape=jax.ShapeDtypeStruct((n_rows, d), F32),
        compiler_params=_cparams(("parallel",)),
        name="moe_combine",
    )(x2, gates, y2, y2)


def kernel(x_prompt, x_sample, cache_k, cache_v, state_ssm_re, state_ssm_im, meta_tokens, norm1_g, w_in, b_in, ssm_a_re, ssm_a_im, ssm_log_dt, ssm_b_re, ssm_b_im, ssm_c_re, ssm_c_im, ssm_d, w_glu, b_glu, w_ssm_proj, q_norm_g, k_norm_g, lam_q1, lam_k1, lam_q2, lam_k2, subln_g, w_att_proj, w_o, norm2_g, w_router_group, b_router_group, w_router_expert, b_router_expert, w1_e, w3_e, w2_e):
    assert x_prompt.shape[0] == 1 and w_in.shape[0] == 1
    seq = x_prompt.shape[1]
    nb, nq = x_sample.shape[0], x_sample.shape[1]
    past = cache_k.shape[2]
    n_s = nb * nq
    t_real = seq + N_META + n_s
    tp = -(-t_real // ROW_ALIGN) * ROW_ALIGN
    off_meta, off_s = seq, seq + N_META
    tq = 256
    assert seq % tq == 0 and nq == N_META and past % 512 == 0
    lam_init = 0.8 - 0.6 * math.exp(-0.3 * 0)
    out_scale = 1.0 - lam_init

    x_cat = jnp.concatenate([x_prompt[0], meta_tokens.astype(F32), x_sample.reshape(n_s, D_MODEL),
                             jnp.zeros((tp - t_real, D_MODEL), F32)], axis=0)

    tm = _row_tile(tp, 1088)
    tr = _row_tile(tp, 256)

    h1 = _rmsnorm(x_cat, norm1_g[0], tr)
    z = _inproj(h1, w_in[0], b_in[0], tm, 512)

    pos = jnp.concatenate([N_META + jnp.arange(seq), jnp.arange(N_META),
                           jnp.tile(past + jnp.arange(nq), nb),
                           jnp.zeros((tp - t_real,), jnp.int32)]).astype(F32)
    half = HEAD_DIM // 2
    inv = ROPE_THETA ** (-jnp.arange(half, dtype=F32) / half)
    ang = pos[:, None] * inv[None, :]
    cos_t = jnp.tile(jnp.cos(ang), (1, LANES // half))
    sin_h = jnp.sin(ang)
    sin_t = jnp.tile(jnp.concatenate([-sin_h, sin_h], axis=1), (1, LANES // HEAD_DIM))
    gq = jnp.tile(q_norm_g[0], LANES // HEAD_DIM).reshape(1, LANES)
    gk = jnp.tile(k_norm_g[0], LANES // HEAD_DIM).reshape(1, LANES)
    qb, kf, kb, vf, vb = _qk_rope(z, cos_t, sin_t, gq, gk, tr)

    lam = (jnp.exp(jnp.sum(lam_q1[0] * lam_k1[0])) - jnp.exp(jnp.sum(lam_q2[0] * lam_k2[0])) + lam_init)
    lam_row = jnp.full((1, LANES), lam, F32)
    sg = subln_g[0].reshape(1, LANES)

    q_s = qb[off_s:off_s + n_s].reshape(nb, nq, N_HEADS, LANES).transpose(0, 2, 1, 3)
    lane = jnp.arange(LANES)
    qz = jnp.concatenate([jnp.where(lane < HEAD_DIM, q_s, 0), jnp.where(lane >= HEAD_DIM, q_s, 0)], axis=2)
    pad_new = lambda a: jnp.pad(a[off_s:off_s + n_s].reshape(nb, nq, D_ATT), ((0, 0), (0, LANES - nq), (0, 0)))
    ck = cache_k[0].reshape(nb, past, D_ATT)
    cv = cache_v[0].reshape(nb, past * N_HEADS, V_DIM)

    logit_bound = 8.1 * jnp.max(jnp.abs(q_norm_g[0])) * jnp.max(jnp.abs(k_norm_g[0]))

    def attention(fast):
        def run():
            if fast:
                o = _attn_prompt_fast(qb, kb, vb, lam_row, sg, seq, out_scale, 512)
            else:
                o = _attn_prompt(qb, kb, vb, lam_row, sg, seq, out_scale, tq)
            return _attn_sample(qz, ck, cv, pad_new(kb), pad_new(vb), o, lam_row, sg, off_s, out_scale,
                                512, fast)
        return run

    o_att = lax.cond(logit_bound <= LOGIT_BOUND_MAX, attention(True), attention(False))

    gp = N_SSM_GROUPS * SSM_STATE
    s5w = _s5_weights(ssm_a_re[0], ssm_a_im[0], ssm_log_dt[0], ssm_b_re[0], ssm_b_im[0],
                      ssm_c_re[0], ssm_c_im[0])
    ys, hp_re, hp_im, hs_re, hs_im = _s5(z, s5w, ssm_d[0], state_ssm_re[0].reshape(nb, gp),
                                         state_ssm_im[0].reshape(nb, gp), seq, nb, nq)
    ysg = _glu(ys, w_glu[0], b_glu[0], tm, 512)
    m = _merge(ysg, o_att, w_ssm_proj[0], w_att_proj[0], z, tm, 512)
    x2 = _outproj(m, w_o[0], x_cat, tm, 512)

    w_r = jnp.concatenate([w_router_group[0], w_router_expert[0],
                           jnp.zeros((D_MODEL, LANES - N_EGROUPS - N_EXPERTS), F32)], axis=1).astype(BF16)
    b_r = jnp.concatenate([b_router_group[0], b_router_expert[0],
                           jnp.zeros((LANES - N_EGROUPS - N_EXPERTS,), F32)]).reshape(1, LANES)
    h2, e_sel, g_sel = _router(x2, norm2_g[0], w_r, b_r, tr)

    plan = _route_plan(e_sel[:t_real, :TOP_K], t_real, tp)
    y2 = _experts(h2, plan, t_real, w1_e[0], w3_e[0], w2_e[0])

    def heads(a, lead):
        return a.reshape(lead + (N_HEADS, 2, HEAD_DIM))

    y_prompt = _combine(x2, g_sel, y2, 0, seq, LANES).reshape(1, seq, D_MODEL)
    y_sample = _combine(x2, g_sel, y2, off_s, n_s, nq).reshape(nb, nq, D_MODEL)
    k_p = jnp.concatenate([kf[off_meta:off_meta + N_META], kf[:seq]], axis=0)
    v_p = jnp.concatenate([vf[off_meta:off_meta + N_META], vf[:seq]], axis=0)
    k_prompt = heads(k_p, (1, 1, seq + N_META))
    v_prompt = v_p.reshape(1, 1, seq + N_META, N_HEADS, V_DIM)
    k_sample = heads(kf[off_s:off_s + n_s], (1, nb, nq))
    v_sample = vf[off_s:off_s + n_s].reshape(1, nb, nq, N_HEADS, V_DIM)
    st = lambda a, lead: a.reshape(lead + (N_SSM_GROUPS, SSM_STATE))
    return (y_prompt, y_sample, k_prompt, v_prompt, st(hp_re, (1, 1)), st(hp_im, (1, 1)),
            k_sample, v_sample, st(hs_re, (1, nb)), st(hs_im, (1, nb)))
```

```python
import functools
import math

import jax
import jax.numpy as jnp
from jax import lax
from jax.experimental import pallas as pl
from jax.experimental.pallas import tpu as pltpu

F32 = jnp.float32
BF16 = jnp.bfloat16

D_MODEL = 4096
N_META = 16
CHUNK = 64
N_HEADS = 16
HEAD_DIM = 64
V_DIM = 128
D_ATT = N_HEADS * V_DIM
D_SSM = 2048
SSM_GROUP = 16
N_SSM_GROUPS = D_SSM // SSM_GROUP
SSM_STATE = 64
IN_WIDTH = D_SSM + 3 * D_ATT + 2 * D_MODEL
ROPE_THETA = 10000.0
N_EGROUPS = 8
EXPERTS_PER_GROUP = 8
N_EXPERTS = N_EGROUPS * EXPERTS_PER_GROUP
TOP_K = 2
D_EXPERT = 512
MOE_BLOCK = 320
EXPERT_SLICES = 4
EPS = 1e-6

LANES = 128
ROW_ALIGN = 512
S5_CHUNK = 8
S5_LANE_GROUPS = LANES // SSM_GROUP
S5_TILES = D_SSM // LANES
VMEM_LIMIT = 56 * 1024 * 1024


def _cparams(sem, vmem=VMEM_LIMIT):
    return pltpu.CompilerParams(dimension_semantics=sem, vmem_limit_bytes=vmem)


def _row_tile(tp, cap):
    best = 16
    for t in range(16, cap + 1, 16):
        if tp % t == 0:
            best = t
    return best


def _dot(a, b):
    return jnp.dot(a, b, preferred_element_type=F32)


def _dot_nt(a, b):
    return lax.dot_general(a, b, (((1,), (1,)), ((), ())), preferred_element_type=F32)


def _rmsnorm_kernel(xa_ref, xb_ref, g_ref, o_ref, xc_ref, *, n_a):
    def emit(x):
        ms = jnp.mean(x * x, axis=-1, keepdims=True)
        o_ref[...] = (x * lax.rsqrt(ms + EPS) * g_ref[...]).astype(o_ref.dtype)
        xc_ref[...] = x

    @pl.when(pl.program_id(0) < n_a)
    def _():
        emit(xa_ref[...])

    @pl.when(pl.program_id(0) >= n_a)
    def _():
        emit(xb_ref[...])


def _rmsnorm(x_a, x_b, g, tr):
    d = x_a.shape[1]
    n_a, n_b = x_a.shape[0] // tr, x_b.shape[0] // tr
    tp = x_a.shape[0] + x_b.shape[0]
    row = pl.BlockSpec((tr, d), lambda i: (i, 0))
    return pl.pallas_call(
        functools.partial(_rmsnorm_kernel, n_a=n_a),
        grid=(n_a + n_b,),
        in_specs=[pl.BlockSpec((tr, d), lambda i: (jnp.minimum(i, n_a - 1), 0)),
                  pl.BlockSpec((tr, d), lambda i: (jnp.maximum(i - n_a, 0), 0)),
                  pl.BlockSpec((1, d), lambda i: (0, 0))],
        out_specs=[row, row],
        out_shape=[jax.ShapeDtypeStruct((tp, d), BF16), jax.ShapeDtypeStruct((tp, d), F32)],
        compiler_params=_cparams(("arbitrary",)),
        name="rmsnorm1",
    )(x_a, x_b, g.reshape(1, d))


def _inproj_kernel(x_ref, w_ref, b_ref, o_ref):
    o_ref[...] = _dot(x_ref[...], w_ref[...].astype(BF16)) + b_ref[...]


def _inproj(h, w, b, tm, tn):
    tp, k = h.shape
    n = w.shape[1]
    return pl.pallas_call(
        _inproj_kernel,
        grid=(n // tn, tp // tm),
        in_specs=[pl.BlockSpec((tm, k), lambda j, i: (i, 0)),
                  pl.BlockSpec((k, tn), lambda j, i: (0, j)),
                  pl.BlockSpec((1, tn), lambda j, i: (0, j))],
        out_specs=pl.BlockSpec((tm, tn), lambda j, i: (i, j)),
        out_shape=jax.ShapeDtypeStruct((tp, n), F32),
        compiler_params=_cparams(("parallel", "parallel")),
        name="in_proj",
    )(h, w, b.reshape(1, n))


def _glu_kernel(x_ref, w_ref, b_ref, xe_ref, o_ref):
    a = _dot(x_ref[...].astype(BF16), w_ref[...].astype(BF16)) + b_ref[...]
    o_ref[...] = (xe_ref[...] * jax.nn.sigmoid(a)).astype(o_ref.dtype)


def _glu(ys, w, b, tm, tn):
    tp, k = ys.shape
    n = w.shape[1]
    return pl.pallas_call(
        _glu_kernel,
        grid=(n // tn, tp // tm),
        in_specs=[pl.BlockSpec((tm, k), lambda j, i: (i, 0)),
                  pl.BlockSpec((k, tn), lambda j, i: (0, j)),
                  pl.BlockSpec((1, tn), lambda j, i: (0, j)),
                  pl.BlockSpec((tm, tn), lambda j, i: (i, j))],
        out_specs=pl.BlockSpec((tm, tn), lambda j, i: (i, j)),
        out_shape=jax.ShapeDtypeStruct((tp, n), BF16),
        compiler_params=_cparams(("parallel", "parallel")),
        name="glu",
    )(ys, w, b.reshape(1, n), ys)


def _merge_kernel(ys_ref, oa_ref, ws_ref, wa_ref, gs_ref, ga_ref, o_ref):
    a = _dot(ys_ref[...], ws_ref[...].astype(BF16))
    b = _dot(oa_ref[...], wa_ref[...].astype(BF16))
    m = jax.nn.sigmoid(gs_ref[...]) * a + jax.nn.sigmoid(ga_ref[...]) * b
    o_ref[...] = m.astype(o_ref.dtype)


def _merge(ysg, oatt, w_ssm, w_att, z, tm, tn):
    tp, k = ysg.shape
    n = w_ssm.shape[1]
    gs_blk = (D_SSM + 3 * D_ATT) // tn
    ga_blk = (D_SSM + 3 * D_ATT + D_MODEL) // tn
    return pl.pallas_call(
        _merge_kernel,
        grid=(n // tn, tp // tm),
        in_specs=[pl.BlockSpec((tm, k), lambda j, i: (i, 0)),
                  pl.BlockSpec((tm, k), lambda j, i: (i, 0)),
                  pl.BlockSpec((k, tn), lambda j, i: (0, j)),
                  pl.BlockSpec((k, tn), lambda j, i: (0, j)),
                  pl.BlockSpec((tm, tn), lambda j, i: (i, gs_blk + j)),
                  pl.BlockSpec((tm, tn), lambda j, i: (i, ga_blk + j))],
        out_specs=pl.BlockSpec((tm, tn), lambda j, i: (i, j)),
        out_shape=jax.ShapeDtypeStruct((tp, n), BF16),
        compiler_params=_cparams(("parallel", "parallel")),
        name="merge_proj",
    )(ysg, oatt, w_ssm, w_att, z, z)


def _outproj_kernel(m_ref, w_ref, x_ref, o_ref):
    o_ref[...] = x_ref[...] + _dot(m_ref[...], w_ref[...].astype(BF16))


def _outproj(m, w, x, tm, tn):
    tp, k = m.shape
    n = w.shape[1]
    return pl.pallas_call(
        _outproj_kernel,
        grid=(n // tn, tp // tm),
        in_specs=[pl.BlockSpec((tm, k), lambda j, i: (i, 0)),
                  pl.BlockSpec((k, tn), lambda j, i: (0, j)),
                  pl.BlockSpec((tm, tn), lambda j, i: (i, j))],
        out_specs=pl.BlockSpec((tm, tn), lambda j, i: (i, j)),
        out_shape=jax.ShapeDtypeStruct((tp, n), F32),
        compiler_params=_cparams(("parallel", "parallel")),
        name="out_proj",
    )(m, w, x)


def _segment_sumsq(x, ones_bd):
    x2 = x * x
    hi = x2.astype(BF16)
    lo = (x2 - hi.astype(F32)).astype(BF16)
    return _dot(hi, ones_bd) + _dot(lo, ones_bd)


def _qk_rope_kernel(zq_ref, zk_ref, zv_ref, cos_ref, sin_ref, gq_ref, gk_ref, ones_ref,
                    qb_ref, kf_ref, kb_ref, vf_ref, vb_ref):
    cos = cos_ref[...]
    sin = sin_ref[...]
    ones_bd = ones_ref[...]
    lane = lax.broadcasted_iota(jnp.int32, cos.shape, 1)
    first_half = (lane % HEAD_DIM) < (HEAD_DIM // 2)

    def norm_rope(x, g):
        ss = _segment_sumsq(x, ones_bd)
        xn = x * lax.rsqrt(ss * (1.0 / HEAD_DIM) + EPS) * g
        partner = jnp.where(first_half,
                            pltpu.roll(xn, LANES - HEAD_DIM // 2, 1),
                            pltpu.roll(xn, HEAD_DIM // 2, 1))
        return xn * cos + partner * sin

    for h in range(N_HEADS):
        sl = slice(h * LANES, (h + 1) * LANES)
        q = norm_rope(zq_ref[:, sl], gq_ref[...])
        qb_ref[:, sl] = (q * (HEAD_DIM ** -0.5)).astype(BF16)
        k = norm_rope(zk_ref[:, sl], gk_ref[...])
        kf_ref[:, sl] = k
        kb_ref[:, sl] = k.astype(BF16)
    v = zv_ref[...]
    vf_ref[...] = v
    vb_ref[...] = v.astype(BF16)


def _qk_rope(z, cos_t, sin_t, gq, gk, tr):
    tp = z.shape[0]
    ones_bd = jnp.kron(jnp.eye(LANES // HEAD_DIM, dtype=F32),
                       jnp.ones((HEAD_DIM, HEAD_DIM), F32)).astype(BF16)
    zspec = lambda c: pl.BlockSpec((tr, D_ATT), lambda i: (i, c))
    row = pl.BlockSpec((tr, LANES), lambda i: (i, 0))
    const = pl.BlockSpec((1, LANES), lambda i: (0, 0))
    out = pl.BlockSpec((tr, D_ATT), lambda i: (i, 0))
    q_blk = D_SSM // D_ATT
    return pl.pallas_call(
        _qk_rope_kernel,
        grid=(tp // tr,),
        in_specs=[zspec(q_blk), zspec(q_blk + 1), zspec(q_blk + 2), row, row, const, const,
                  pl.BlockSpec((LANES, LANES), lambda i: (0, 0))],
        out_specs=[out, out, out, out, out],
        out_shape=[jax.ShapeDtypeStruct((tp, D_ATT), BF16),
                   jax.ShapeDtypeStruct((tp, D_ATT), F32),
                   jax.ShapeDtypeStruct((tp, D_ATT), BF16),
                   jax.ShapeDtypeStruct((tp, D_ATT), F32),
                   jax.ShapeDtypeStruct((tp, D_ATT), BF16)],
        compiler_params=_cparams(("parallel",)),
        name="qk_norm_rope",
    )(z, z, z, cos_t, sin_t, gq, gk, ones_bd)


def _softmax_step(c, qc, kt, vt, mask, m_ref, l_ref, acc_ref):
    s = _dot_nt(qc, kt)
    if mask is not None:
        s = jnp.where(mask, s, -jnp.inf)
    m_prev = m_ref[c]
    m_new = jnp.maximum(m_prev, jnp.max(s, axis=1, keepdims=True))
    alpha = jnp.exp(m_prev - m_new)
    p = jnp.exp(s - m_new[:, :1])
    l_ref[c] = alpha * l_ref[c] + jnp.sum(p, axis=1, keepdims=True)
    acc_ref[c] = alpha * acc_ref[c] + _dot(p.astype(BF16), vt)
    m_ref[c] = m_new


def _diff_finish(o0, o1, lam, g, out_scale):
    o = o0 - lam * o1
    ms = jnp.mean(o * o, axis=-1, keepdims=True)
    return o * lax.rsqrt(ms + EPS) * g * out_scale


def _split_components(q):
    lane = lax.broadcasted_iota(jnp.int32, q.shape, 1)
    zero = jnp.zeros_like(q)
    return jnp.where(lane < HEAD_DIM, q, zero), jnp.where(lane >= HEAD_DIM, q, zero)


def _attn_prompt_kernel(lam_ref, g_ref, q_ref, k_ref, v_ref, o_ref, m_ref, l_ref, acc_ref,
                        *, tq, nq_main, seq, out_scale):
    i = pl.program_id(1)
    q0, q1 = _split_components(q_ref[...])
    m_ref[...] = jnp.full(m_ref.shape, -jnp.inf, F32)
    l_ref[...] = jnp.zeros(l_ref.shape, F32)
    acc_ref[...] = jnp.zeros(acc_ref.shape, F32)

    def update(kt, vt, mask):
        _softmax_step(0, q0, kt, vt, mask, m_ref, l_ref, acc_ref)
        _softmax_step(1, q1, kt, vt, mask, m_ref, l_ref, acc_ref)

    col = lax.broadcasted_iota(jnp.int32, (tq, LANES), 1)
    update(k_ref[pl.ds(seq, LANES), :], v_ref[pl.ds(seq, LANES), :], col < N_META)

    is_main = i < nq_main

    def body(j, carry):
        start = pl.multiple_of(j * tq, tq)
        update(k_ref[pl.ds(start, tq), :], v_ref[pl.ds(start, tq), :], None)
        return carry

    lax.fori_loop(0, jnp.where(is_main, i, 0), body, 0)

    @pl.when(is_main)
    def _():
        start = pl.multiple_of(i * tq, tq)
        r = lax.broadcasted_iota(jnp.int32, (tq, tq), 0) // CHUNK
        c = lax.broadcasted_iota(jnp.int32, (tq, tq), 1) // CHUNK
        update(k_ref[pl.ds(start, tq), :], v_ref[pl.ds(start, tq), :], c <= r)

    o0 = acc_ref[0] / l_ref[0]
    o1 = acc_ref[1] / l_ref[1]
    o_ref[...] = _diff_finish(o0, o1, lam_ref[...], g_ref[...], out_scale).astype(o_ref.dtype)


def _attn_prompt(qb, kb, vb, lam_row, subln_g, seq, out_scale, tq):
    tp = qb.shape[0]
    kern = functools.partial(_attn_prompt_kernel, tq=tq, nq_main=seq // tq, seq=seq,
                             out_scale=out_scale)
    const = pl.BlockSpec((1, LANES), lambda h, i: (0, 0))
    return pl.pallas_call(
        kern,
        grid=(N_HEADS, tp // tq),
        in_specs=[const, const,
                  pl.BlockSpec((tq, LANES), lambda h, i: (i, h)),
                  pl.BlockSpec((tp, LANES), lambda h, i: (0, h)),
                  pl.BlockSpec((tp, LANES), lambda h, i: (0, h))],
        out_specs=pl.BlockSpec((tq, LANES), lambda h, i: (i, h)),
        out_shape=jax.ShapeDtypeStruct((tp, D_ATT), BF16),
        scratch_shapes=[pltpu.VMEM((2, tq, LANES), F32),
                        pltpu.VMEM((2, tq, LANES), F32),
                        pltpu.VMEM((2, tq, LANES), F32)],
        compiler_params=_cparams(("parallel", "parallel")),
        name="attn_prompt",
    )(lam_row, subln_g, qb, kb, vb)


LOGIT_BOUND_MAX = 40.0
ATTN_WIDE = 4


def _with_ones(vt):
    return jnp.concatenate([vt, jnp.ones(vt.shape, vt.dtype)], axis=1)


def _cache_v_head(vc_ref, h):
    tk = vc_ref.shape[0] // N_HEADS
    return vc_ref[pl.ds(h, tk, stride=N_HEADS), :].astype(BF16)


def _attn_prompt_fast_kernel(lam_ref, g_ref, q_ref, k_ref, v_ref, o_ref, acc_ref,
                             *, tq, nq_main, seq, out_scale):
    i = pl.program_id(1)
    q0, q1 = _split_components(q_ref[...])
    qq = jnp.concatenate([q0, q1], axis=0)

    def scores(start, rows):
        return _dot_nt(qq, k_ref[pl.ds(start, rows), :])

    def weighted(s, start, rows, mask):
        p = jnp.exp(s)
        if mask is not None:
            p = jnp.where(mask, p, 0.0)
        return _dot(p.astype(BF16), _with_ones(v_ref[pl.ds(start, rows), :]))

    def tile_pv(start, rows, mask):
        return weighted(scores(start, rows), start, rows, mask)

    col = lax.broadcasted_iota(jnp.int32, (2 * tq, LANES), 1)
    acc_ref[...] = tile_pv(seq, LANES, col < N_META)

    is_main = i < nq_main
    n_full = jnp.where(is_main, i, 0)

    wide = ATTN_WIDE

    def body(j, carry):
        start = pl.multiple_of(wide * j * tq, tq)
        acc_ref[...] += tile_pv(start, wide * tq, None)
        return carry

    n_wide = n_full // wide
    lax.fori_loop(0, n_wide, body, 0)
    rem = n_full - wide * n_wide
    base = wide * n_wide

    @pl.when(rem >= 2)
    def _():
        acc_ref[...] += tile_pv(pl.multiple_of(base * tq, tq), 2 * tq, None)

    @pl.when(rem % 2 == 1)
    def _():
        acc_ref[...] += tile_pv(pl.multiple_of((n_full - 1) * tq, tq), tq, None)

    @pl.when(is_main)
    def _():
        r = (lax.broadcasted_iota(jnp.int32, (2 * tq, tq), 0) % tq) // CHUNK
        c = lax.broadcasted_iota(jnp.int32, (2 * tq, tq), 1) // CHUNK
        acc_ref[...] += tile_pv(pl.multiple_of(i * tq, tq), tq, c <= r)

    acc = acc_ref[...]
    o0 = acc[:tq, :LANES] / acc[:tq, LANES:]
    o1 = acc[tq:, :LANES] / acc[tq:, LANES:]
    o_ref[...] = _diff_finish(o0, o1, lam_ref[...], g_ref[...], out_scale).astype(o_ref.dtype)


def _attn_prompt_fast(qb, kb, vb, lam_row, subln_g, seq, out_scale, tq):
    tp = qb.shape[0]
    kern = functools.partial(_attn_prompt_fast_kernel, tq=tq, nq_main=seq // tq, seq=seq,
                             out_scale=out_scale)
    const = pl.BlockSpec((1, LANES), lambda h, i: (0, 0))
    return pl.pallas_call(
        kern,
        grid=(N_HEADS, tp // tq),
        in_specs=[const, const,
                  pl.BlockSpec((tq, LANES), lambda h, i: (i, h)),
                  pl.BlockSpec((tp, LANES), lambda h, i: (0, h)),
                  pl.BlockSpec((tp, LANES), lambda h, i: (0, h))],
        out_specs=pl.BlockSpec((tq, LANES), lambda h, i: (i, h)),
        out_shape=jax.ShapeDtypeStruct((tp, D_ATT), BF16),
        scratch_shapes=[pltpu.VMEM((2 * tq, 2 * LANES), F32)],
        compiler_params=_cparams(("parallel", "parallel")),
        name="attn_prompt_fast",
    )(lam_row, subln_g, qb, kb, vb)


def _attn_sample_fast_kernel(lam_ref, g_ref, q_ref, kc_ref, vc_ref, kn_ref, vn_ref, o_in_ref, o_ref,
                             acc_ref, *, nq, out_scale):
    del o_in_ref
    j = pl.program_id(1)
    rows = 2 * nq

    def head_update(h, kt, vt, mask):
        p = jnp.exp(_dot_nt(q_ref[h], kt))
        if mask is not None:
            p = jnp.where(mask, p, 0.0)
        acc_ref[h] += _dot(p.astype(BF16), _with_ones(vt))

    @pl.when(j == 0)
    def _():
        acc_ref[...] = jnp.zeros(acc_ref.shape, F32)
        col = lax.broadcasted_iota(jnp.int32, (rows, LANES), 1)
        for h in range(N_HEADS):
            sl = slice(h * LANES, (h + 1) * LANES)
            head_update(h, kn_ref[:, sl], vn_ref[:, sl], col < nq)

    for h in range(N_HEADS):
        sl = slice(h * LANES, (h + 1) * LANES)
        head_update(h, kc_ref[:, sl].astype(BF16), _cache_v_head(vc_ref, h), None)

    @pl.when(j == pl.num_programs(1) - 1)
    def _():
        for h in range(N_HEADS):
            acc = acc_ref[h]
            o = acc[:, :LANES] / acc[:, LANES:]
            res = _diff_finish(o[:nq], o[nq:], lam_ref[...], g_ref[...], out_scale)
            o_ref[:, h * LANES:(h + 1) * LANES] = res.astype(o_ref.dtype)


def _attn_sample_kernel(lam_ref, g_ref, q_ref, kc_ref, vc_ref, kn_ref, vn_ref, o_in_ref, o_ref,
                        m_ref, l_ref, acc_ref, *, nq, out_scale):
    del o_in_ref
    j = pl.program_id(1)
    rows = 2 * nq

    @pl.when(j == 0)
    def _():
        m_ref[...] = jnp.full(m_ref.shape, -jnp.inf, F32)
        l_ref[...] = jnp.zeros(l_ref.shape, F32)
        acc_ref[...] = jnp.zeros(acc_ref.shape, F32)
        col = lax.broadcasted_iota(jnp.int32, (rows, LANES), 1)
        for h in range(N_HEADS):
            sl = slice(h * LANES, (h + 1) * LANES)
            _softmax_step(h, q_ref[h], kn_ref[:, sl], vn_ref[:, sl], col < nq,
                          m_ref, l_ref, acc_ref)

    for h in range(N_HEADS):
        sl = slice(h * LANES, (h + 1) * LANES)
        _softmax_step(h, q_ref[h], kc_ref[:, sl].astype(BF16), _cache_v_head(vc_ref, h), None,
                      m_ref, l_ref, acc_ref)

    @pl.when(j == pl.num_programs(1) - 1)
    def _():
        for h in range(N_HEADS):
            o = acc_ref[h] / l_ref[h]
            res = _diff_finish(o[:nq], o[nq:], lam_ref[...], g_ref[...], out_scale)
            o_ref[:, h * LANES:(h + 1) * LANES] = res.astype(o_ref.dtype)


def _attn_sample(qz, cache_k, cache_v, k_new, v_new, o_buf, lam_row, subln_g, row_off, out_scale, tk, fast):
    nb, past = cache_k.shape[0], cache_k.shape[1]
    nq = qz.shape[2] // 2
    if fast:
        kern = functools.partial(_attn_sample_fast_kernel, nq=nq, out_scale=out_scale)
        scratch = [pltpu.VMEM((N_HEADS, 2 * nq, 2 * LANES), F32)]
    else:
        kern = functools.partial(_attn_sample_kernel, nq=nq, out_scale=out_scale)
        scratch = [pltpu.VMEM((N_HEADS, 2 * nq, LANES), F32)] * 3
    const = pl.BlockSpec((1, LANES), lambda b, j: (0, 0))
    blk_off = row_off // nq
    return pl.pallas_call(
        kern,
        grid=(nb, past // tk),
        in_specs=[const, const,
                  pl.BlockSpec((None, N_HEADS, 2 * nq, LANES), lambda b, j: (b, 0, 0, 0)),
                  pl.BlockSpec((None, tk, D_ATT), lambda b, j: (b, j, 0)),
                  pl.BlockSpec((None, tk * N_HEADS, V_DIM), lambda b, j: (b, j, 0)),
                  pl.BlockSpec((None, LANES, D_ATT), lambda b, j: (b, 0, 0)),
                  pl.BlockSpec((None, LANES, D_ATT), lambda b, j: (b, 0, 0)),
                  pl.BlockSpec(memory_space=pl.ANY)],
        out_specs=pl.BlockSpec((nq, D_ATT), lambda b, j: (blk_off + b, 0)),
        out_shape=jax.ShapeDtypeStruct(o_buf.shape, o_buf.dtype),
        scratch_shapes=scratch,
        input_output_aliases={7: 0},
        compiler_params=_cparams(("parallel", "arbitrary")),
        name="attn_sample_fast" if fast else "attn_sample",
    )(lam_row, subln_g, qz, cache_k, cache_v, k_new, v_new, o_buf)


def _s5_weights(a_re, a_im, log_dt, b_re, b_im, c_re, c_im):
    hp = lax.Precision.HIGHEST
    n_t, gl, tc = S5_TILES, S5_LANE_GROUPS, S5_CHUNK
    dt = jnp.exp(log_dt)[:, None]
    mag = jnp.exp(a_re * dt)
    abar_re = mag * jnp.cos(a_im * dt)
    abar_im = mag * jnp.sin(a_im * dt)
    nr, ni = abar_re - 1.0, abar_im
    den = a_re * a_re + a_im * a_im
    coef_re = (nr * a_re + ni * a_im) / den
    coef_im = (ni * a_re - nr * a_im) / den
    bbar_re = coef_re[..., None] * b_re - coef_im[..., None] * b_im
    bbar_im = coef_re[..., None] * b_im + coef_im[..., None] * b_re
    n = jnp.arange(tc + 1, dtype=F32)[:, None, None]
    pw_mag = jnp.exp(n * (a_re * dt))
    pw_re = pw_mag * jnp.cos(n * (a_im * dt))
    pw_im = pw_mag * jnp.sin(n * (a_im * dt))
    e_re = pw_re[:tc, :, :, None] * bbar_re - pw_im[:tc, :, :, None] * bbar_im
    e_im = pw_re[:tc, :, :, None] * bbar_im + pw_im[:tc, :, :, None] * bbar_re
    kern = (jnp.einsum('gcp,lgpd->glcd', c_re, e_re, precision=hp)
            - jnp.einsum('gcp,lgpd->glcd', c_im, e_im, precision=hp))
    eye = jnp.eye(gl, dtype=F32)
    w_intra = jnp.einsum('jglcd,gh->jlgdhc', kern.reshape(n_t, gl, tc, SSM_GROUP, SSM_GROUP), eye)
    w_intra = w_intra.reshape(n_t, tc, LANES, LANES)
    eb = jnp.stack([e_re[::-1], e_im[::-1]], 0)
    eb = eb.reshape(2, tc, n_t, gl, SSM_STATE, SSM_GROUP)
    w_state = eb.transpose(2, 1, 3, 5, 0, 4).reshape(n_t, tc, LANES, 2 * SSM_STATE)
    cp_re = c_re[None] * pw_re[1:, :, None, :] - c_im[None] * pw_im[1:, :, None, :]
    cp_im = c_re[None] * pw_im[1:, :, None, :] + c_im[None] * pw_re[1:, :, None, :]
    cp = jnp.stack([cp_re, -cp_im], 0).reshape(2, tc, n_t, gl, SSM_GROUP, SSM_STATE)
    w_read = cp.transpose(2, 1, 0, 5, 3, 4).reshape(n_t, tc, 2 * SSM_STATE, LANES)
    half = gl * SSM_STATE
    a_pow = jnp.concatenate([pw_re[tc].reshape(n_t, 1, half), pw_im[tc].reshape(n_t, 1, half)], -1)
    rp = jnp.arange(2 * SSM_STATE)
    col = jnp.arange(2 * half)
    spread = ((rp[:, None] // SSM_STATE == col[None, :] // half)
              & (rp[:, None] % SSM_STATE == col[None, :] % SSM_STATE)).astype(BF16)
    return (w_intra.astype(BF16), w_state.astype(BF16), w_read.astype(BF16), a_pow, spread, spread.T)


def _s5_kernel(u_ref, wi_ref, wsc_ref, wrc_ref, ap_ref, sp_ref, spt_ref, d_ref, h0r_ref, h0i_ref,
               y_ref, hpr_ref, hpi_ref, hsr_ref, hsi_ref,
               y_acc, v_ref, hs_ref, wt_ref, ws_ref, wr_ref,
               *, nc, n_main, n_meta_chunks, n_seq, seq_chunks):
    tc = S5_CHUNK
    half = hs_ref.shape[1] // 2
    grp_r = lax.broadcasted_iota(jnp.int32, (LANES, 2 * half), 0) // SSM_GROUP
    grp_c = (lax.broadcasted_iota(jnp.int32, (LANES, 2 * half), 1) % half) // SSM_STATE
    for s in range(tc):
        full = _dot(wsc_ref[s], sp_ref[...])
        ws_ref[s * LANES:(s + 1) * LANES, :] = jnp.where(grp_r == grp_c, full, 0.0).astype(BF16)
    grp_r = (lax.broadcasted_iota(jnp.int32, (2 * half, LANES), 0) % half) // SSM_STATE
    grp_c = lax.broadcasted_iota(jnp.int32, (2 * half, LANES), 1) // SSM_GROUP
    for t in range(tc):
        full = _dot(spt_ref[...], wrc_ref[t])
        wr_ref[:, t * LANES:(t + 1) * LANES] = jnp.where(grp_r == grp_c, full, 0.0).astype(BF16)
    for s in range(tc):
        for t in range(tc):
            blk = wi_ref[t - s] if t >= s else jnp.zeros((LANES, LANES), BF16)
            wt_ref[s * LANES:(s + 1) * LANES, t * LANES:(t + 1) * LANES] = blk
    lhs = jnp.concatenate(
        [u_ref[pl.ds(s, nc, stride=tc), :].astype(BF16) for s in range(tc)], axis=1)
    y_acc[...] = _dot(lhs, wt_ref[...])
    v_ref[...] = _dot(lhs, ws_ref[...])
    a_re = ap_ref[:, :half]
    a_im = ap_ref[:, half:]

    def advance(h_re, h_im, v):
        return (a_re * h_re - a_im * h_im + v[:, :half],
                a_re * h_im + a_im * h_re + v[:, half:])

    hs_ref[...] = jnp.zeros(hs_ref.shape, F32)

    h_re = jnp.zeros((1, half), F32)
    h_im = jnp.zeros((1, half), F32)
    for c in range(n_main, n_main + n_meta_chunks):
        hs_ref[pl.ds(c, 1), :] = jnp.concatenate([h_re, h_im], axis=1)
        h_re, h_im = advance(h_re, h_im, v_ref[pl.ds(c, 1), :])

    def body(c, carry):
        h_re, h_im = carry
        hs_ref[pl.ds(c, 1), :] = jnp.concatenate([h_re, h_im], axis=1)
        return advance(h_re, h_im, v_ref[pl.ds(c, 1), :])

    h_re, h_im = lax.fori_loop(0, n_main, body, (h_re, h_im), unroll=4)
    hpr_ref[...] = h_re
    hpi_ref[...] = h_im

    base = n_main + n_meta_chunks
    for b in range(n_seq):
        s_re = h0r_ref[pl.ds(b, 1), :]
        s_im = h0i_ref[pl.ds(b, 1), :]
        for c in range(base + b * seq_chunks, base + (b + 1) * seq_chunks):
            hs_ref[pl.ds(c, 1), :] = jnp.concatenate([s_re, s_im], axis=1)
            s_re, s_im = advance(s_re, s_im, v_ref[pl.ds(c, 1), :])
        hsr_ref[pl.ds(b, 1), :] = s_re
        hsi_ref[pl.ds(b, 1), :] = s_im

    y_acc[...] += _dot(hs_ref[...].astype(BF16), wr_ref[...])
    d = d_ref[...]
    for t in range(tc):
        rows = pl.ds(t, nc, stride=tc)
        y = y_acc[:, t * LANES:(t + 1) * LANES] + d * u_ref[rows, :]
        y_ref[rows, :] = jax.nn.gelu(y)


def _s5(z, weights, d_skip, h0_re, h0_im, seq, n_seq, seq_len):
    tp = z.shape[0]
    tc = S5_CHUNK
    nc = tp // tc
    w_intra, w_state, w_read, a_pow, spread, spread_t = weights
    half = S5_LANE_GROUPS * SSM_STATE
    whole = lambda a: pl.BlockSpec(a.shape, lambda j: (0,) * a.ndim)
    kern = functools.partial(_s5_kernel, nc=nc, n_main=seq // tc, n_meta_chunks=N_META // tc,
                             n_seq=n_seq, seq_chunks=seq_len // tc)
    wspec = lambda a: pl.BlockSpec((None,) + a.shape[1:], lambda j: (j,) + (0,) * (a.ndim - 1))
    col = pl.BlockSpec((tp, LANES), lambda j: (0, j))
    st = lambda r: pl.BlockSpec((r, half), lambda j: (0, j))
    gp = N_SSM_GROUPS * SSM_STATE
    return pl.pallas_call(
        kern,
        grid=(S5_TILES,),
        in_specs=[col, wspec(w_intra), wspec(w_state), wspec(w_read), wspec(a_pow),
                  whole(spread), whole(spread_t),
                  pl.BlockSpec((1, LANES), lambda j: (0, j)), st(n_seq), st(n_seq)],
        out_specs=[col, st(1), st(1), st(n_seq), st(n_seq)],
        out_shape=[jax.ShapeDtypeStruct((tp, D_SSM), F32),
                   jax.ShapeDtypeStruct((1, gp), F32),
                   jax.ShapeDtypeStruct((1, gp), F32),
                   jax.ShapeDtypeStruct((n_seq, gp), F32),
                   jax.ShapeDtypeStruct((n_seq, gp), F32)],
        scratch_shapes=[pltpu.VMEM((nc, tc * LANES), F32),
                        pltpu.VMEM((nc, 2 * half), F32),
                        pltpu.VMEM((nc, 2 * half), F32),
                        pltpu.VMEM((tc * LANES, tc * LANES), BF16),
                        pltpu.VMEM((tc * LANES, 2 * half), BF16),
                        pltpu.VMEM((2 * half, tc * LANES), BF16)],
        compiler_params=_cparams(("parallel",)),
        name="s5_scan",
    )(z, w_intra, w_state, w_read, a_pow, spread, spread_t, d_skip.reshape(1, D_SSM), h0_re, h0_im)


ROW_CHUNKS = D_MODEL // LANES
TOKEN_PITCH = ROW_CHUNKS + 1


def _store_token_major(ref, x, spare_too=False):
    n = x.shape[0]
    for c in range(ROW_CHUNKS):
        ref[pl.ds(c, n, stride=TOKEN_PITCH), :] = x[:, c * LANES:(c + 1) * LANES].astype(ref.dtype)
    if spare_too:
        ref[pl.ds(ROW_CHUNKS, n, stride=TOKEN_PITCH), :] = jnp.zeros((n, LANES), ref.dtype)


def _load_token_major(ref, n, dtype, chunks=range(ROW_CHUNKS)):
    return jnp.concatenate([ref[pl.ds(c, n, stride=TOKEN_PITCH), :].astype(dtype) for c in chunks],
                           axis=1)


def _router_kernel(x_ref, g_ref, w_ref, b_ref, h_ref, e_ref, gate_ref):
    x = x_ref[...]
    ms = jnp.mean(x * x, axis=-1, keepdims=True)
    h = x * lax.rsqrt(ms + EPS) * g_ref[...]
    _store_token_major(h_ref, h, spare_too=True)
    logits = _dot(h.astype(BF16), w_ref[...]) + b_ref[...]
    lane = lax.broadcasted_iota(jnp.int32, logits.shape, 1)
    neg = -jnp.inf
    big = jnp.int32(LANES)

    def first_argmax(vals, vmax):
        return jnp.min(jnp.where(vals == vmax, lane, big), axis=1, keepdims=True)

    lg = jnp.where(lane < N_EGROUPS, logits, neg)
    mg = jnp.max(lg, axis=1, keepdims=True)
    sg = jnp.sum(jnp.exp(lg - mg), axis=1, keepdims=True)
    g_w = 1.0 / sg
    g_idx = first_argmax(lg, mg)
    lo = N_EGROUPS + EXPERTS_PER_GROUP * g_idx
    le = jnp.where((lane >= lo) & (lane < lo + EXPERTS_PER_GROUP), logits, neg)
    m1 = jnp.max(le, axis=1, keepdims=True)
    se = jnp.sum(jnp.exp(le - m1), axis=1, keepdims=True)
    i1 = first_argmax(le, m1)
    le2 = jnp.where(lane == i1, neg, le)
    m2 = jnp.max(le2, axis=1, keepdims=True)
    i2 = first_argmax(le2, m2)
    p1 = 1.0 / se
    p2 = jnp.exp(m2 - m1) / se
    tot = p1 + p2
    w1 = g_w * (p1 / tot)
    w2 = g_w * (p2 / tot)
    e_ref[...] = jnp.where(lane == 0, i1 - N_EGROUPS, jnp.where(lane == 1, i2 - N_EGROUPS, 0))
    gate_ref[...] = jnp.where(lane == 0, w1, jnp.where(lane == 1, w2, 0.0))


def _router(x2, g, w_r, b_r, tr):
    tp, d = x2.shape
    return pl.pallas_call(
        _router_kernel,
        grid=(tp // tr,),
        in_specs=[pl.BlockSpec((tr, d), lambda i: (i, 0)),
                  pl.BlockSpec((1, d), lambda i: (0, 0)),
                  pl.BlockSpec((d, LANES), lambda i: (0, 0)),
                  pl.BlockSpec((1, LANES), lambda i: (0, 0))],
        out_specs=[pl.BlockSpec((tr * TOKEN_PITCH, LANES), lambda i: (i, 0)),
                   pl.BlockSpec((tr, LANES), lambda i: (i, 0)),
                   pl.BlockSpec((tr, LANES), lambda i: (i, 0))],
        out_shape=[jax.ShapeDtypeStruct((tp * TOKEN_PITCH, LANES), F32),
                   jax.ShapeDtypeStruct((tp, LANES), jnp.int32),
                   jax.ShapeDtypeStruct((tp, LANES), F32)],
        compiler_params=_cparams(("parallel",)),
        name="norm2_router",
    )(x2, g.reshape(1, d), w_r, b_r)


def _expert_kernel(be_ref, nu_ref, first_ref, cnt_ref, tok_ref, dst_ref, h_hbm, w1_ref, w3_ref, w2_ref,
                   y_hbm, xbuf, xb16, acc_ref, ybuf, gsem, ssem, *, nb, plane_rows, plane_pad):
    del be_ref
    b = pl.program_id(0)
    hh = pl.program_id(1)
    last = pl.num_programs(1) - 1
    n_used = nu_ref[0]
    active = b < n_used
    slot = b % 2
    rows = xb16.shape[1]
    rc = ROW_CHUNKS
    pitch = TOKEN_PITCH
    spare_row = TOP_K * plane_rows

    def token_rows(ref, t):
        return ref.at[pl.ds(t * pitch, pitch), :]

    def start_gather(blk, s):
        base = first_ref[blk]
        for r in range(rows):
            pltpu.make_async_copy(token_rows(h_hbm, tok_ref[base + r]),
                                  xbuf.at[s, pl.ds(r * pitch, pitch), :], gsem.at[s]).start()

    def all_rows_gathered(s):
        return pltpu.make_async_copy(h_hbm.at[pl.ds(0, rows * pitch), :], xbuf.at[s], gsem.at[s])

    def all_rows_scattered():
        return pltpu.make_async_copy(ybuf, y_hbm.at[pl.ds(spare_row * pitch, rows * pitch), :], ssem)

    @pl.when((b == 0) & (hh == 0))
    def _():
        ybuf[...] = jnp.zeros(ybuf.shape, F32)
        fills = [all_rows_scattered()]
        if plane_pad:
            fills += [pltpu.make_async_copy(
                ybuf.at[pl.ds(0, plane_pad * pitch), :],
                y_hbm.at[pl.ds(((k + 1) * plane_rows - plane_pad) * pitch, plane_pad * pitch), :], ssem)
                for k in range(TOP_K)]
        for f in fills:
            f.start()
        for f in fills:
            f.wait()

        @pl.when(active)
        def _():
            start_gather(0, 0)

    @pl.when(active & (hh == 0))
    def _():
        all_rows_gathered(slot).wait()
        per = rc // EXPERT_SLICES
        for q in range(EXPERT_SLICES):
            xb16[q] = _load_token_major(xbuf.at[slot], rows, BF16, range(q * per, (q + 1) * per))

    @pl.when(active & (hh == 1) & (b + 1 < n_used))
    def _():
        start_gather(jnp.minimum(b + 1, nb - 1), 1 - slot)

    @pl.when(active)
    def _():
        w13 = jnp.concatenate([w1_ref[...].astype(BF16), w3_ref[...].astype(BF16)], axis=1)
        part = _dot(xb16[_expert_slice(b, hh)], w13)

        @pl.when(hh == 0)
        def _():
            acc_ref[...] = part

        @pl.when(hh > 0)
        def _():
            acc_ref[...] += part

    @pl.when(active & (hh == last))
    def _():
        @pl.when(b > 0)
        def _():
            all_rows_scattered().wait()

        ac = acc_ref[...]
        hid = (jax.nn.silu(ac[:, :D_EXPERT]) * ac[:, D_EXPERT:]).astype(BF16)
        _store_token_major(ybuf, _dot(hid, w2_ref[...].astype(BF16)))
        base = first_ref[b]
        n_real = cnt_ref[b]
        for r in range(rows):
            dst = jnp.where(r < n_real, dst_ref[base + r], spare_row + r)
            pltpu.make_async_copy(ybuf.at[pl.ds(r * pitch, pitch), :], token_rows(y_hbm, dst), ssem).start()

        @pl.when(b + 1 >= n_used)
        def _():
            all_rows_scattered().wait()


def _expert_slice(b, hh):
    return jnp.where(b % 2 == 0, hh, EXPERT_SLICES - 1 - hh)


def _experts(h2, plan, t_real, w1, w3, w2):
    tp = h2.shape[0] // TOKEN_PITCH
    assert tp - t_real <= MOE_BLOCK
    block_expert, n_used, first, cnt, tok_sorted, dst_sorted = plan
    d = D_MODEL
    nb = block_expert.shape[0]
    dk = d // EXPERT_SLICES

    def eidx(b, be, nu):
        return be[jnp.minimum(b, nu[0] - 1)]

    def sidx(b, hh, nu):
        live = b < nu[0]
        return _expert_slice(jnp.minimum(b, nu[0] - 1), jnp.where(live, hh, EXPERT_SLICES - 1))

    grid_spec = pltpu.PrefetchScalarGridSpec(
        num_scalar_prefetch=6,
        grid=(nb, EXPERT_SLICES),
        in_specs=[pl.BlockSpec(memory_space=pl.ANY),
                  pl.BlockSpec((None, dk, D_EXPERT), lambda b, hh, be, nu, *_: (eidx(b, be, nu), sidx(b, hh, nu), 0)),
                  pl.BlockSpec((None, dk, D_EXPERT), lambda b, hh, be, nu, *_: (eidx(b, be, nu), sidx(b, hh, nu), 0)),
                  pl.BlockSpec((None, D_EXPERT, d), lambda b, hh, be, nu, *_: (eidx(b, be, nu), 0, 0))],
        out_specs=pl.BlockSpec(memory_space=pl.ANY),
        scratch_shapes=[pltpu.VMEM((2, MOE_BLOCK * TOKEN_PITCH, LANES), F32),
                        pltpu.VMEM((EXPERT_SLICES, MOE_BLOCK, dk), BF16),
                        pltpu.VMEM((MOE_BLOCK, 2 * D_EXPERT), F32),
                        pltpu.VMEM((MOE_BLOCK * TOKEN_PITCH, LANES), F32),
                        pltpu.SemaphoreType.DMA((2,)),
                        pltpu.SemaphoreType.DMA(())],
    )
    return pl.pallas_call(
        functools.partial(_expert_kernel, nb=nb, plane_rows=tp, plane_pad=tp - t_real),
        grid_spec=grid_spec,
        out_shape=jax.ShapeDtypeStruct(((TOP_K * tp + MOE_BLOCK) * TOKEN_PITCH, LANES), F32),
        compiler_params=_cparams(("arbitrary", "arbitrary"), 60 * 1024 * 1024),
        name="expert_mlp",
    )(block_expert, n_used, first, cnt, tok_sorted, dst_sorted, h2, w1, w3, w2)


def _route_plan(expert, t_real, tp):
    s = t_real * TOP_K
    n_blocks = -(-(s + N_EXPERTS * (MOE_BLOCK - 1)) // MOE_BLOCK)
    flat_e = expert.reshape(-1).astype(jnp.int32)
    se, order = lax.sort((flat_e, jnp.arange(s, dtype=jnp.int32)), num_keys=1, is_stable=True)
    bounds = jnp.searchsorted(se, jnp.arange(N_EXPERTS + 1, dtype=jnp.int32)).astype(jnp.int32)
    start = bounds[:-1]
    counts = bounds[1:] - start
    padded = (counts + MOE_BLOCK - 1) // MOE_BLOCK * MOE_BLOCK
    pad_end = jnp.cumsum(padded)
    pad_start = pad_end - padded
    block_start = jnp.arange(n_blocks, dtype=jnp.int32) * MOE_BLOCK
    block_expert = jnp.minimum(jnp.searchsorted(pad_end, block_start, side='right'),
                               N_EXPERTS - 1).astype(jnp.int32)
    n_used = (pad_end[-1] // MOE_BLOCK).astype(jnp.int32).reshape(1)
    cnt = jnp.clip(counts[block_expert] - (block_start - pad_start[block_expert]), 0, MOE_BLOCK)
    first = jnp.clip(start[block_expert] + block_start - pad_start[block_expert], 0, s).astype(jnp.int32)
    tail = jnp.zeros((MOE_BLOCK,), jnp.int32)
    tok = order // TOP_K
    tok_sorted = jnp.concatenate([tok, tail])
    dst_sorted = jnp.concatenate([(order % TOP_K) * tp + tok, tail])
    return block_expert, n_used, first, cnt.astype(jnp.int32), tok_sorted, dst_sorted


def _combine_kernel(x_ref, g_ref, y0_ref, y1_ref, o_ref):
    n = x_ref.shape[0]
    g = g_ref[...]
    y = (g[:, 0:1] * _load_token_major(y0_ref, n, F32)
         + g[:, 1:2] * _load_token_major(y1_ref, n, F32))
    o_ref[...] = x_ref[...] + y


def _combine(x2, gates, y2, row_off, n_rows, tile):
    tp, d = x2.shape
    off = row_off // tile
    plane = tp // tile
    return pl.pallas_call(
        _combine_kernel,
        grid=(n_rows // tile,),
        in_specs=[pl.BlockSpec((tile, d), lambda i: (off + i, 0)),
                  pl.BlockSpec((tile, LANES), lambda i: (off + i, 0)),
                  pl.BlockSpec((tile * TOKEN_PITCH, LANES), lambda i: (off + i, 0)),
                  pl.BlockSpec((tile * TOKEN_PITCH, LANES), lambda i: (plane + off + i, 0))],
        out_specs=pl.BlockSpec((tile, d), lambda i: (i, 0)),
        out_shape=jax.ShapeDtypeStruct((n_rows, d), F32),
        compiler_params=_cparams(("parallel",)),
        name="moe_combine",
    )(x2, gates, y2, y2)


def kernel(x_prompt, x_sample, cache_k, cache_v, state_ssm_re, state_ssm_im, meta_tokens, norm1_g, w_in, b_in, ssm_a_re, ssm_a_im, ssm_log_dt, ssm_b_re, ssm_b_im, ssm_c_re, ssm_c_im, ssm_d, w_glu, b_glu, w_ssm_proj, q_norm_g, k_norm_g, lam_q1, lam_k1, lam_q2, lam_k2, subln_g, w_att_proj, w_o, norm2_g, w_router_group, b_router_group, w_router_expert, b_router_expert, w1_e, w3_e, w2_e):
    assert x_prompt.shape[0] == 1 and w_in.shape[0] == 1
    seq = x_prompt.shape[1]
    nb, nq = x_sample.shape[0], x_sample.shape[1]
    past = cache_k.shape[2]
    n_s = nb * nq
    t_real = seq + N_META + n_s
    tp = -(-t_real // ROW_ALIGN) * ROW_ALIGN
    off_meta, off_s = seq, seq + N_META
    tq = 256
    assert seq % tq == 0 and nq == N_META and past % 512 == 0
    lam_init = 0.8 - 0.6 * math.exp(-0.3 * 0)
    out_scale = 1.0 - lam_init

    x_tail = jnp.concatenate([meta_tokens.astype(F32), x_sample.reshape(n_s, D_MODEL),
                              jnp.zeros((tp - t_real, D_MODEL), F32)], axis=0)

    tm = _row_tile(tp, 1088)
    tr = _row_tile(tp, 256)
    assert seq % tr == 0

    h1, x_cat = _rmsnorm(x_prompt[0], x_tail, norm1_g[0], tr)
    z = _inproj(h1, w_in[0], b_in[0], tm, 512)

    pos = jnp.concatenate([N_META + jnp.arange(seq), jnp.arange(N_META),
                           jnp.tile(past + jnp.arange(nq), nb),
                           jnp.zeros((tp - t_real,), jnp.int32)]).astype(F32)
    half = HEAD_DIM // 2
    inv = ROPE_THETA ** (-jnp.arange(half, dtype=F32) / half)
    ang = pos[:, None] * inv[None, :]
    cos_t = jnp.tile(jnp.cos(ang), (1, LANES // half))
    sin_h = jnp.sin(ang)
    sin_t = jnp.tile(jnp.concatenate([-sin_h, sin_h], axis=1), (1, LANES // HEAD_DIM))
    gq = jnp.tile(q_norm_g[0], LANES // HEAD_DIM).reshape(1, LANES)
    gk = jnp.tile(k_norm_g[0], LANES // HEAD_DIM).reshape(1, LANES)
    qb, kf, kb, vf, vb = _qk_rope(z, cos_t, sin_t, gq, gk, tr)

    lam = (jnp.exp(jnp.sum(lam_q1[0] * lam_k1[0])) - jnp.exp(jnp.sum(lam_q2[0] * lam_k2[0])) + lam_init)
    lam_row = jnp.full((1, LANES), lam, F32)
    sg = subln_g[0].reshape(1, LANES)

    q_s = qb[off_s:off_s + n_s].reshape(nb, nq, N_HEADS, LANES).transpose(0, 2, 1, 3)
    lane = jnp.arange(LANES)
    qz = jnp.concatenate([jnp.where(lane < HEAD_DIM, q_s, 0), jnp.where(lane >= HEAD_DIM, q_s, 0)], axis=2)
    pad_new = lambda a: jnp.pad(a[off_s:off_s + n_s].reshape(nb, nq, D_ATT), ((0, 0), (0, LANES - nq), (0, 0)))
    ck = cache_k[0].reshape(nb, past, D_ATT)
    cv = cache_v[0].reshape(nb, past * N_HEADS, V_DIM)

    logit_bound = 8.1 * jnp.max(jnp.abs(q_norm_g[0])) * jnp.max(jnp.abs(k_norm_g[0]))

    def attention(fast):
        def run():
            if fast:
                o = _attn_prompt_fast(qb, kb, vb, lam_row, sg, seq, out_scale, 512)
            else:
                o = _attn_prompt(qb, kb, vb, lam_row, sg, seq, out_scale, tq)
            return _attn_sample(qz, ck, cv, pad_new(kb), pad_new(vb), o, lam_row, sg, off_s, out_scale,
                                512, fast)
        return run

    o_att = lax.cond(logit_bound <= LOGIT_BOUND_MAX, attention(True), attention(False))

    gp = N_SSM_GROUPS * SSM_STATE
    s5w = _s5_weights(ssm_a_re[0], ssm_a_im[0], ssm_log_dt[0], ssm_b_re[0], ssm_b_im[0],
                      ssm_c_re[0], ssm_c_im[0])
    ys, hp_re, hp_im, hs_re, hs_im = _s5(z, s5w, ssm_d[0], state_ssm_re[0].reshape(nb, gp),
                                         state_ssm_im[0].reshape(nb, gp), seq, nb, nq)
    ysg = _glu(ys, w_glu[0], b_glu[0], tm, 512)
    m = _merge(ysg, o_att, w_ssm_proj[0], w_att_proj[0], z, tm, 512)
    x2 = _outproj(m, w_o[0], x_cat, tm, 512)

    w_r = jnp.concatenate([w_router_group[0], w_router_expert[0],
                           jnp.zeros((D_MODEL, LANES - N_EGROUPS - N_EXPERTS), F32)], axis=1).astype(BF16)
    b_r = jnp.concatenate([b_router_group[0], b_router_expert[0],
                           jnp.zeros((LANES - N_EGROUPS - N_EXPERTS,), F32)]).reshape(1, LANES)
    h2, e_sel, g_sel = _router(x2, norm2_g[0], w_r, b_r, tr)

    plan = _route_plan(e_sel[:t_real, :TOP_K], t_real, tp)
    y2 = _experts(h2, plan, t_real, w1_e[0], w3_e[0], w2_e[0])

    def heads(a, lead):
        return a.reshape(lead + (N_HEADS, 2, HEAD_DIM))

    y_prompt = _combine(x2, g_sel, y2, 0, seq, LANES).reshape(1, seq, D_MODEL)
    y_sample = _combine(x2, g_sel, y2, off_s, n_s, nq).reshape(nb, nq, D_MODEL)
    k_p = jnp.concatenate([kf[off_meta:off_meta + N_META], kf[:seq]], axis=0)
    v_p = jnp.concatenate([vf[off_meta:off_meta + N_META], vf[:seq]], axis=0)
    k_prompt = heads(k_p, (1, 1, seq + N_META))
    v_prompt = v_p.reshape(1, 1, seq + N_META, N_HEADS, V_DIM)
    k_sample = heads(kf[off_s:off_s + n_s], (1, nb, nq))
    v_sample = vf[off_s:off_s + n_s].reshape(1, nb, nq, N_HEADS, V_DIM)
    st = lambda a, lead: a.reshape(lead + (N_SSM_GROUPS, SSM_STATE))
    return (y_prompt, y_sample, k_prompt, v_prompt, st(hp_re, (1, 1)), st(hp_im, (1, 1)),
            k_sample, v_sample, st(hs_re, (1, nb)), st(hs_im, (1, nb)))
```

```python
import functools
import math

import jax
import jax.numpy as jnp
from jax import lax
from jax.experimental import pallas as pl
from jax.experimental.pallas import tpu as pltpu

F32 = jnp.float32
BF16 = jnp.bfloat16

D_MODEL = 4096
N_META = 16
CHUNK = 64
N_HEADS = 16
HEAD_DIM = 64
V_DIM = 128
D_ATT = N_HEADS * V_DIM
D_SSM = 2048
SSM_GROUP = 16
N_SSM_GROUPS = D_SSM // SSM_GROUP
SSM_STATE = 64
IN_WIDTH = D_SSM + 3 * D_ATT + 2 * D_MODEL
ROPE_THETA = 10000.0
N_EGROUPS = 8
EXPERTS_PER_GROUP = 8
N_EXPERTS = N_EGROUPS * EXPERTS_PER_GROUP
TOP_K = 2
D_EXPERT = 512
MOE_BLOCK = 320
EXPERT_SLICES = 4
EPS = 1e-6

LANES = 128
ROW_ALIGN = 512
S5_CHUNK = 8
S5_LANE_GROUPS = LANES // SSM_GROUP
S5_TILES = D_SSM // LANES
VMEM_LIMIT = 56 * 1024 * 1024


def _cparams(sem, vmem=VMEM_LIMIT):
    return pltpu.CompilerParams(dimension_semantics=sem, vmem_limit_bytes=vmem)


def _row_tile(tp, cap):
    best = 16
    for t in range(16, cap + 1, 16):
        if tp % t == 0:
            best = t
    return best


def _dot(a, b):
    return jnp.dot(a, b, preferred_element_type=F32)


def _dot_nt(a, b):
    return lax.dot_general(a, b, (((1,), (1,)), ((), ())), preferred_element_type=F32)


def _rmsnorm_kernel(xa_ref, xb_ref, g_ref, o_ref, xc_ref, *, n_a):
    def emit(x):
        ms = jnp.mean(x * x, axis=-1, keepdims=True)
        o_ref[...] = (x * lax.rsqrt(ms + EPS) * g_ref[...]).astype(o_ref.dtype)
        xc_ref[...] = x

    @pl.when(pl.program_id(0) < n_a)
    def _():
        emit(xa_ref[...])

    @pl.when(pl.program_id(0) >= n_a)
    def _():
        emit(xb_ref[...])


def _rmsnorm(x_a, x_b, g, tr):
    d = x_a.shape[1]
    n_a, n_b = x_a.shape[0] // tr, x_b.shape[0] // tr
    tp = x_a.shape[0] + x_b.shape[0]
    row = pl.BlockSpec((tr, d), lambda i: (i, 0))
    return pl.pallas_call(
        functools.partial(_rmsnorm_kernel, n_a=n_a),
        grid=(n_a + n_b,),
        in_specs=[pl.BlockSpec((tr, d), lambda i: (jnp.minimum(i, n_a - 1), 0)),
                  pl.BlockSpec((tr, d), lambda i: (jnp.maximum(i - n_a, 0), 0)),
                  pl.BlockSpec((1, d), lambda i: (0, 0))],
        out_specs=[row, row],
        out_shape=[jax.ShapeDtypeStruct((tp, d), BF16), jax.ShapeDtypeStruct((tp, d), F32)],
        compiler_params=_cparams(("arbitrary",)),
        name="rmsnorm1",
    )(x_a, x_b, g.reshape(1, d))


def _inproj_kernel(x_ref, w_ref, b_ref, o_ref):
    o_ref[...] = _dot(x_ref[...], w_ref[...].astype(BF16)) + b_ref[...]


def _inproj(h, w, b, tm, tn):
    tp, k = h.shape
    n = w.shape[1]
    return pl.pallas_call(
        _inproj_kernel,
        grid=(n // tn, tp // tm),
        in_specs=[pl.BlockSpec((tm, k), lambda j, i: (i, 0)),
                  pl.BlockSpec((k, tn), lambda j, i: (0, j)),
                  pl.BlockSpec((1, tn), lambda j, i: (0, j))],
        out_specs=pl.BlockSpec((tm, tn), lambda j, i: (i, j)),
        out_shape=jax.ShapeDtypeStruct((tp, n), F32),
        compiler_params=_cparams(("parallel", "parallel")),
        name="in_proj",
    )(h, w, b.reshape(1, n))


def _glu_kernel(x_ref, w_ref, b_ref, xe_ref, o_ref):
    a = _dot(x_ref[...].astype(BF16), w_ref[...].astype(BF16)) + b_ref[...]
    o_ref[...] = (xe_ref[...] * jax.nn.sigmoid(a)).astype(o_ref.dtype)


def _glu(ys, w, b, tm, tn):
    tp, k = ys.shape
    n = w.shape[1]
    return pl.pallas_call(
        _glu_kernel,
        grid=(n // tn, tp // tm),
        in_specs=[pl.BlockSpec((tm, k), lambda j, i: (i, 0)),
                  pl.BlockSpec((k, tn), lambda j, i: (0, j)),
                  pl.BlockSpec((1, tn), lambda j, i: (0, j)),
                  pl.BlockSpec((tm, tn), lambda j, i: (i, j))],
        out_specs=pl.BlockSpec((tm, tn), lambda j, i: (i, j)),
        out_shape=jax.ShapeDtypeStruct((tp, n), BF16),
        compiler_params=_cparams(("parallel", "parallel")),
        name="glu",
    )(ys, w, b.reshape(1, n), ys)


def _merge_kernel(ys_ref, oa_ref, ws_ref, wa_ref, gs_ref, ga_ref, o_ref):
    a = _dot(ys_ref[...], ws_ref[...].astype(BF16))
    b = _dot(oa_ref[...], wa_ref[...].astype(BF16))
    m = jax.nn.sigmoid(gs_ref[...]) * a + jax.nn.sigmoid(ga_ref[...]) * b
    o_ref[...] = m.astype(o_ref.dtype)


def _merge(ysg, oatt, w_ssm, w_att, z, tm, tn):
    tp, k = ysg.shape
    n = w_ssm.shape[1]
    gs_blk = (D_SSM + 3 * D_ATT) // tn
    ga_blk = (D_SSM + 3 * D_ATT + D_MODEL) // tn
    return pl.pallas_call(
        _merge_kernel,
        grid=(n // tn, tp // tm),
        in_specs=[pl.BlockSpec((tm, k), lambda j, i: (i, 0)),
                  pl.BlockSpec((tm, k), lambda j, i: (i, 0)),
                  pl.BlockSpec((k, tn), lambda j, i: (0, j)),
                  pl.BlockSpec((k, tn), lambda j, i: (0, j)),
                  pl.BlockSpec((tm, tn), lambda j, i: (i, gs_blk + j)),
                  pl.BlockSpec((tm, tn), lambda j, i: (i, ga_blk + j))],
        out_specs=pl.BlockSpec((tm, tn), lambda j, i: (i, j)),
        out_shape=jax.ShapeDtypeStruct((tp, n), BF16),
        compiler_params=_cparams(("parallel", "parallel")),
        name="merge_proj",
    )(ysg, oatt, w_ssm, w_att, z, z)


def _outproj_kernel(m_ref, w_ref, x_ref, o_ref):
    o_ref[...] = x_ref[...] + _dot(m_ref[...], w_ref[...].astype(BF16))


def _outproj(m, w, x, tm, tn):
    tp, k = m.shape
    n = w.shape[1]
    return pl.pallas_call(
        _outproj_kernel,
        grid=(n // tn, tp // tm),
        in_specs=[pl.BlockSpec((tm, k), lambda j, i: (i, 0)),
                  pl.BlockSpec((k, tn), lambda j, i: (0, j)),
                  pl.BlockSpec((tm, tn), lambda j, i: (i, j))],
        out_specs=pl.BlockSpec((tm, tn), lambda j, i: (i, j)),
        out_shape=jax.ShapeDtypeStruct((tp, n), F32),
        compiler_params=_cparams(("parallel", "parallel")),
        name="out_proj",
    )(m, w, x)


def _segment_sumsq(x, ones_bd):
    x2 = x * x
    hi = x2.astype(BF16)
    lo = (x2 - hi.astype(F32)).astype(BF16)
    return _dot(hi, ones_bd) + _dot(lo, ones_bd)


def _qk_rope_kernel(zq_ref, zk_ref, zv_ref, cos_ref, sin_ref, gq_ref, gk_ref, ones_ref,
                    qb_ref, kf_ref, kb_ref, vf_ref, vb_ref):
    cos = cos_ref[...]
    sin = sin_ref[...]
    ones_bd = ones_ref[...]
    lane = lax.broadcasted_iota(jnp.int32, cos.shape, 1)
    first_half = (lane % HEAD_DIM) < (HEAD_DIM // 2)

    def norm_rope(x, g):
        ss = _segment_sumsq(x, ones_bd)
        xn = x * lax.rsqrt(ss * (1.0 / HEAD_DIM) + EPS) * g
        partner = jnp.where(first_half,
                            pltpu.roll(xn, LANES - HEAD_DIM // 2, 1),
                            pltpu.roll(xn, HEAD_DIM // 2, 1))
        return xn * cos + partner * sin

    for h in range(N_HEADS):
        sl = slice(h * LANES, (h + 1) * LANES)
        q = norm_rope(zq_ref[:, sl], gq_ref[...])
        qb_ref[:, sl] = (q * (HEAD_DIM ** -0.5)).astype(BF16)
        k = norm_rope(zk_ref[:, sl], gk_ref[...])
        kf_ref[:, sl] = k
        kb_ref[:, sl] = k.astype(BF16)
    v = zv_ref[...]
    vf_ref[...] = v
    vb_ref[...] = v.astype(BF16)


def _qk_rope(z, cos_t, sin_t, gq, gk, tr):
    tp = z.shape[0]
    ones_bd = jnp.kron(jnp.eye(LANES // HEAD_DIM, dtype=F32),
                       jnp.ones((HEAD_DIM, HEAD_DIM), F32)).astype(BF16)
    zspec = lambda c: pl.BlockSpec((tr, D_ATT), lambda i: (i, c))
    row = pl.BlockSpec((tr, LANES), lambda i: (i, 0))
    const = pl.BlockSpec((1, LANES), lambda i: (0, 0))
    out = pl.BlockSpec((tr, D_ATT), lambda i: (i, 0))
    q_blk = D_SSM // D_ATT
    return pl.pallas_call(
        _qk_rope_kernel,
        grid=(tp // tr,),
        in_specs=[zspec(q_blk), zspec(q_blk + 1), zspec(q_blk + 2), row, row, const, const,
                  pl.BlockSpec((LANES, LANES), lambda i: (0, 0))],
        out_specs=[out, out, out, out, out],
        out_shape=[jax.ShapeDtypeStruct((tp, D_ATT), BF16),
                   jax.ShapeDtypeStruct((tp, D_ATT), F32),
                   jax.ShapeDtypeStruct((tp, D_ATT), BF16),
                   jax.ShapeDtypeStruct((tp, D_ATT), F32),
                   jax.ShapeDtypeStruct((tp, D_ATT), BF16)],
        compiler_params=_cparams(("parallel",)),
        name="qk_norm_rope",
    )(z, z, z, cos_t, sin_t, gq, gk, ones_bd)


def _softmax_step(c, qc, kt, vt, mask, m_ref, l_ref, acc_ref):
    s = _dot_nt(qc, kt)
    if mask is not None:
        s = jnp.where(mask, s, -jnp.inf)
    m_prev = m_ref[c]
    m_new = jnp.maximum(m_prev, jnp.max(s, axis=1, keepdims=True))
    alpha = jnp.exp(m_prev - m_new)
    p = jnp.exp(s - m_new[:, :1])
    l_ref[c] = alpha * l_ref[c] + jnp.sum(p, axis=1, keepdims=True)
    acc_ref[c] = alpha * acc_ref[c] + _dot(p.astype(BF16), vt)
    m_ref[c] = m_new


def _diff_finish(o0, o1, lam, g, out_scale):
    o = o0 - lam * o1
    ms = jnp.mean(o * o, axis=-1, keepdims=True)
    return o * lax.rsqrt(ms + EPS) * g * out_scale


def _split_components(q):
    lane = lax.broadcasted_iota(jnp.int32, q.shape, 1)
    zero = jnp.zeros_like(q)
    return jnp.where(lane < HEAD_DIM, q, zero), jnp.where(lane >= HEAD_DIM, q, zero)


def _attn_prompt_kernel(lam_ref, g_ref, q_ref, k_ref, v_ref, o_ref, m_ref, l_ref, acc_ref,
                        *, tq, nq_main, seq, out_scale):
    i = pl.program_id(1)
    q0, q1 = _split_components(q_ref[...])
    m_ref[...] = jnp.full(m_ref.shape, -jnp.inf, F32)
    l_ref[...] = jnp.zeros(l_ref.shape, F32)
    acc_ref[...] = jnp.zeros(acc_ref.shape, F32)

    def update(kt, vt, mask):
        _softmax_step(0, q0, kt, vt, mask, m_ref, l_ref, acc_ref)
        _softmax_step(1, q1, kt, vt, mask, m_ref, l_ref, acc_ref)

    col = lax.broadcasted_iota(jnp.int32, (tq, LANES), 1)
    update(k_ref[pl.ds(seq, LANES), :], v_ref[pl.ds(seq, LANES), :], col < N_META)

    is_main = i < nq_main

    def body(j, carry):
        start = pl.multiple_of(j * tq, tq)
        update(k_ref[pl.ds(start, tq), :], v_ref[pl.ds(start, tq), :], None)
        return carry

    lax.fori_loop(0, jnp.where(is_main, i, 0), body, 0)

    @pl.when(is_main)
    def _():
        start = pl.multiple_of(i * tq, tq)
        r = lax.broadcasted_iota(jnp.int32, (tq, tq), 0) // CHUNK
        c = lax.broadcasted_iota(jnp.int32, (tq, tq), 1) // CHUNK
        update(k_ref[pl.ds(start, tq), :], v_ref[pl.ds(start, tq), :], c <= r)

    o0 = acc_ref[0] / l_ref[0]
    o1 = acc_ref[1] / l_ref[1]
    o_ref[...] = _diff_finish(o0, o1, lam_ref[...], g_ref[...], out_scale).astype(o_ref.dtype)


def _attn_prompt(qb, kb, vb, lam_row, subln_g, seq, out_scale, tq):
    tp = qb.shape[0]
    kern = functools.partial(_attn_prompt_kernel, tq=tq, nq_main=seq // tq, seq=seq,
                             out_scale=out_scale)
    const = pl.BlockSpec((1, LANES), lambda h, i: (0, 0))
    return pl.pallas_call(
        kern,
        grid=(N_HEADS, tp // tq),
        in_specs=[const, const,
                  pl.BlockSpec((tq, LANES), lambda h, i: (i, h)),
                  pl.BlockSpec((tp, LANES), lambda h, i: (0, h)),
                  pl.BlockSpec((tp, LANES), lambda h, i: (0, h))],
        out_specs=pl.BlockSpec((tq, LANES), lambda h, i: (i, h)),
        out_shape=jax.ShapeDtypeStruct((tp, D_ATT), BF16),
        scratch_shapes=[pltpu.VMEM((2, tq, LANES), F32),
                        pltpu.VMEM((2, tq, LANES), F32),
                        pltpu.VMEM((2, tq, LANES), F32)],
        compiler_params=_cparams(("parallel", "parallel")),
        name="attn_prompt",
    )(lam_row, subln_g, qb, kb, vb)


LOGIT_BOUND_MAX = 40.0
ATTN_WIDE = 4


def _with_ones(vt):
    return jnp.concatenate([vt, jnp.ones(vt.shape, vt.dtype)], axis=1)


def _cache_v_head(vc_ref, h):
    tk = vc_ref.shape[0] // N_HEADS
    return vc_ref[pl.ds(h, tk, stride=N_HEADS), :].astype(BF16)


def _attn_prompt_fast_kernel(lam_ref, g_ref, q_ref, k_ref, v_ref, o_ref, acc_ref,
                             *, tq, nq_main, seq, out_scale):
    i = pl.program_id(1)
    q0, q1 = _split_components(q_ref[...])
    qq = jnp.concatenate([q0, q1], axis=0)

    def scores(start, rows):
        return _dot_nt(qq, k_ref[pl.ds(start, rows), :])

    def weighted(s, start, rows, mask):
        p = jnp.exp(s)
        if mask is not None:
            p = jnp.where(mask, p, 0.0)
        return _dot(p.astype(BF16), _with_ones(v_ref[pl.ds(start, rows), :]))

    def tile_pv(start, rows, mask):
        return weighted(scores(start, rows), start, rows, mask)

    is_main = i < nq_main
    n_full = jnp.where(is_main, i, 0)

    @pl.when(jnp.logical_not(is_main))
    def _():
        col = lax.broadcasted_iota(jnp.int32, (2 * tq, LANES), 1)
        acc_ref[...] = tile_pv(seq, LANES, col < N_META)

    @pl.when(is_main)
    def _():
        start = pl.multiple_of(i * tq, tq)
        kt = jnp.concatenate([k_ref[pl.ds(start, tq), :], k_ref[pl.ds(seq, LANES), :]], axis=0)
        vt = jnp.concatenate([v_ref[pl.ds(start, tq), :], v_ref[pl.ds(seq, LANES), :]], axis=0)
        r = (lax.broadcasted_iota(jnp.int32, (2 * tq, tq + LANES), 0) % tq) // CHUNK
        c = lax.broadcasted_iota(jnp.int32, (2 * tq, tq + LANES), 1)
        mask = ((c < tq) & (c // CHUNK <= r)) | ((c >= tq) & (c < tq + N_META))
        p = jnp.where(mask, jnp.exp(_dot_nt(qq, kt)), 0.0)
        acc_ref[...] = _dot(p.astype(BF16), _with_ones(vt))

    wide = ATTN_WIDE

    def body(j, carry):
        start = pl.multiple_of(wide * j * tq, tq)
        acc_ref[...] += tile_pv(start, wide * tq, None)
        return carry

    n_wide = n_full // wide
    lax.fori_loop(0, n_wide, body, 0)
    rem = n_full - wide * n_wide
    base = wide * n_wide

    @pl.when(rem >= 2)
    def _():
        acc_ref[...] += tile_pv(pl.multiple_of(base * tq, tq), 2 * tq, None)

    @pl.when(rem % 2 == 1)
    def _():
        acc_ref[...] += tile_pv(pl.multiple_of((n_full - 1) * tq, tq), tq, None)

    acc = acc_ref[...]
    o0 = acc[:tq, :LANES] / acc[:tq, LANES:]
    o1 = acc[tq:, :LANES] / acc[tq:, LANES:]
    o_ref[...] = _diff_finish(o0, o1, lam_ref[...], g_ref[...], out_scale).astype(o_ref.dtype)


def _attn_prompt_fast(qb, kb, vb, lam_row, subln_g, seq, out_scale, tq):
    tp = qb.shape[0]
    kern = functools.partial(_attn_prompt_fast_kernel, tq=tq, nq_main=seq // tq, seq=seq,
                             out_scale=out_scale)
    const = pl.BlockSpec((1, LANES), lambda h, i: (0, 0))
    return pl.pallas_call(
        kern,
        grid=(N_HEADS, tp // tq),
        in_specs=[const, const,
                  pl.BlockSpec((tq, LANES), lambda h, i: (i, h)),
                  pl.BlockSpec((tp, LANES), lambda h, i: (0, h)),
                  pl.BlockSpec((tp, LANES), lambda h, i: (0, h))],
        out_specs=pl.BlockSpec((tq, LANES), lambda h, i: (i, h)),
        out_shape=jax.ShapeDtypeStruct((tp, D_ATT), BF16),
        scratch_shapes=[pltpu.VMEM((2 * tq, 2 * LANES), F32)],
        compiler_params=_cparams(("parallel", "parallel")),
        name="attn_prompt_fast",
    )(lam_row, subln_g, qb, kb, vb)


def _attn_sample_fast_kernel(lam_ref, g_ref, q_ref, kc_ref, vc_ref, kn_ref, vn_ref, o_in_ref, o_ref,
                             acc_ref, *, nq, out_scale):
    del o_in_ref
    j = pl.program_id(1)
    rows = 2 * nq

    def head_update(h, kt, vt, mask):
        p = jnp.exp(_dot_nt(q_ref[h], kt))
        if mask is not None:
            p = jnp.where(mask, p, 0.0)
        acc_ref[h] += _dot(p.astype(BF16), _with_ones(vt))

    @pl.when(j == 0)
    def _():
        acc_ref[...] = jnp.zeros(acc_ref.shape, F32)
        col = lax.broadcasted_iota(jnp.int32, (rows, LANES), 1)
        for h in range(N_HEADS):
            sl = slice(h * LANES, (h + 1) * LANES)
            head_update(h, kn_ref[:, sl], vn_ref[:, sl], col < nq)

    for h in range(N_HEADS):
        sl = slice(h * LANES, (h + 1) * LANES)
        head_update(h, kc_ref[:, sl].astype(BF16), _cache_v_head(vc_ref, h), None)

    @pl.when(j == pl.num_programs(1) - 1)
    def _():
        for h in range(N_HEADS):
            acc = acc_ref[h]
            o = acc[:, :LANES] / acc[:, LANES:]
            res = _diff_finish(o[:nq], o[nq:], lam_ref[...], g_ref[...], out_scale)
            o_ref[:, h * LANES:(h + 1) * LANES] = res.astype(o_ref.dtype)


def _attn_sample_kernel(lam_ref, g_ref, q_ref, kc_ref, vc_ref, kn_ref, vn_ref, o_in_ref, o_ref,
                        m_ref, l_ref, acc_ref, *, nq, out_scale):
    del o_in_ref
    j = pl.program_id(1)
    rows = 2 * nq

    @pl.when(j == 0)
    def _():
        m_ref[...] = jnp.full(m_ref.shape, -jnp.inf, F32)
        l_ref[...] = jnp.zeros(l_ref.shape, F32)
        acc_ref[...] = jnp.zeros(acc_ref.shape, F32)
        col = lax.broadcasted_iota(jnp.int32, (rows, LANES), 1)
        for h in range(N_HEADS):
            sl = slice(h * LANES, (h + 1) * LANES)
            _softmax_step(h, q_ref[h], kn_ref[:, sl], vn_ref[:, sl], col < nq,
                          m_ref, l_ref, acc_ref)

    for h in range(N_HEADS):
        sl = slice(h * LANES, (h + 1) * LANES)
        _softmax_step(h, q_ref[h], kc_ref[:, sl].astype(BF16), _cache_v_head(vc_ref, h), None,
                      m_ref, l_ref, acc_ref)

    @pl.when(j == pl.num_programs(1) - 1)
    def _():
        for h in range(N_HEADS):
            o = acc_ref[h] / l_ref[h]
            res = _diff_finish(o[:nq], o[nq:], lam_ref[...], g_ref[...], out_scale)
            o_ref[:, h * LANES:(h + 1) * LANES] = res.astype(o_ref.dtype)


def _attn_sample(qz, cache_k, cache_v, k_new, v_new, o_buf, lam_row, subln_g, row_off, out_scale, tk, fast):
    nb, past = cache_k.shape[0], cache_k.shape[1]
    nq = qz.shape[2] // 2
    if fast:
        kern = functools.partial(_attn_sample_fast_kernel, nq=nq, out_scale=out_scale)
        scratch = [pltpu.VMEM((N_HEADS, 2 * nq, 2 * LANES), F32)]
    else:
        kern = functools.partial(_attn_sample_kernel, nq=nq, out_scale=out_scale)
        scratch = [pltpu.VMEM((N_HEADS, 2 * nq, LANES), F32)] * 3
    const = pl.BlockSpec((1, LANES), lambda b, j: (0, 0))
    blk_off = row_off // nq
    return pl.pallas_call(
        kern,
        grid=(nb, past // tk),
        in_specs=[const, const,
                  pl.BlockSpec((None, N_HEADS, 2 * nq, LANES), lambda b, j: (b, 0, 0, 0)),
                  pl.BlockSpec((None, tk, D_ATT), lambda b, j: (b, j, 0)),
                  pl.BlockSpec((None, tk * N_HEADS, V_DIM), lambda b, j: (b, j, 0)),
                  pl.BlockSpec((None, LANES, D_ATT), lambda b, j: (b, 0, 0)),
                  pl.BlockSpec((None, LANES, D_ATT), lambda b, j: (b, 0, 0)),
                  pl.BlockSpec(memory_space=pl.ANY)],
        out_specs=pl.BlockSpec((nq, D_ATT), lambda b, j: (blk_off + b, 0)),
        out_shape=jax.ShapeDtypeStruct(o_buf.shape, o_buf.dtype),
        scratch_shapes=scratch,
        input_output_aliases={7: 0},
        compiler_params=_cparams(("parallel", "arbitrary")),
        name="attn_sample_fast" if fast else "attn_sample",
    )(lam_row, subln_g, qz, cache_k, cache_v, k_new, v_new, o_buf)


def _s5_weights(a_re, a_im, log_dt, b_re, b_im, c_re, c_im):
    hp = lax.Precision.HIGHEST
    n_t, gl, tc = S5_TILES, S5_LANE_GROUPS, S5_CHUNK
    dt = jnp.exp(log_dt)[:, None]
    mag = jnp.exp(a_re * dt)
    abar_re = mag * jnp.cos(a_im * dt)
    abar_im = mag * jnp.sin(a_im * dt)
    nr, ni = abar_re - 1.0, abar_im
    den = a_re * a_re + a_im * a_im
    coef_re = (nr * a_re + ni * a_im) / den
    coef_im = (ni * a_re - nr * a_im) / den
    bbar_re = coef_re[..., None] * b_re - coef_im[..., None] * b_im
    bbar_im = coef_re[..., None] * b_im + coef_im[..., None] * b_re
    n = jnp.arange(tc + 1, dtype=F32)[:, None, None]
    pw_mag = jnp.exp(n * (a_re * dt))
    pw_re = pw_mag * jnp.cos(n * (a_im * dt))
    pw_im = pw_mag * jnp.sin(n * (a_im * dt))
    e_re = pw_re[:tc, :, :, None] * bbar_re - pw_im[:tc, :, :, None] * bbar_im
    e_im = pw_re[:tc, :, :, None] * bbar_im + pw_im[:tc, :, :, None] * bbar_re
    kern = (jnp.einsum('gcp,lgpd->glcd', c_re, e_re, precision=hp)
            - jnp.einsum('gcp,lgpd->glcd', c_im, e_im, precision=hp))
    eye = jnp.eye(gl, dtype=F32)
    w_intra = jnp.einsum('jglcd,gh->jlgdhc', kern.reshape(n_t, gl, tc, SSM_GROUP, SSM_GROUP), eye)
    w_intra = w_intra.reshape(n_t, tc, LANES, LANES)
    eb = jnp.stack([e_re[::-1], e_im[::-1]], 0)
    eb = eb.reshape(2, tc, n_t, gl, SSM_STATE, SSM_GROUP)
    w_state = eb.transpose(2, 1, 3, 5, 0, 4).reshape(n_t, tc, LANES, 2 * SSM_STATE)
    cp_re = c_re[None] * pw_re[1:, :, None, :] - c_im[None] * pw_im[1:, :, None, :]
    cp_im = c_re[None] * pw_im[1:, :, None, :] + c_im[None] * pw_re[1:, :, None, :]
    cp = jnp.stack([cp_re, -cp_im], 0).reshape(2, tc, n_t, gl, SSM_GROUP, SSM_STATE)
    w_read = cp.transpose(2, 1, 0, 5, 3, 4).reshape(n_t, tc, 2 * SSM_STATE, LANES)
    half = gl * SSM_STATE
    a_pow = jnp.concatenate([pw_re[tc].reshape(n_t, 1, half), pw_im[tc].reshape(n_t, 1, half)], -1)
    rp = jnp.arange(2 * SSM_STATE)
    col = jnp.arange(2 * half)
    spread = ((rp[:, None] // SSM_STATE == col[None, :] // half)
              & (rp[:, None] % SSM_STATE == col[None, :] % SSM_STATE)).astype(BF16)
    return (w_intra.astype(BF16), w_state.astype(BF16), w_read.astype(BF16), a_pow, spread, spread.T)


def _s5_kernel(u_ref, wi_ref, wsc_ref, wrc_ref, ap_ref, sp_ref, spt_ref, d_ref, h0r_ref, h0i_ref,
               y_ref, hpr_ref, hpi_ref, hsr_ref, hsi_ref,
               y_acc, v_ref, hs_ref, wt_ref, ws_ref, wr_ref,
               *, nc, n_main, n_meta_chunks, n_seq, seq_chunks):
    tc = S5_CHUNK
    half = hs_ref.shape[1] // 2
    grp_r = lax.broadcasted_iota(jnp.int32, (LANES, 2 * half), 0) // SSM_GROUP
    grp_c = (lax.broadcasted_iota(jnp.int32, (LANES, 2 * half), 1) % half) // SSM_STATE
    for s in range(tc):
        full = _dot(wsc_ref[s], sp_ref[...])
        ws_ref[s * LANES:(s + 1) * LANES, :] = jnp.where(grp_r == grp_c, full, 0.0).astype(BF16)
    grp_r = (lax.broadcasted_iota(jnp.int32, (2 * half, LANES), 0) % half) // SSM_STATE
    grp_c = lax.broadcasted_iota(jnp.int32, (2 * half, LANES), 1) // SSM_GROUP
    for t in range(tc):
        full = _dot(spt_ref[...], wrc_ref[t])
        wr_ref[:, t * LANES:(t + 1) * LANES] = jnp.where(grp_r == grp_c, full, 0.0).astype(BF16)
    for s in range(tc):
        for t in range(tc):
            blk = wi_ref[t - s] if t >= s else jnp.zeros((LANES, LANES), BF16)
            wt_ref[s * LANES:(s + 1) * LANES, t * LANES:(t + 1) * LANES] = blk
    lhs = jnp.concatenate(
        [u_ref[pl.ds(s, nc, stride=tc), :].astype(BF16) for s in range(tc)], axis=1)
    y_acc[...] = _dot(lhs, wt_ref[...])
    v_ref[...] = _dot(lhs, ws_ref[...])
    a_re = ap_ref[:, :half]
    a_im = ap_ref[:, half:]

    def advance(h_re, h_im, v):
        return (a_re * h_re - a_im * h_im + v[:, :half],
                a_re * h_im + a_im * h_re + v[:, half:])

    hs_ref[...] = jnp.zeros(hs_ref.shape, F32)

    h_re = jnp.zeros((1, half), F32)
    h_im = jnp.zeros((1, half), F32)
    for c in range(n_main, n_main + n_meta_chunks):
        hs_ref[pl.ds(c, 1), :] = jnp.concatenate([h_re, h_im], axis=1)
        h_re, h_im = advance(h_re, h_im, v_ref[pl.ds(c, 1), :])

    def body(c, carry):
        h_re, h_im = carry
        hs_ref[pl.ds(c, 1), :] = jnp.concatenate([h_re, h_im], axis=1)
        return advance(h_re, h_im, v_ref[pl.ds(c, 1), :])

    h_re, h_im = lax.fori_loop(0, n_main, body, (h_re, h_im), unroll=8)
    hpr_ref[...] = h_re
    hpi_ref[...] = h_im

    base = n_main + n_meta_chunks
    for b in range(n_seq):
        s_re = h0r_ref[pl.ds(b, 1), :]
        s_im = h0i_ref[pl.ds(b, 1), :]
        for c in range(base + b * seq_chunks, base + (b + 1) * seq_chunks):
            hs_ref[pl.ds(c, 1), :] = jnp.concatenate([s_re, s_im], axis=1)
            s_re, s_im = advance(s_re, s_im, v_ref[pl.ds(c, 1), :])
        hsr_ref[pl.ds(b, 1), :] = s_re
        hsi_ref[pl.ds(b, 1), :] = s_im

    y_acc[...] += _dot(hs_ref[...].astype(BF16), wr_ref[...])
    d = d_ref[...]
    for t in range(tc):
        rows = pl.ds(t, nc, stride=tc)
        y = y_acc[:, t * LANES:(t + 1) * LANES] + d * u_ref[rows, :]
        y_ref[rows, :] = jax.nn.gelu(y)


def _s5(z, weights, d_skip, h0_re, h0_im, seq, n_seq, seq_len):
    tp = z.shape[0]
    tc = S5_CHUNK
    nc = tp // tc
    w_intra, w_state, w_read, a_pow, spread, spread_t = weights
    half = S5_LANE_GROUPS * SSM_STATE
    whole = lambda a: pl.BlockSpec(a.shape, lambda j: (0,) * a.ndim)
    kern = functools.partial(_s5_kernel, nc=nc, n_main=seq // tc, n_meta_chunks=N_META // tc,
                             n_seq=n_seq, seq_chunks=seq_len // tc)
    wspec = lambda a: pl.BlockSpec((None,) + a.shape[1:], lambda j: (j,) + (0,) * (a.ndim - 1))
    col = pl.BlockSpec((tp, LANES), lambda j: (0, j))
    st = lambda r: pl.BlockSpec((r, half), lambda j: (0, j))
    gp = N_SSM_GROUPS * SSM_STATE
    return pl.pallas_call(
        kern,
        grid=(S5_TILES,),
        in_specs=[col, wspec(w_intra), wspec(w_state), wspec(w_read), wspec(a_pow),
                  whole(spread), whole(spread_t),
                  pl.BlockSpec((1, LANES), lambda j: (0, j)), st(n_seq), st(n_seq)],
        out_specs=[col, st(1), st(1), st(n_seq), st(n_seq)],
        out_shape=[jax.ShapeDtypeStruct((tp, D_SSM), F32),
                   jax.ShapeDtypeStruct((1, gp), F32),
                   jax.ShapeDtypeStruct((1, gp), F32),
                   jax.ShapeDtypeStruct((n_seq, gp), F32),
                   jax.ShapeDtypeStruct((n_seq, gp), F32)],
        scratch_shapes=[pltpu.VMEM((nc, tc * LANES), F32),
                        pltpu.VMEM((nc, 2 * half), F32),
                        pltpu.VMEM((nc, 2 * half), F32),
                        pltpu.VMEM((tc * LANES, tc * LANES), BF16),
                        pltpu.VMEM((tc * LANES, 2 * half), BF16),
                        pltpu.VMEM((2 * half, tc * LANES), BF16)],
        compiler_params=_cparams(("parallel",)),
        name="s5_scan",
    )(z, w_intra, w_state, w_read, a_pow, spread, spread_t, d_skip.reshape(1, D_SSM), h0_re, h0_im)


ROW_CHUNKS = D_MODEL // LANES
TOKEN_PITCH = ROW_CHUNKS + 1


def _store_token_major(ref, x, spare_too=False):
    n = x.shape[0]
    for c in range(ROW_CHUNKS):
        ref[pl.ds(c, n, stride=TOKEN_PITCH), :] = x[:, c * LANES:(c + 1) * LANES].astype(ref.dtype)
    if spare_too:
        ref[pl.ds(ROW_CHUNKS, n, stride=TOKEN_PITCH), :] = jnp.zeros((n, LANES), ref.dtype)


def _load_token_major(ref, n, dtype, chunks=range(ROW_CHUNKS)):
    return jnp.concatenate([ref[pl.ds(c, n, stride=TOKEN_PITCH), :].astype(dtype) for c in chunks],
                           axis=1)


def _router_kernel(x_ref, g_ref, w_ref, b_ref, h_ref, e_ref, gate_ref):
    x = x_ref[...]
    ms = jnp.mean(x * x, axis=-1, keepdims=True)
    h = x * lax.rsqrt(ms + EPS) * g_ref[...]
    _store_token_major(h_ref, h, spare_too=True)
    logits = _dot(h.astype(BF16), w_ref[...]) + b_ref[...]
    lane = lax.broadcasted_iota(jnp.int32, logits.shape, 1)
    neg = -jnp.inf
    big = jnp.int32(LANES)

    def first_argmax(vals, vmax):
        return jnp.min(jnp.where(vals == vmax, lane, big), axis=1, keepdims=True)

    lg = jnp.where(lane < N_EGROUPS, logits, neg)
    mg = jnp.max(lg, axis=1, keepdims=True)
    sg = jnp.sum(jnp.exp(lg - mg), axis=1, keepdims=True)
    g_w = 1.0 / sg
    g_idx = first_argmax(lg, mg)
    lo = N_EGROUPS + EXPERTS_PER_GROUP * g_idx
    le = jnp.where((lane >= lo) & (lane < lo + EXPERTS_PER_GROUP), logits, neg)
    m1 = jnp.max(le, axis=1, keepdims=True)
    se = jnp.sum(jnp.exp(le - m1), axis=1, keepdims=True)
    i1 = first_argmax(le, m1)
    le2 = jnp.where(lane == i1, neg, le)
    m2 = jnp.max(le2, axis=1, keepdims=True)
    i2 = first_argmax(le2, m2)
    p1 = 1.0 / se
    p2 = jnp.exp(m2 - m1) / se
    tot = p1 + p2
    w1 = g_w * (p1 / tot)
    w2 = g_w * (p2 / tot)
    e_ref[...] = jnp.where(lane == 0, i1 - N_EGROUPS, jnp.where(lane == 1, i2 - N_EGROUPS, 0))
    gate_ref[...] = jnp.where(lane == 0, w1, jnp.where(lane == 1, w2, 0.0))


def _router(x2, g, w_r, b_r, tr):
    tp, d = x2.shape
    return pl.pallas_call(
        _router_kernel,
        grid=(tp // tr,),
        in_specs=[pl.BlockSpec((tr, d), lambda i: (i, 0)),
                  pl.BlockSpec((1, d), lambda i: (0, 0)),
                  pl.BlockSpec((d, LANES), lambda i: (0, 0)),
                  pl.BlockSpec((1, LANES), lambda i: (0, 0))],
        out_specs=[pl.BlockSpec((tr * TOKEN_PITCH, LANES), lambda i: (i, 0)),
                   pl.BlockSpec((tr, LANES), lambda i: (i, 0)),
                   pl.BlockSpec((tr, LANES), lambda i: (i, 0))],
        out_shape=[jax.ShapeDtypeStruct((tp * TOKEN_PITCH, LANES), F32),
                   jax.ShapeDtypeStruct((tp, LANES), jnp.int32),
                   jax.ShapeDtypeStruct((tp, LANES), F32)],
        compiler_params=_cparams(("parallel",)),
        name="norm2_router",
    )(x2, g.reshape(1, d), w_r, b_r)


def _expert_kernel(be_ref, nu_ref, first_ref, cnt_ref, tok_ref, dst_ref, h_hbm, w1_ref, w3_ref, w2_ref,
                   y_hbm, xbuf, xb16, acc_ref, ybuf, gsem, ssem, *, nb, plane_rows, plane_pad):
    del be_ref
    b = pl.program_id(0)
    hh = pl.program_id(1)
    last = pl.num_programs(1) - 1
    n_used = nu_ref[0]
    active = b < n_used
    slot = b % 2
    rows = xb16.shape[1]
    rc = ROW_CHUNKS
    pitch = TOKEN_PITCH
    spare_row = TOP_K * plane_rows

    def token_rows(ref, t):
        return ref.at[pl.ds(t * pitch, pitch), :]

    def start_gather(blk, s):
        base = first_ref[blk]
        for r in range(rows):
            pltpu.make_async_copy(token_rows(h_hbm, tok_ref[base + r]),
                                  xbuf.at[s, pl.ds(r * pitch, pitch), :], gsem.at[s]).start()

    def all_rows_gathered(s):
        return pltpu.make_async_copy(h_hbm.at[pl.ds(0, rows * pitch), :], xbuf.at[s], gsem.at[s])

    def all_rows_scattered():
        return pltpu.make_async_copy(ybuf, y_hbm.at[pl.ds(spare_row * pitch, rows * pitch), :], ssem)

    @pl.when((b == 0) & (hh == 0))
    def _():
        ybuf[...] = jnp.zeros(ybuf.shape, F32)
        fills = [all_rows_scattered()]
        if plane_pad:
            fills += [pltpu.make_async_copy(
                ybuf.at[pl.ds(0, plane_pad * pitch), :],
                y_hbm.at[pl.ds(((k + 1) * plane_rows - plane_pad) * pitch, plane_pad * pitch), :], ssem)
                for k in range(TOP_K)]
        for f in fills:
            f.start()
        for f in fills:
            f.wait()

        @pl.when(active)
        def _():
            start_gather(0, 0)

    @pl.when(active & (hh == 0))
    def _():
        all_rows_gathered(slot).wait()
        per = rc // EXPERT_SLICES
        for q in range(EXPERT_SLICES):
            xb16[q] = _load_token_major(xbuf.at[slot], rows, BF16, range(q * per, (q + 1) * per))

    @pl.when(active & (hh == 1) & (b + 1 < n_used))
    def _():
        start_gather(jnp.minimum(b + 1, nb - 1), 1 - slot)

    @pl.when(active)
    def _():
        w13 = jnp.concatenate([w1_ref[...].astype(BF16), w3_ref[...].astype(BF16)], axis=1)
        part = _dot(xb16[_expert_slice(b, hh)], w13)

        @pl.when(hh == 0)
        def _():
            acc_ref[...] = part

        @pl.when(hh > 0)
        def _():
            acc_ref[...] += part

    @pl.when(active & (hh == last))
    def _():
        @pl.when(b > 0)
        def _():
            all_rows_scattered().wait()

        ac = acc_ref[...]
        hid = (jax.nn.silu(ac[:, :D_EXPERT]) * ac[:, D_EXPERT:]).astype(BF16)
        _store_token_major(ybuf, _dot(hid, w2_ref[...].astype(BF16)))
        base = first_ref[b]
        n_real = cnt_ref[b]
        for r in range(rows):
            dst = jnp.where(r < n_real, dst_ref[base + r], spare_row + r)
            pltpu.make_async_copy(ybuf.at[pl.ds(r * pitch, pitch), :], token_rows(y_hbm, dst), ssem).start()

        @pl.when(b + 1 >= n_used)
        def _():
            all_rows_scattered().wait()


def _expert_slice(b, hh):
    return jnp.where(b % 2 == 0, hh, EXPERT_SLICES - 1 - hh)


def _experts(h2, plan, t_real, w1, w3, w2):
    tp = h2.shape[0] // TOKEN_PITCH
    assert tp - t_real <= MOE_BLOCK
    block_expert, n_used, first, cnt, tok_sorted, dst_sorted = plan
    d = D_MODEL
    nb = block_expert.shape[0]
    dk = d // EXPERT_SLICES

    def eidx(b, be, nu):
        return be[jnp.minimum(b, nu[0] - 1)]

    def sidx(b, hh, nu):
        live = b < nu[0]
        return _expert_slice(jnp.minimum(b, nu[0] - 1), jnp.where(live, hh, EXPERT_SLICES - 1))

    grid_spec = pltpu.PrefetchScalarGridSpec(
        num_scalar_prefetch=6,
        grid=(nb, EXPERT_SLICES),
        in_specs=[pl.BlockSpec(memory_space=pl.ANY),
                  pl.BlockSpec((None, dk, D_EXPERT), lambda b, hh, be, nu, *_: (eidx(b, be, nu), sidx(b, hh, nu), 0)),
                  pl.BlockSpec((None, dk, D_EXPERT), lambda b, hh, be, nu, *_: (eidx(b, be, nu), sidx(b, hh, nu), 0)),
                  pl.BlockSpec((None, D_EXPERT, d), lambda b, hh, be, nu, *_: (eidx(b, be, nu), 0, 0))],
        out_specs=pl.BlockSpec(memory_space=pl.ANY),
        scratch_shapes=[pltpu.VMEM((2, MOE_BLOCK * TOKEN_PITCH, LANES), F32),
                        pltpu.VMEM((EXPERT_SLICES, MOE_BLOCK, dk), BF16),
                        pltpu.VMEM((MOE_BLOCK, 2 * D_EXPERT), F32),
                        pltpu.VMEM((MOE_BLOCK * TOKEN_PITCH, LANES), F32),
                        pltpu.SemaphoreType.DMA((2,)),
                        pltpu.SemaphoreType.DMA(())],
    )
    return pl.pallas_call(
        functools.partial(_expert_kernel, nb=nb, plane_rows=tp, plane_pad=tp - t_real),
        grid_spec=grid_spec,
        out_shape=jax.ShapeDtypeStruct(((TOP_K * tp + MOE_BLOCK) * TOKEN_PITCH, LANES), F32),
        compiler_params=_cparams(("arbitrary", "arbitrary"), 60 * 1024 * 1024),
        name="expert_mlp",
    )(block_expert, n_used, first, cnt, tok_sorted, dst_sorted, h2, w1, w3, w2)


def _route_plan(expert, t_real, tp):
    s = t_real * TOP_K
    n_blocks = -(-(s + N_EXPERTS * (MOE_BLOCK - 1)) // MOE_BLOCK)
    flat_e = expert.reshape(-1).astype(jnp.int32)
    se, order = lax.sort((flat_e, jnp.arange(s, dtype=jnp.int32)), num_keys=1, is_stable=True)
    bounds = jnp.searchsorted(se, jnp.arange(N_EXPERTS + 1, dtype=jnp.int32)).astype(jnp.int32)
    start = bounds[:-1]
    counts = bounds[1:] - start
    padded = (counts + MOE_BLOCK - 1) // MOE_BLOCK * MOE_BLOCK
    pad_end = jnp.cumsum(padded)
    pad_start = pad_end - padded
    block_start = jnp.arange(n_blocks, dtype=jnp.int32) * MOE_BLOCK
    block_expert = jnp.minimum(jnp.searchsorted(pad_end, block_start, side='right'),
                               N_EXPERTS - 1).astype(jnp.int32)
    n_used = (pad_end[-1] // MOE_BLOCK).astype(jnp.int32).reshape(1)
    cnt = jnp.clip(counts[block_expert] - (block_start - pad_start[block_expert]), 0, MOE_BLOCK)
    first = jnp.clip(start[block_expert] + block_start - pad_start[block_expert], 0, s).astype(jnp.int32)
    tail = jnp.zeros((MOE_BLOCK,), jnp.int32)
    tok = order // TOP_K
    tok_sorted = jnp.concatenate([tok, tail])
    dst_sorted = jnp.concatenate([(order % TOP_K) * tp + tok, tail])
    return block_expert, n_used, first, cnt.astype(jnp.int32), tok_sorted, dst_sorted


def _combine_kernel(x_ref, g_ref, y0_ref, y1_ref, o_ref):
    n = x_ref.shape[0]
    g = g_ref[...]
    y = (g[:, 0:1] * _load_token_major(y0_ref, n, F32)
         + g[:, 1:2] * _load_token_major(y1_ref, n, F32))
    o_ref[...] = x_ref[...] + y


def _combine(x2, gates, y2, row_off, n_rows, tile):
    tp, d = x2.shape
    off = row_off // tile
    plane = tp // tile
    return pl.pallas_call(
        _combine_kernel,
        grid=(n_rows // tile,),
        in_specs=[pl.BlockSpec((tile, d), lambda i: (off + i, 0)),
                  pl.BlockSpec((tile, LANES), lambda i: (off + i, 0)),
                  pl.BlockSpec((tile * TOKEN_PITCH, LANES), lambda i: (off + i, 0)),
                  pl.BlockSpec((tile * TOKEN_PITCH, LANES), lambda i: (plane + off + i, 0))],
        out_specs=pl.BlockSpec((tile, d), lambda i: (i, 0)),
        out_shape=jax.ShapeDtypeStruct((n_rows, d), F32),
        compiler_params=_cparams(("parallel",)),
        name="moe_combine",
    )(x2, gates, y2, y2)


def kernel(x_prompt, x_sample, cache_k, cache_v, state_ssm_re, state_ssm_im, meta_tokens, norm1_g, w_in, b_in, ssm_a_re, ssm_a_im, ssm_log_dt, ssm_b_re, ssm_b_im, ssm_c_re, ssm_c_im, ssm_d, w_glu, b_glu, w_ssm_proj, q_norm_g, k_norm_g, lam_q1, lam_k1, lam_q2, lam_k2, subln_g, w_att_proj, w_o, norm2_g, w_router_group, b_router_group, w_router_expert, b_router_expert, w1_e, w3_e, w2_e):
    assert x_prompt.shape[0] == 1 and w_in.shape[0] == 1
    seq = x_prompt.shape[1]
    nb, nq = x_sample.shape[0], x_sample.shape[1]
    past = cache_k.shape[2]
    n_s = nb * nq
    t_real = seq + N_META + n_s
    tp = -(-t_real // ROW_ALIGN) * ROW_ALIGN
    off_meta, off_s = seq, seq + N_META
    tq = 256
    assert seq % tq == 0 and nq == N_META and past % 512 == 0
    lam_init = 0.8 - 0.6 * math.exp(-0.3 * 0)
    out_scale = 1.0 - lam_init

    x_tail = jnp.concatenate([meta_tokens.astype(F32), x_sample.reshape(n_s, D_MODEL),
                              jnp.zeros((tp - t_real, D_MODEL), F32)], axis=0)

    tm = _row_tile(tp, 1088)
    tr = _row_tile(tp, 256)
    assert seq % tr == 0

    h1, x_cat = _rmsnorm(x_prompt[0], x_tail, norm1_g[0], tr)
    z = _inproj(h1, w_in[0], b_in[0], tm, 512)

    pos = jnp.concatenate([N_META + jnp.arange(seq), jnp.arange(N_META),
                           jnp.tile(past + jnp.arange(nq), nb),
                           jnp.zeros((tp - t_real,), jnp.int32)]).astype(F32)
    half = HEAD_DIM // 2
    inv = ROPE_THETA ** (-jnp.arange(half, dtype=F32) / half)
    ang = pos[:, None] * inv[None, :]
    cos_t = jnp.tile(jnp.cos(ang), (1, LANES // half))
    sin_h = jnp.sin(ang)
    sin_t = jnp.tile(jnp.concatenate([-sin_h, sin_h], axis=1), (1, LANES // HEAD_DIM))
    gq = jnp.tile(q_norm_g[0], LANES // HEAD_DIM).reshape(1, LANES)
    gk = jnp.tile(k_norm_g[0], LANES // HEAD_DIM).reshape(1, LANES)
    qb, kf, kb, vf, vb = _qk_rope(z, cos_t, sin_t, gq, gk, tr)

    lam = (jnp.exp(jnp.sum(lam_q1[0] * lam_k1[0])) - jnp.exp(jnp.sum(lam_q2[0] * lam_k2[0])) + lam_init)
    lam_row = jnp.full((1, LANES), lam, F32)
    sg = subln_g[0].reshape(1, LANES)

    q_s = qb[off_s:off_s + n_s].reshape(nb, nq, N_HEADS, LANES).transpose(0, 2, 1, 3)
    lane = jnp.arange(LANES)
    qz = jnp.concatenate([jnp.where(lane < HEAD_DIM, q_s, 0), jnp.where(lane >= HEAD_DIM, q_s, 0)], axis=2)
    pad_new = lambda a: jnp.pad(a[off_s:off_s + n_s].reshape(nb, nq, D_ATT), ((0, 0), (0, LANES - nq), (0, 0)))
    ck = cache_k[0].reshape(nb, past, D_ATT)
    cv = cache_v[0].reshape(nb, past * N_HEADS, V_DIM)

    logit_bound = 8.1 * jnp.max(jnp.abs(q_norm_g[0])) * jnp.max(jnp.abs(k_norm_g[0]))

    def attention(fast):
        def run():
            if fast:
                o = _attn_prompt_fast(qb, kb, vb, lam_row, sg, seq, out_scale, 512)
            else:
                o = _attn_prompt(qb, kb, vb, lam_row, sg, seq, out_scale, tq)
            return _attn_sample(qz, ck, cv, pad_new(kb), pad_new(vb), o, lam_row, sg, off_s, out_scale,
                                512, fast)
        return run

    o_att = lax.cond(logit_bound <= LOGIT_BOUND_MAX, attention(True), attention(False))

    gp = N_SSM_GROUPS * SSM_STATE
    s5w = _s5_weights(ssm_a_re[0], ssm_a_im[0], ssm_log_dt[0], ssm_b_re[0], ssm_b_im[0],
                      ssm_c_re[0], ssm_c_im[0])
    ys, hp_re, hp_im, hs_re, hs_im = _s5(z, s5w, ssm_d[0], state_ssm_re[0].reshape(nb, gp),
                                         state_ssm_im[0].reshape(nb, gp), seq, nb, nq)
    ysg = _glu(ys, w_glu[0], b_glu[0], tm, 512)
    m = _merge(ysg, o_att, w_ssm_proj[0], w_att_proj[0], z, tm, 512)
    x2 = _outproj(m, w_o[0], x_cat, tm, 512)

    w_r = jnp.concatenate([w_router_group[0], w_router_expert[0],
                           jnp.zeros((D_MODEL, LANES - N_EGROUPS - N_EXPERTS), F32)], axis=1).astype(BF16)
    b_r = jnp.concatenate([b_router_group[0], b_router_expert[0],
                           jnp.zeros((LANES - N_EGROUPS - N_EXPERTS,), F32)]).reshape(1, LANES)
    h2, e_sel, g_sel = _router(x2, norm2_g[0], w_r, b_r, tr)

    plan = _route_plan(e_sel[:t_real, :TOP_K], t_real, tp)
    y2 = _experts(h2, plan, t_real, w1_e[0], w3_e[0], w2_e[0])

    def heads(a, lead):
        return a.reshape(lead + (N_HEADS, 2, HEAD_DIM))

    y_prompt = _combine(x2, g_sel, y2, 0, seq, LANES).reshape(1, seq, D_MODEL)
    y_sample = _combine(x2, g_sel, y2, off_s, n_s, nq).reshape(nb, nq, D_MODEL)
    k_p = jnp.concatenate([kf[off_meta:off_meta + N_META], kf[:seq]], axis=0)
    v_p = jnp.concatenate([vf[off_meta:off_meta + N_META], vf[:seq]], axis=0)
    k_prompt = heads(k_p, (1, 1, seq + N_META))
    v_prompt = v_p.reshape(1, 1, seq + N_META, N_HEADS, V_DIM)
    k_sample = heads(kf[off_s:off_s + n_s], (1, nb, nq))
    v_sample = vf[off_s:off_s + n_s].reshape(1, nb, nq, N_HEADS, V_DIM)
    st = lambda a, lead: a.reshape(lead + (N_SSM_GROUPS, SSM_STATE))
    return (y_prompt, y_sample, k_prompt, v_prompt, st(hp_re, (1, 1)), st(hp_im, (1, 1)),
            k_sample, v_sample, st(hs_re, (1, nb)), st(hs_im, (1, nb)))
```

```python
import functools
import math

import jax
import jax.numpy as jnp
from jax import lax
from jax.experimental import pallas as pl
from jax.experimental.pallas import tpu as pltpu

F32 = jnp.float32
BF16 = jnp.bfloat16

D_MODEL = 4096
N_META = 16
CHUNK = 64
N_HEADS = 16
HEAD_DIM = 64
V_DIM = 128
D_ATT = N_HEADS * V_DIM
D_SSM = 2048
SSM_GROUP = 16
N_SSM_GROUPS = D_SSM // SSM_GROUP
SSM_STATE = 64
IN_WIDTH = D_SSM + 3 * D_ATT + 2 * D_MODEL
ROPE_THETA = 10000.0
N_EGROUPS = 8
EXPERTS_PER_GROUP = 8
N_EXPERTS = N_EGROUPS * EXPERTS_PER_GROUP
TOP_K = 2
D_EXPERT = 512
MOE_BLOCK = 320
EXPERT_SLICES = 4
EPS = 1e-6

LANES = 128
ROW_ALIGN = 512
S5_CHUNK = 8
S5_LANE_GROUPS = LANES // SSM_GROUP
S5_TILES = D_SSM // LANES
VMEM_LIMIT = 56 * 1024 * 1024


def _cparams(sem, vmem=VMEM_LIMIT):
    return pltpu.CompilerParams(dimension_semantics=sem, vmem_limit_bytes=vmem)


def _row_tile(tp, cap):
    best = 16
    for t in range(16, cap + 1, 16):
        if tp % t == 0:
            best = t
    return best


def _dot(a, b):
    return jnp.dot(a, b, preferred_element_type=F32)


def _dot_nt(a, b):
    return lax.dot_general(a, b, (((1,), (1,)), ((), ())), preferred_element_type=F32)


def _rmsnorm_kernel(xa_ref, xb_ref, g_ref, o_ref, xc_ref, *, n_a):
    def emit(x):
        ms = jnp.mean(x * x, axis=-1, keepdims=True)
        o_ref[...] = (x * lax.rsqrt(ms + EPS) * g_ref[...]).astype(o_ref.dtype)
        xc_ref[...] = x

    @pl.when(pl.program_id(0) < n_a)
    def _():
        emit(xa_ref[...])

    @pl.when(pl.program_id(0) >= n_a)
    def _():
        emit(xb_ref[...])


def _rmsnorm(x_a, x_b, g, tr):
    d = x_a.shape[1]
    n_a, n_b = x_a.shape[0] // tr, x_b.shape[0] // tr
    tp = x_a.shape[0] + x_b.shape[0]
    row = pl.BlockSpec((tr, d), lambda i: (i, 0))
    return pl.pallas_call(
        functools.partial(_rmsnorm_kernel, n_a=n_a),
        grid=(n_a + n_b,),
        in_specs=[pl.BlockSpec((tr, d), lambda i: (jnp.minimum(i, n_a - 1), 0)),
                  pl.BlockSpec((tr, d), lambda i: (jnp.maximum(i - n_a, 0), 0)),
                  pl.BlockSpec((1, d), lambda i: (0, 0))],
        out_specs=[row, row],
        out_shape=[jax.ShapeDtypeStruct((tp, d), BF16), jax.ShapeDtypeStruct((tp, d), F32)],
        compiler_params=_cparams(("arbitrary",)),
        name="rmsnorm1",
    )(x_a, x_b, g.reshape(1, d))


def _inproj_kernel(x_ref, w_ref, b_ref, o_ref):
    o_ref[...] = _dot(x_ref[...], w_ref[...].astype(BF16)) + b_ref[...]


def _inproj(h, w, b, tm, tn):
    tp, k = h.shape
    n = w.shape[1]
    return pl.pallas_call(
        _inproj_kernel,
        grid=(n // tn, tp // tm),
        in_specs=[pl.BlockSpec((tm, k), lambda j, i: (i, 0)),
                  pl.BlockSpec((k, tn), lambda j, i: (0, j)),
                  pl.BlockSpec((1, tn), lambda j, i: (0, j))],
        out_specs=pl.BlockSpec((tm, tn), lambda j, i: (i, j)),
        out_shape=jax.ShapeDtypeStruct((tp, n), F32),
        compiler_params=_cparams(("parallel", "parallel")),
        name="in_proj",
    )(h, w, b.reshape(1, n))


def _glu_kernel(x_ref, w_ref, b_ref, xe_ref, o_ref):
    a = _dot(x_ref[...].astype(BF16), w_ref[...].astype(BF16)) + b_ref[...]
    o_ref[...] = (xe_ref[...] * jax.nn.sigmoid(a)).astype(o_ref.dtype)


def _glu(ys, w, b, tm, tn):
    tp, k = ys.shape
    n = w.shape[1]
    return pl.pallas_call(
        _glu_kernel,
        grid=(n // tn, tp // tm),
        in_specs=[pl.BlockSpec((tm, k), lambda j, i: (i, 0)),
                  pl.BlockSpec((k, tn), lambda j, i: (0, j)),
                  pl.BlockSpec((1, tn), lambda j, i: (0, j)),
                  pl.BlockSpec((tm, tn), lambda j, i: (i, j))],
        out_specs=pl.BlockSpec((tm, tn), lambda j, i: (i, j)),
        out_shape=jax.ShapeDtypeStruct((tp, n), BF16),
        compiler_params=_cparams(("parallel", "parallel")),
        name="glu",
    )(ys, w, b.reshape(1, n), ys)


def _merge_kernel(ys_ref, oa_ref, ws_ref, wa_ref, gs_ref, ga_ref, o_ref):
    a = _dot(ys_ref[...], ws_ref[...].astype(BF16))
    b = _dot(oa_ref[...], wa_ref[...].astype(BF16))
    m = jax.nn.sigmoid(gs_ref[...]) * a + jax.nn.sigmoid(ga_ref[...]) * b
    o_ref[...] = m.astype(o_ref.dtype)


def _merge(ysg, oatt, w_ssm, w_att, z, tm, tn):
    tp, k = ysg.shape
    n = w_ssm.shape[1]
    gs_blk = (D_SSM + 3 * D_ATT) // tn
    ga_blk = (D_SSM + 3 * D_ATT + D_MODEL) // tn
    return pl.pallas_call(
        _merge_kernel,
        grid=(n // tn, tp // tm),
        in_specs=[pl.BlockSpec((tm, k), lambda j, i: (i, 0)),
                  pl.BlockSpec((tm, k), lambda j, i: (i, 0)),
                  pl.BlockSpec((k, tn), lambda j, i: (0, j)),
                  pl.BlockSpec((k, tn), lambda j, i: (0, j)),
                  pl.BlockSpec((tm, tn), lambda j, i: (i, gs_blk + j)),
                  pl.BlockSpec((tm, tn), lambda j, i: (i, ga_blk + j))],
        out_specs=pl.BlockSpec((tm, tn), lambda j, i: (i, j)),
        out_shape=jax.ShapeDtypeStruct((tp, n), BF16),
        compiler_params=_cparams(("parallel", "parallel")),
        name="merge_proj",
    )(ysg, oatt, w_ssm, w_att, z, z)


def _outproj_kernel(m_ref, w_ref, x_ref, o_ref):
    o_ref[...] = x_ref[...] + _dot(m_ref[...], w_ref[...].astype(BF16))


def _outproj(m, w, x, tm, tn):
    tp, k = m.shape
    n = w.shape[1]
    return pl.pallas_call(
        _outproj_kernel,
        grid=(n // tn, tp // tm),
        in_specs=[pl.BlockSpec((tm, k), lambda j, i: (i, 0)),
                  pl.BlockSpec((k, tn), lambda j, i: (0, j)),
                  pl.BlockSpec((tm, tn), lambda j, i: (i, j))],
        out_specs=pl.BlockSpec((tm, tn), lambda j, i: (i, j)),
        out_shape=jax.ShapeDtypeStruct((tp, n), F32),
        compiler_params=_cparams(("parallel", "parallel")),
        name="out_proj",
    )(m, w, x)


def _segment_sumsq(x, ones_bd):
    x2 = x * x
    hi = x2.astype(BF16)
    lo = (x2 - hi.astype(F32)).astype(BF16)
    return _dot(hi, ones_bd) + _dot(lo, ones_bd)


def _qk_rope_kernel(zq_ref, zk_ref, zv_ref, cos_ref, sin_ref, gq_ref, gk_ref, ones_ref,
                    qb_ref, kf_ref, kb_ref, vf_ref, vb_ref):
    cos = cos_ref[...]
    sin = sin_ref[...]
    ones_bd = ones_ref[...]
    lane = lax.broadcasted_iota(jnp.int32, cos.shape, 1)
    first_half = (lane % HEAD_DIM) < (HEAD_DIM // 2)

    def norm_rope(x, g):
        ss = _segment_sumsq(x, ones_bd)
        xn = x * lax.rsqrt(ss * (1.0 / HEAD_DIM) + EPS) * g
        partner = jnp.where(first_half,
                            pltpu.roll(xn, LANES - HEAD_DIM // 2, 1),
                            pltpu.roll(xn, HEAD_DIM // 2, 1))
        return xn * cos + partner * sin

    for h in range(N_HEADS):
        sl = slice(h * LANES, (h + 1) * LANES)
        q = norm_rope(zq_ref[:, sl], gq_ref[...])
        qb_ref[:, sl] = (q * (HEAD_DIM ** -0.5)).astype(BF16)
        k = norm_rope(zk_ref[:, sl], gk_ref[...])
        kf_ref[:, sl] = k
        kb_ref[:, sl] = k.astype(BF16)
    v = zv_ref[...]
    vf_ref[...] = v
    vb_ref[...] = v.astype(BF16)


def _qk_rope(z, cos_t, sin_t, gq, gk, tr):
    tp = z.shape[0]
    ones_bd = jnp.kron(jnp.eye(LANES // HEAD_DIM, dtype=F32),
                       jnp.ones((HEAD_DIM, HEAD_DIM), F32)).astype(BF16)
    zspec = lambda c: pl.BlockSpec((tr, D_ATT), lambda i: (i, c))
    row = pl.BlockSpec((tr, LANES), lambda i: (i, 0))
    const = pl.BlockSpec((1, LANES), lambda i: (0, 0))
    out = pl.BlockSpec((tr, D_ATT), lambda i: (i, 0))
    q_blk = D_SSM // D_ATT
    return pl.pallas_call(
        _qk_rope_kernel,
        grid=(tp // tr,),
        in_specs=[zspec(q_blk), zspec(q_blk + 1), zspec(q_blk + 2), row, row, const, const,
                  pl.BlockSpec((LANES, LANES), lambda i: (0, 0))],
        out_specs=[out, out, out, out, out],
        out_shape=[jax.ShapeDtypeStruct((tp, D_ATT), BF16),
                   jax.ShapeDtypeStruct((tp, D_ATT), F32),
                   jax.ShapeDtypeStruct((tp, D_ATT), BF16),
                   jax.ShapeDtypeStruct((tp, D_ATT), F32),
                   jax.ShapeDtypeStruct((tp, D_ATT), BF16)],
        compiler_params=_cparams(("parallel",)),
        name="qk_norm_rope",
    )(z, z, z, cos_t, sin_t, gq, gk, ones_bd)


def _softmax_step(c, qc, kt, vt, mask, m_ref, l_ref, acc_ref):
    s = _dot_nt(qc, kt)
    if mask is not None:
        s = jnp.where(mask, s, -jnp.inf)
    m_prev = m_ref[c]
    m_new = jnp.maximum(m_prev, jnp.max(s, axis=1, keepdims=True))
    alpha = jnp.exp(m_prev - m_new)
    p = jnp.exp(s - m_new[:, :1])
    l_ref[c] = alpha * l_ref[c] + jnp.sum(p, axis=1, keepdims=True)
    acc_ref[c] = alpha * acc_ref[c] + _dot(p.astype(BF16), vt)
    m_ref[c] = m_new


def _diff_finish(o0, o1, lam, g, out_scale):
    o = o0 - lam * o1
    ms = jnp.mean(o * o, axis=-1, keepdims=True)
    return o * lax.rsqrt(ms + EPS) * g * out_scale


def _split_components(q):
    lane = lax.broadcasted_iota(jnp.int32, q.shape, 1)
    zero = jnp.zeros_like(q)
    return jnp.where(lane < HEAD_DIM, q, zero), jnp.where(lane >= HEAD_DIM, q, zero)


def _attn_prompt_kernel(lam_ref, g_ref, q_ref, k_ref, v_ref, o_ref, m_ref, l_ref, acc_ref,
                        *, tq, nq_main, seq, out_scale):
    i = pl.program_id(1)
    q0, q1 = _split_components(q_ref[...])
    m_ref[...] = jnp.full(m_ref.shape, -jnp.inf, F32)
    l_ref[...] = jnp.zeros(l_ref.shape, F32)
    acc_ref[...] = jnp.zeros(acc_ref.shape, F32)

    def update(kt, vt, mask):
        _softmax_step(0, q0, kt, vt, mask, m_ref, l_ref, acc_ref)
        _softmax_step(1, q1, kt, vt, mask, m_ref, l_ref, acc_ref)

    col = lax.broadcasted_iota(jnp.int32, (tq, LANES), 1)
    update(k_ref[pl.ds(seq, LANES), :], v_ref[pl.ds(seq, LANES), :], col < N_META)

    is_main = i < nq_main

    def body(j, carry):
        start = pl.multiple_of(j * tq, tq)
        update(k_ref[pl.ds(start, tq), :], v_ref[pl.ds(start, tq), :], None)
        return carry

    lax.fori_loop(0, jnp.where(is_main, i, 0), body, 0)

    @pl.when(is_main)
    def _():
        start = pl.multiple_of(i * tq, tq)
        r = lax.broadcasted_iota(jnp.int32, (tq, tq), 0) // CHUNK
        c = lax.broadcasted_iota(jnp.int32, (tq, tq), 1) // CHUNK
        update(k_ref[pl.ds(start, tq), :], v_ref[pl.ds(start, tq), :], c <= r)

    o0 = acc_ref[0] / l_ref[0]
    o1 = acc_ref[1] / l_ref[1]
    o_ref[...] = _diff_finish(o0, o1, lam_ref[...], g_ref[...], out_scale).astype(o_ref.dtype)


def _attn_prompt(qb, kb, vb, lam_row, subln_g, seq, out_scale, tq):
    tp = qb.shape[0]
    kern = functools.partial(_attn_prompt_kernel, tq=tq, nq_main=seq // tq, seq=seq,
                             out_scale=out_scale)
    const = pl.BlockSpec((1, LANES), lambda h, i: (0, 0))
    return pl.pallas_call(
        kern,
        grid=(N_HEADS, tp // tq),
        in_specs=[const, const,
                  pl.BlockSpec((tq, LANES), lambda h, i: (i, h)),
                  pl.BlockSpec((tp, LANES), lambda h, i: (0, h)),
                  pl.BlockSpec((tp, LANES), lambda h, i: (0, h))],
        out_specs=pl.BlockSpec((tq, LANES), lambda h, i: (i, h)),
        out_shape=jax.ShapeDtypeStruct((tp, D_ATT), BF16),
        scratch_shapes=[pltpu.VMEM((2, tq, LANES), F32),
                        pltpu.VMEM((2, tq, LANES), F32),
                        pltpu.VMEM((2, tq, LANES), F32)],
        compiler_params=_cparams(("parallel", "parallel")),
        name="attn_prompt",
    )(lam_row, subln_g, qb, kb, vb)


LOGIT_BOUND_MAX = 40.0
ATTN_WIDE = 4


def _with_ones(vt):
    return jnp.concatenate([vt, jnp.ones(vt.shape, vt.dtype)], axis=1)


def _cache_v_head(vc_ref, h):
    tk = vc_ref.shape[0] // N_HEADS
    return vc_ref[pl.ds(h, tk, stride=N_HEADS), :].astype(BF16)


def _attn_prompt_fast_kernel(lam_ref, g_ref, q_ref, k_ref, v_ref, o_ref, acc_ref,
                             *, tq, nq_main, seq, out_scale):
    i = pl.program_id(1)
    q0, q1 = _split_components(q_ref[...])
    qq = jnp.concatenate([q0, q1], axis=0)

    def scores(start, rows):
        return _dot_nt(qq, k_ref[pl.ds(start, rows), :])

    def weighted(s, start, rows, mask):
        p = jnp.exp(s)
        if mask is not None:
            p = jnp.where(mask, p, 0.0)
        return _dot(p.astype(BF16), _with_ones(v_ref[pl.ds(start, rows), :]))

    def tile_pv(start, rows, mask):
        return weighted(scores(start, rows), start, rows, mask)

    is_main = i < nq_main
    n_full = jnp.where(is_main, i, 0)

    @pl.when(jnp.logical_not(is_main))
    def _():
        col = lax.broadcasted_iota(jnp.int32, (2 * tq, LANES), 1)
        acc_ref[...] = tile_pv(seq, LANES, col < N_META)

    @pl.when(is_main)
    def _():
        start = pl.multiple_of(i * tq, tq)
        kt = jnp.concatenate([k_ref[pl.ds(start, tq), :], k_ref[pl.ds(seq, LANES), :]], axis=0)
        vt = jnp.concatenate([v_ref[pl.ds(start, tq), :], v_ref[pl.ds(seq, LANES), :]], axis=0)
        r = (lax.broadcasted_iota(jnp.int32, (2 * tq, tq + LANES), 0) % tq) // CHUNK
        c = lax.broadcasted_iota(jnp.int32, (2 * tq, tq + LANES), 1)
        mask = ((c < tq) & (c // CHUNK <= r)) | ((c >= tq) & (c < tq + N_META))
        p = jnp.where(mask, jnp.exp(_dot_nt(qq, kt)), 0.0)
        acc_ref[...] = _dot(p.astype(BF16), _with_ones(vt))

    wide = ATTN_WIDE

    def body(j, carry):
        start = pl.multiple_of(wide * j * tq, tq)
        acc_ref[...] += tile_pv(start, wide * tq, None)
        return carry

    n_wide = n_full // wide
    lax.fori_loop(0, n_wide, body, 0)
    rem = n_full - wide * n_wide
    base = wide * n_wide

    @pl.when(rem >= 2)
    def _():
        acc_ref[...] += tile_pv(pl.multiple_of(base * tq, tq), 2 * tq, None)

    @pl.when(rem % 2 == 1)
    def _():
        acc_ref[...] += tile_pv(pl.multiple_of((n_full - 1) * tq, tq), tq, None)

    acc = acc_ref[...]
    o0 = acc[:tq, :LANES] / acc[:tq, LANES:]
    o1 = acc[tq:, :LANES] / acc[tq:, LANES:]
    o_ref[...] = _diff_finish(o0, o1, lam_ref[...], g_ref[...], out_scale).astype(o_ref.dtype)


def _attn_prompt_fast(qb, kb, vb, lam_row, subln_g, seq, out_scale, tq):
    tp = qb.shape[0]
    kern = functools.partial(_attn_prompt_fast_kernel, tq=tq, nq_main=seq // tq, seq=seq,
                             out_scale=out_scale)
    const = pl.BlockSpec((1, LANES), lambda h, i: (0, 0))
    return pl.pallas_call(
        kern,
        grid=(N_HEADS, tp // tq),
        in_specs=[const, const,
                  pl.BlockSpec((tq, LANES), lambda h, i: (i, h)),
                  pl.BlockSpec((tp, LANES), lambda h, i: (0, h)),
                  pl.BlockSpec((tp, LANES), lambda h, i: (0, h))],
        out_specs=pl.BlockSpec((tq, LANES), lambda h, i: (i, h)),
        out_shape=jax.ShapeDtypeStruct((tp, D_ATT), BF16),
        scratch_shapes=[pltpu.VMEM((2 * tq, 2 * LANES), F32)],
        compiler_params=_cparams(("parallel", "parallel")),
        name="attn_prompt_fast",
    )(lam_row, subln_g, qb, kb, vb)


def _attn_sample_fast_kernel(lam_ref, g_ref, q_ref, kc_ref, vc_ref, kn_ref, vn_ref, o_in_ref, o_ref,
                             acc_ref, *, nq, out_scale):
    del o_in_ref
    j = pl.program_id(1)
    rows = 2 * nq

    def head_update(h, kt, vt, mask):
        p = jnp.exp(_dot_nt(q_ref[h], kt))
        if mask is not None:
            p = jnp.where(mask, p, 0.0)
        acc_ref[h] += _dot(p.astype(BF16), _with_ones(vt))

    @pl.when(j == 0)
    def _():
        acc_ref[...] = jnp.zeros(acc_ref.shape, F32)
        col = lax.broadcasted_iota(jnp.int32, (rows, LANES), 1)
        for h in range(N_HEADS):
            sl = slice(h * LANES, (h + 1) * LANES)
            head_update(h, kn_ref[:, sl], vn_ref[:, sl], col < nq)

    for h in range(N_HEADS):
        sl = slice(h * LANES, (h + 1) * LANES)
        head_update(h, kc_ref[:, sl].astype(BF16), _cache_v_head(vc_ref, h), None)

    @pl.when(j == pl.num_programs(1) - 1)
    def _():
        for h in range(N_HEADS):
            acc = acc_ref[h]
            o = acc[:, :LANES] / acc[:, LANES:]
            res = _diff_finish(o[:nq], o[nq:], lam_ref[...], g_ref[...], out_scale)
            o_ref[:, h * LANES:(h + 1) * LANES] = res.astype(o_ref.dtype)


def _attn_sample_kernel(lam_ref, g_ref, q_ref, kc_ref, vc_ref, kn_ref, vn_ref, o_in_ref, o_ref,
                        m_ref, l_ref, acc_ref, *, nq, out_scale):
    del o_in_ref
    j = pl.program_id(1)
    rows = 2 * nq

    @pl.when(j == 0)
    def _():
        m_ref[...] = jnp.full(m_ref.shape, -jnp.inf, F32)
        l_ref[...] = jnp.zeros(l_ref.shape, F32)
        acc_ref[...] = jnp.zeros(acc_ref.shape, F32)
        col = lax.broadcasted_iota(jnp.int32, (rows, LANES), 1)
        for h in range(N_HEADS):
            sl = slice(h * LANES, (h + 1) * LANES)
            _softmax_step(h, q_ref[h], kn_ref[:, sl], vn_ref[:, sl], col < nq,
                          m_ref, l_ref, acc_ref)

    for h in range(N_HEADS):
        sl = slice(h * LANES, (h + 1) * LANES)
        _softmax_step(h, q_ref[h], kc_ref[:, sl].astype(BF16), _cache_v_head(vc_ref, h), None,
                      m_ref, l_ref, acc_ref)

    @pl.when(j == pl.num_programs(1) - 1)
    def _():
        for h in range(N_HEADS):
            o = acc_ref[h] / l_ref[h]
            res = _diff_finish(o[:nq], o[nq:], lam_ref[...], g_ref[...], out_scale)
            o_ref[:, h * LANES:(h + 1) * LANES] = res.astype(o_ref.dtype)


def _attn_sample(qz, cache_k, cache_v, k_new, v_new, o_buf, lam_row, subln_g, row_off, out_scale, tk, fast):
    nb, past = cache_k.shape[0], cache_k.shape[1]
    nq = qz.shape[2] // 2
    if fast:
        kern = functools.partial(_attn_sample_fast_kernel, nq=nq, out_scale=out_scale)
        scratch = [pltpu.VMEM((N_HEADS, 2 * nq, 2 * LANES), F32)]
    else:
        kern = functools.partial(_attn_sample_kernel, nq=nq, out_scale=out_scale)
        scratch = [pltpu.VMEM((N_HEADS, 2 * nq, LANES), F32)] * 3
    const = pl.BlockSpec((1, LANES), lambda b, j: (0, 0))
    blk_off = row_off // nq
    return pl.pallas_call(
        kern,
        grid=(nb, past // tk),
        in_specs=[const, const,
                  pl.BlockSpec((None, N_HEADS, 2 * nq, LANES), lambda b, j: (b, 0, 0, 0)),
                  pl.BlockSpec((None, tk, D_ATT), lambda b, j: (b, j, 0)),
                  pl.BlockSpec((None, tk * N_HEADS, V_DIM), lambda b, j: (b, j, 0)),
                  pl.BlockSpec((None, LANES, D_ATT), lambda b, j: (b, 0, 0)),
                  pl.BlockSpec((None, LANES, D_ATT), lambda b, j: (b, 0, 0)),
                  pl.BlockSpec(memory_space=pl.ANY)],
        out_specs=pl.BlockSpec((nq, D_ATT), lambda b, j: (blk_off + b, 0)),
        out_shape=jax.ShapeDtypeStruct(o_buf.shape, o_buf.dtype),
        scratch_shapes=scratch,
        input_output_aliases={7: 0},
        compiler_params=_cparams(("parallel", "arbitrary")),
        name="attn_sample_fast" if fast else "attn_sample",
    )(lam_row, subln_g, qz, cache_k, cache_v, k_new, v_new, o_buf)


def _s5_weights(a_re, a_im, log_dt, b_re, b_im, c_re, c_im):
    hp = lax.Precision.HIGHEST
    n_t, gl, tc = S5_TILES, S5_LANE_GROUPS, S5_CHUNK
    dt = jnp.exp(log_dt)[:, None]
    mag = jnp.exp(a_re * dt)
    abar_re = mag * jnp.cos(a_im * dt)
    abar_im = mag * jnp.sin(a_im * dt)
    nr, ni = abar_re - 1.0, abar_im
    den = a_re * a_re + a_im * a_im
    coef_re = (nr * a_re + ni * a_im) / den
    coef_im = (ni * a_re - nr * a_im) / den
    bbar_re = coef_re[..., None] * b_re - coef_im[..., None] * b_im
    bbar_im = coef_re[..., None] * b_im + coef_im[..., None] * b_re
    n = jnp.arange(tc + 1, dtype=F32)[:, None, None]
    pw_mag = jnp.exp(n * (a_re * dt))
    pw_re = pw_mag * jnp.cos(n * (a_im * dt))
    pw_im = pw_mag * jnp.sin(n * (a_im * dt))
    e_re = pw_re[:tc, :, :, None] * bbar_re - pw_im[:tc, :, :, None] * bbar_im
    e_im = pw_re[:tc, :, :, None] * bbar_im + pw_im[:tc, :, :, None] * bbar_re
    kern = (jnp.einsum('gcp,lgpd->glcd', c_re, e_re, precision=hp)
            - jnp.einsum('gcp,lgpd->glcd', c_im, e_im, precision=hp))
    eye = jnp.eye(gl, dtype=F32)
    w_intra = jnp.einsum('jglcd,gh->jlgdhc', kern.reshape(n_t, gl, tc, SSM_GROUP, SSM_GROUP), eye)
    w_intra = w_intra.reshape(n_t, tc, LANES, LANES)
    eb = jnp.stack([e_re[::-1], e_im[::-1]], 0)
    eb = eb.reshape(2, tc, n_t, gl, SSM_STATE, SSM_GROUP)
    w_state = eb.transpose(2, 1, 3, 5, 0, 4).reshape(n_t, tc, LANES, 2 * SSM_STATE)
    cp_re = c_re[None] * pw_re[1:, :, None, :] - c_im[None] * pw_im[1:, :, None, :]
    cp_im = c_re[None] * pw_im[1:, :, None, :] + c_im[None] * pw_re[1:, :, None, :]
    cp = jnp.stack([cp_re, -cp_im], 0).reshape(2, tc, n_t, gl, SSM_GROUP, SSM_STATE)
    w_read = cp.transpose(2, 1, 0, 5, 3, 4).reshape(n_t, tc, 2 * SSM_STATE, LANES)
    half = gl * SSM_STATE
    a_pow = jnp.concatenate([pw_re[tc].reshape(n_t, 1, half), pw_im[tc].reshape(n_t, 1, half)], -1)
    rp = jnp.arange(2 * SSM_STATE)
    col = jnp.arange(2 * half)
    spread = ((rp[:, None] // SSM_STATE == col[None, :] // half)
              & (rp[:, None] % SSM_STATE == col[None, :] % SSM_STATE)).astype(BF16)
    return (w_intra.astype(BF16), w_state.astype(BF16), w_read.astype(BF16), a_pow, spread, spread.T)


def _s5_kernel(u_ref, wi_ref, wsc_ref, wrc_ref, ap_ref, sp_ref, spt_ref, d_ref, h0r_ref, h0i_ref,
               y_ref, hpr_ref, hpi_ref, hsr_ref, hsi_ref,
               y_acc, v_ref, hs_ref, wt_ref, ws_ref, wr_ref,
               *, nc, n_main, n_meta_chunks, n_seq, seq_chunks):
    tc = S5_CHUNK
    half = hs_ref.shape[1] // 2
    grp_r = lax.broadcasted_iota(jnp.int32, (LANES, 2 * half), 0) // SSM_GROUP
    grp_c = (lax.broadcasted_iota(jnp.int32, (LANES, 2 * half), 1) % half) // SSM_STATE
    for s in range(tc):
        full = _dot(wsc_ref[s], sp_ref[...])
        ws_ref[s * LANES:(s + 1) * LANES, :] = jnp.where(grp_r == grp_c, full, 0.0).astype(BF16)
    grp_r = (lax.broadcasted_iota(jnp.int32, (2 * half, LANES), 0) % half) // SSM_STATE
    grp_c = lax.broadcasted_iota(jnp.int32, (2 * half, LANES), 1) // SSM_GROUP
    for t in range(tc):
        full = _dot(spt_ref[...], wrc_ref[t])
        wr_ref[:, t * LANES:(t + 1) * LANES] = jnp.where(grp_r == grp_c, full, 0.0).astype(BF16)
    for s in range(tc):
        for t in range(tc):
            blk = wi_ref[t - s] if t >= s else jnp.zeros((LANES, LANES), BF16)
            wt_ref[s * LANES:(s + 1) * LANES, t * LANES:(t + 1) * LANES] = blk
    lhs = jnp.concatenate(
        [u_ref[pl.ds(s, nc, stride=tc), :].astype(BF16) for s in range(tc)], axis=1)
    y_acc[...] = _dot(lhs, wt_ref[...])
    v_ref[...] = _dot(lhs, ws_ref[...])
    a_re = ap_ref[:, :half]
    a_im = ap_ref[:, half:]

    def advance(h_re, h_im, v):
        return (a_re * h_re - a_im * h_im + v[:, :half],
                a_re * h_im + a_im * h_re + v[:, half:])

    hs_ref[...] = jnp.zeros(hs_ref.shape, F32)

    h_re = jnp.zeros((1, half), F32)
    h_im = jnp.zeros((1, half), F32)
    for c in range(n_main, n_main + n_meta_chunks):
        hs_ref[pl.ds(c, 1), :] = jnp.concatenate([h_re, h_im], axis=1)
        h_re, h_im = advance(h_re, h_im, v_ref[pl.ds(c, 1), :])

    def body(c, carry):
        h_re, h_im = carry
        hs_ref[pl.ds(c, 1), :] = jnp.concatenate([h_re, h_im], axis=1)
        return advance(h_re, h_im, v_ref[pl.ds(c, 1), :])

    h_re, h_im = lax.fori_loop(0, n_main, body, (h_re, h_im), unroll=8)
    hpr_ref[...] = h_re
    hpi_ref[...] = h_im

    base = n_main + n_meta_chunks
    for b in range(n_seq):
        s_re = h0r_ref[pl.ds(b, 1), :]
        s_im = h0i_ref[pl.ds(b, 1), :]
        for c in range(base + b * seq_chunks, base + (b + 1) * seq_chunks):
            hs_ref[pl.ds(c, 1), :] = jnp.concatenate([s_re, s_im], axis=1)
            s_re, s_im = advance(s_re, s_im, v_ref[pl.ds(c, 1), :])
        hsr_ref[pl.ds(b, 1), :] = s_re
        hsi_ref[pl.ds(b, 1), :] = s_im

    y_acc[...] += _dot(hs_ref[...].astype(BF16), wr_ref[...])
    d = d_ref[...]
    for t in range(tc):
        rows = pl.ds(t, nc, stride=tc)
        y = y_acc[:, t * LANES:(t + 1) * LANES] + d * u_ref[rows, :]
        y_ref[rows, :] = jax.nn.gelu(y)


def _s5(z, weights, d_skip, h0_re, h0_im, seq, n_seq, seq_len):
    tp = z.shape[0]
    tc = S5_CHUNK
    nc = tp // tc
    w_intra, w_state, w_read, a_pow, spread, spread_t = weights
    half = S5_LANE_GROUPS * SSM_STATE
    whole = lambda a: pl.BlockSpec(a.shape, lambda j: (0,) * a.ndim)
    kern = functools.partial(_s5_kernel, nc=nc, n_main=seq // tc, n_meta_chunks=N_META // tc,
                             n_seq=n_seq, seq_chunks=seq_len // tc)
    wspec = lambda a: pl.BlockSpec((None,) + a.shape[1:], lambda j: (j,) + (0,) * (a.ndim - 1))
    col = pl.BlockSpec((tp, LANES), lambda j: (0, j))
    st = lambda r: pl.BlockSpec((r, half), lambda j: (0, j))
    gp = N_SSM_GROUPS * SSM_STATE
    return pl.pallas_call(
        kern,
        grid=(S5_TILES,),
        in_specs=[col, wspec(w_intra), wspec(w_state), wspec(w_read), wspec(a_pow),
                  whole(spread), whole(spread_t),
                  pl.BlockSpec((1, LANES), lambda j: (0, j)), st(n_seq), st(n_seq)],
        out_specs=[col, st(1), st(1), st(n_seq), st(n_seq)],
        out_shape=[jax.ShapeDtypeStruct((tp, D_SSM), F32),
                   jax.ShapeDtypeStruct((1, gp), F32),
                   jax.ShapeDtypeStruct((1, gp), F32),
                   jax.ShapeDtypeStruct((n_seq, gp), F32),
                   jax.ShapeDtypeStruct((n_seq, gp), F32)],
        scratch_shapes=[pltpu.VMEM((nc, tc * LANES), F32),
                        pltpu.VMEM((nc, 2 * half), F32),
                        pltpu.VMEM((nc, 2 * half), F32),
                        pltpu.VMEM((tc * LANES, tc * LANES), BF16),
                        pltpu.VMEM((tc * LANES, 2 * half), BF16),
                        pltpu.VMEM((2 * half, tc * LANES), BF16)],
        compiler_params=_cparams(("parallel",)),
        name="s5_scan",
    )(z, w_intra, w_state, w_read, a_pow, spread, spread_t, d_skip.reshape(1, D_SSM), h0_re, h0_im)


ROW_CHUNKS = D_MODEL // LANES
TOKEN_PITCH = ROW_CHUNKS + 1


def _store_token_major(ref, x, spare_too=False):
    n = x.shape[0]
    for c in range(ROW_CHUNKS):
        ref[pl.ds(c, n, stride=TOKEN_PITCH), :] = x[:, c * LANES:(c + 1) * LANES].astype(ref.dtype)
    if spare_too:
        ref[pl.ds(ROW_CHUNKS, n, stride=TOKEN_PITCH), :] = jnp.zeros((n, LANES), ref.dtype)


def _load_token_major(ref, n, dtype, chunks=range(ROW_CHUNKS)):
    return jnp.concatenate([ref[pl.ds(c, n, stride=TOKEN_PITCH), :].astype(dtype) for c in chunks],
                           axis=1)


def _router_kernel(x_ref, g_ref, w_ref, b_ref, h_ref, e_ref, gate_ref):
    x = x_ref[...]
    ms = jnp.mean(x * x, axis=-1, keepdims=True)
    h = x * lax.rsqrt(ms + EPS) * g_ref[...]
    _store_token_major(h_ref, h, spare_too=True)
    logits = _dot(h.astype(BF16), w_ref[...]) + b_ref[...]
    lane = lax.broadcasted_iota(jnp.int32, logits.shape, 1)
    neg = -jnp.inf
    big = jnp.int32(LANES)

    def first_argmax(vals, vmax):
        return jnp.min(jnp.where(vals == vmax, lane, big), axis=1, keepdims=True)

    lg = jnp.where(lane < N_EGROUPS, logits, neg)
    mg = jnp.max(lg, axis=1, keepdims=True)
    sg = jnp.sum(jnp.exp(lg - mg), axis=1, keepdims=True)
    g_w = 1.0 / sg
    g_idx = first_argmax(lg, mg)
    lo = N_EGROUPS + EXPERTS_PER_GROUP * g_idx
    le = jnp.where((lane >= lo) & (lane < lo + EXPERTS_PER_GROUP), logits, neg)
    m1 = jnp.max(le, axis=1, keepdims=True)
    se = jnp.sum(jnp.exp(le - m1), axis=1, keepdims=True)
    i1 = first_argmax(le, m1)
    le2 = jnp.where(lane == i1, neg, le)
    m2 = jnp.max(le2, axis=1, keepdims=True)
    i2 = first_argmax(le2, m2)
    p1 = 1.0 / se
    p2 = jnp.exp(m2 - m1) / se
    tot = p1 + p2
    w1 = g_w * (p1 / tot)
    w2 = g_w * (p2 / tot)
    e_ref[...] = jnp.where(lane == 0, i1 - N_EGROUPS, jnp.where(lane == 1, i2 - N_EGROUPS, 0))
    gate_ref[...] = jnp.where(lane == 0, w1, jnp.where(lane == 1, w2, 0.0))


def _router(x2, g, w_r, b_r, tr):
    tp, d = x2.shape
    return pl.pallas_call(
        _router_kernel,
        grid=(tp // tr,),
        in_specs=[pl.BlockSpec((tr, d), lambda i: (i, 0)),
                  pl.BlockSpec((1, d), lambda i: (0, 0)),
                  pl.BlockSpec((d, LANES), lambda i: (0, 0)),
                  pl.BlockSpec((1, LANES), lambda i: (0, 0))],
        out_specs=[pl.BlockSpec((tr * TOKEN_PITCH, LANES), lambda i: (i, 0)),
                   pl.BlockSpec((tr, LANES), lambda i: (i, 0)),
                   pl.BlockSpec((tr, LANES), lambda i: (i, 0))],
        out_shape=[jax.ShapeDtypeStruct((tp * TOKEN_PITCH, LANES), F32),
                   jax.ShapeDtypeStruct((tp, LANES), jnp.int32),
                   jax.ShapeDtypeStruct((tp, LANES), F32)],
        compiler_params=_cparams(("parallel",)),
        name="norm2_router",
    )(x2, g.reshape(1, d), w_r, b_r)


def _expert_kernel(be_ref, nu_ref, first_ref, cnt_ref, tok_ref, dst_ref, h_hbm, w1_ref, w3_ref, w2_ref,
                   y_hbm, xbuf, xb16, acc_ref, ybuf, gsem, ssem, *, nb, plane_rows, plane_pad):
    del be_ref
    b = pl.program_id(0)
    hh = pl.program_id(1)
    last = pl.num_programs(1) - 1
    n_used = nu_ref[0]
    active = b < n_used
    slot = b % 2
    rows = xb16.shape[1]
    rc = ROW_CHUNKS
    pitch = TOKEN_PITCH
    spare_row = TOP_K * plane_rows

    def token_rows(ref, t):
        return ref.at[pl.ds(t * pitch, pitch), :]

    def start_gather(blk, s):
        base = first_ref[blk]
        for r in range(rows):
            pltpu.make_async_copy(token_rows(h_hbm, tok_ref[base + r]),
                                  xbuf.at[s, pl.ds(r * pitch, pitch), :], gsem.at[s]).start()

    def all_rows_gathered(s):
        return pltpu.make_async_copy(h_hbm.at[pl.ds(0, rows * pitch), :], xbuf.at[s], gsem.at[s])

    def all_rows_scattered():
        return pltpu.make_async_copy(ybuf, y_hbm.at[pl.ds(spare_row * pitch, rows * pitch), :], ssem)

    @pl.when((b == 0) & (hh == 0))
    def _():
        ybuf[...] = jnp.zeros(ybuf.shape, F32)
        fills = [all_rows_scattered()]
        if plane_pad:
            fills += [pltpu.make_async_copy(
                ybuf.at[pl.ds(0, plane_pad * pitch), :],
                y_hbm.at[pl.ds(((k + 1) * plane_rows - plane_pad) * pitch, plane_pad * pitch), :], ssem)
                for k in range(TOP_K)]
        for f in fills:
            f.start()
        for f in fills:
            f.wait()

        @pl.when(active)
        def _():
            start_gather(0, 0)

    @pl.when(active & (hh == 0))
    def _():
        all_rows_gathered(slot).wait()
        per = rc // EXPERT_SLICES
        for q in range(EXPERT_SLICES):
            xb16[q] = _load_token_major(xbuf.at[slot], rows, BF16, range(q * per, (q + 1) * per))

    @pl.when(active & (hh == 1) & (b + 1 < n_used))
    def _():
        start_gather(jnp.minimum(b + 1, nb - 1), 1 - slot)

    @pl.when(active)
    def _():
        w13 = jnp.concatenate([w1_ref[...].astype(BF16), w3_ref[...].astype(BF16)], axis=1)
        part = _dot(xb16[_expert_slice(b, hh)], w13)

        @pl.when(hh == 0)
        def _():
            acc_ref[...] = part

        @pl.when(hh > 0)
        def _():
            acc_ref[...] += part

    @pl.when(active & (hh == last))
    def _():
        @pl.when(b > 0)
        def _():
            all_rows_scattered().wait()

        ac = acc_ref[...]
        hid = (jax.nn.silu(ac[:, :D_EXPERT]) * ac[:, D_EXPERT:]).astype(BF16)
        _store_token_major(ybuf, _dot(hid, w2_ref[...].astype(BF16)))
        base = first_ref[b]
        n_real = cnt_ref[b]
        for r in range(rows):
            dst = jnp.where(r < n_real, dst_ref[base + r], spare_row + r)
            pltpu.make_async_copy(ybuf.at[pl.ds(r * pitch, pitch), :], token_rows(y_hbm, dst),
                                  ssem).start(priority=r % 2)

        @pl.when(b + 1 >= n_used)
        def _():
            all_rows_scattered().wait()


def _expert_slice(b, hh):
    return jnp.where(b % 2 == 0, hh, EXPERT_SLICES - 1 - hh)


def _experts(h2, plan, t_real, w1, w3, w2):
    tp = h2.shape[0] // TOKEN_PITCH
    assert tp - t_real <= MOE_BLOCK
    block_expert, n_used, first, cnt, tok_sorted, dst_sorted = plan
    d = D_MODEL
    nb = block_expert.shape[0]
    dk = d // EXPERT_SLICES

    def eidx(b, be, nu):
        return be[jnp.minimum(b, nu[0] - 1)]

    def sidx(b, hh, nu):
        live = b < nu[0]
        return _expert_slice(jnp.minimum(b, nu[0] - 1), jnp.where(live, hh, EXPERT_SLICES - 1))

    grid_spec = pltpu.PrefetchScalarGridSpec(
        num_scalar_prefetch=6,
        grid=(nb, EXPERT_SLICES),
        in_specs=[pl.BlockSpec(memory_space=pl.ANY),
                  pl.BlockSpec((None, dk, D_EXPERT), lambda b, hh, be, nu, *_: (eidx(b, be, nu), sidx(b, hh, nu), 0)),
                  pl.BlockSpec((None, dk, D_EXPERT), lambda b, hh, be, nu, *_: (eidx(b, be, nu), sidx(b, hh, nu), 0)),
                  pl.BlockSpec((None, D_EXPERT, d), lambda b, hh, be, nu, *_: (eidx(b, be, nu), 0, 0))],
        out_specs=pl.BlockSpec(memory_space=pl.ANY),
        scratch_shapes=[pltpu.VMEM((2, MOE_BLOCK * TOKEN_PITCH, LANES), F32),
                        pltpu.VMEM((EXPERT_SLICES, MOE_BLOCK, dk), BF16),
                        pltpu.VMEM((MOE_BLOCK, 2 * D_EXPERT), F32),
                        pltpu.VMEM((MOE_BLOCK * TOKEN_PITCH, LANES), F32),
                        pltpu.SemaphoreType.DMA((2,)),
                        pltpu.SemaphoreType.DMA(())],
    )
    return pl.pallas_call(
        functools.partial(_expert_kernel, nb=nb, plane_rows=tp, plane_pad=tp - t_real),
        grid_spec=grid_spec,
        out_shape=jax.ShapeDtypeStruct(((TOP_K * tp + MOE_BLOCK) * TOKEN_PITCH, LANES), F32),
        compiler_params=_cparams(("arbitrary", "arbitrary"), 60 * 1024 * 1024),
        name="expert_mlp",
    )(block_expert, n_used, first, cnt, tok_sorted, dst_sorted, h2, w1, w3, w2)


def _route_plan(expert, t_real, tp):
    s = t_real * TOP_K
    n_blocks = -(-(s + N_EXPERTS * (MOE_BLOCK - 1)) // MOE_BLOCK)
    flat_e = expert.reshape(-1).astype(jnp.int32)
    se, order = lax.sort((flat_e, jnp.arange(s, dtype=jnp.int32)), num_keys=1, is_stable=True)
    bounds = jnp.searchsorted(se, jnp.arange(N_EXPERTS + 1, dtype=jnp.int32)).astype(jnp.int32)
    start = bounds[:-1]
    counts = bounds[1:] - start
    padded = (counts + MOE_BLOCK - 1) // MOE_BLOCK * MOE_BLOCK
    pad_end = jnp.cumsum(padded)
    pad_start = pad_end - padded
    block_start = jnp.arange(n_blocks, dtype=jnp.int32) * MOE_BLOCK
    block_expert = jnp.minimum(jnp.searchsorted(pad_end, block_start, side='right'),
                               N_EXPERTS - 1).astype(jnp.int32)
    n_used = (pad_end[-1] // MOE_BLOCK).astype(jnp.int32).reshape(1)
    cnt = jnp.clip(counts[block_expert] - (block_start - pad_start[block_expert]), 0, MOE_BLOCK)
    first = jnp.clip(start[block_expert] + block_start - pad_start[block_expert], 0, s).astype(jnp.int32)
    tail = jnp.zeros((MOE_BLOCK,), jnp.int32)
    tok = order // TOP_K
    tok_sorted = jnp.concatenate([tok, tail])
    dst_sorted = jnp.concatenate([(order % TOP_K) * tp + tok, tail])
    return block_expert, n_used, first, cnt.astype(jnp.int32), tok_sorted, dst_sorted


def _combine_kernel(x_ref, g_ref, y0_ref, y1_ref, o_ref):
    n = x_ref.shape[0]
    g = g_ref[...]
    y = (g[:, 0:1] * _load_token_major(y0_ref, n, F32)
         + g[:, 1:2] * _load_token_major(y1_ref, n, F32))
    o_ref[...] = x_ref[...] + y


def _combine(x2, gates, y2, row_off, n_rows, tile):
    tp, d = x2.shape
    off = row_off // tile
    plane = tp // tile
    return pl.pallas_call(
        _combine_kernel,
        grid=(n_rows // tile,),
        in_specs=[pl.BlockSpec((tile, d), lambda i: (off + i, 0)),
                  pl.BlockSpec((tile, LANES), lambda i: (off + i, 0)),
                  pl.BlockSpec((tile * TOKEN_PITCH, LANES), lambda i: (off + i, 0)),
                  pl.BlockSpec((tile * TOKEN_PITCH, LANES), lambda i: (plane + off + i, 0))],
        out_specs=pl.BlockSpec((tile, d), lambda i: (i, 0)),
        out_shape=jax.ShapeDtypeStruct((n_rows, d), F32),
        compiler_params=_cparams(("parallel",)),
        name="moe_combine",
    )(x2, gates, y2, y2)


def kernel(x_prompt, x_sample, cache_k, cache_v, state_ssm_re, state_ssm_im, meta_tokens, norm1_g, w_in, b_in, ssm_a_re, ssm_a_im, ssm_log_dt, ssm_b_re, ssm_b_im, ssm_c_re, ssm_c_im, ssm_d, w_glu, b_glu, w_ssm_proj, q_norm_g, k_norm_g, lam_q1, lam_k1, lam_q2, lam_k2, subln_g, w_att_proj, w_o, norm2_g, w_router_group, b_router_group, w_router_expert, b_router_expert, w1_e, w3_e, w2_e):
    assert x_prompt.shape[0] == 1 and w_in.shape[0] == 1
    seq = x_prompt.shape[1]
    nb, nq = x_sample.shape[0], x_sample.shape[1]
    past = cache_k.shape[2]
    n_s = nb * nq
    t_real = seq + N_META + n_s
    tp = -(-t_real // ROW_ALIGN) * ROW_ALIGN
    off_meta, off_s = seq, seq + N_META
    tq = 256
    assert seq % tq == 0 and nq == N_META and past % 512 == 0
    lam_init = 0.8 - 0.6 * math.exp(-0.3 * 0)
    out_scale = 1.0 - lam_init

    x_tail = jnp.concatenate([meta_tokens.astype(F32), x_sample.reshape(n_s, D_MODEL),
                              jnp.zeros((tp - t_real, D_MODEL), F32)], axis=0)

    tm = _row_tile(tp, 1088)
    tr = _row_tile(tp, 256)
    assert seq % tr == 0

    h1, x_cat = _rmsnorm(x_prompt[0], x_tail, norm1_g[0], tr)
    z = _inproj(h1, w_in[0], b_in[0], tm, 512)

    pos = jnp.concatenate([N_META + jnp.arange(seq), jnp.arange(N_META),
                           jnp.tile(past + jnp.arange(nq), nb),
                           jnp.zeros((tp - t_real,), jnp.int32)]).astype(F32)
    half = HEAD_DIM // 2
    inv = ROPE_THETA ** (-jnp.arange(half, dtype=F32) / half)
    ang = pos[:, None] * inv[None, :]
    cos_t = jnp.tile(jnp.cos(ang), (1, LANES // half))
    sin_h = jnp.sin(ang)
    sin_t = jnp.tile(jnp.concatenate([-sin_h, sin_h], axis=1), (1, LANES // HEAD_DIM))
    gq = jnp.tile(q_norm_g[0], LANES // HEAD_DIM).reshape(1, LANES)
    gk = jnp.tile(k_norm_g[0], LANES // HEAD_DIM).reshape(1, LANES)
    qb, kf, kb, vf, vb = _qk_rope(z, cos_t, sin_t, gq, gk, tr)

    lam = (jnp.exp(jnp.sum(lam_q1[0] * lam_k1[0])) - jnp.exp(jnp.sum(lam_q2[0] * lam_k2[0])) + lam_init)
    lam_row = jnp.full((1, LANES), lam, F32)
    sg = subln_g[0].reshape(1, LANES)

    q_s = qb[off_s:off_s + n_s].reshape(nb, nq, N_HEADS, LANES).transpose(0, 2, 1, 3)
    lane = jnp.arange(LANES)
    qz = jnp.concatenate([jnp.where(lane < HEAD_DIM, q_s, 0), jnp.where(lane >= HEAD_DIM, q_s, 0)], axis=2)
    pad_new = lambda a: jnp.pad(a[off_s:off_s + n_s].reshape(nb, nq, D_ATT), ((0, 0), (0, LANES - nq), (0, 0)))
    ck = cache_k[0].reshape(nb, past, D_ATT)
    cv = cache_v[0].reshape(nb, past * N_HEADS, V_DIM)

    logit_bound = 8.1 * jnp.max(jnp.abs(q_norm_g[0])) * jnp.max(jnp.abs(k_norm_g[0]))

    def attention(fast):
        def run():
            if fast:
                o = _attn_prompt_fast(qb, kb, vb, lam_row, sg, seq, out_scale, 512)
            else:
                o = _attn_prompt(qb, kb, vb, lam_row, sg, seq, out_scale, tq)
            return _attn_sample(qz, ck, cv, pad_new(kb), pad_new(vb), o, lam_row, sg, off_s, out_scale,
                                512, fast)
        return run

    o_att = lax.cond(logit_bound <= LOGIT_BOUND_MAX, attention(True), attention(False))

    gp = N_SSM_GROUPS * SSM_STATE
    s5w = _s5_weights(ssm_a_re[0], ssm_a_im[0], ssm_log_dt[0], ssm_b_re[0], ssm_b_im[0],
                      ssm_c_re[0], ssm_c_im[0])
    ys, hp_re, hp_im, hs_re, hs_im = _s5(z, s5w, ssm_d[0], state_ssm_re[0].reshape(nb, gp),
                                         state_ssm_im[0].reshape(nb, gp), seq, nb, nq)
    ysg = _glu(ys, w_glu[0], b_glu[0], tm, 512)
    m = _merge(ysg, o_att, w_ssm_proj[0], w_att_proj[0], z, tm, 512)
    x2 = _outproj(m, w_o[0], x_cat, tm, 512)

    w_r = jnp.concatenate([w_router_group[0], w_router_expert[0],
                           jnp.zeros((D_MODEL, LANES - N_EGROUPS - N_EXPERTS), F32)], axis=1).astype(BF16)
    b_r = jnp.concatenate([b_router_group[0], b_router_expert[0],
                           jnp.zeros((LANES - N_EGROUPS - N_EXPERTS,), F32)]).reshape(1, LANES)
    h2, e_sel, g_sel = _router(x2, norm2_g[0], w_r, b_r, tr)

    plan = _route_plan(e_sel[:t_real, :TOP_K], t_real, tp)
    y2 = _experts(h2, plan, t_real, w1_e[0], w3_e[0], w2_e[0])

    def heads(a, lead):
        return a.reshape(lead + (N_HEADS, 2, HEAD_DIM))

    y_prompt = _combine(x2, g_sel, y2, 0, seq, LANES).reshape(1, seq, D_MODEL)
    y_sample = _combine(x2, g_sel, y2, off_s, n_s, nq).reshape(nb, nq, D_MODEL)
    k_p = jnp.concatenate([kf[off_meta:off_meta + N_META], kf[:seq]], axis=0)
    v_p = jnp.concatenate([vf[off_meta:off_meta + N_META], vf[:seq]], axis=0)
    k_prompt = heads(k_p, (1, 1, seq + N_META))
    v_prompt = v_p.reshape(1, 1, seq + N_META, N_HEADS, V_DIM)
    k_sample = heads(kf[off_s:off_s + n_s], (1, nb, nq))
    v_sample = vf[off_s:off_s + n_s].reshape(1, nb, nq, N_HEADS, V_DIM)
    st = lambda a, lead: a.reshape(lead + (N_SSM_GROUPS, SSM_STATE))
    return (y_prompt, y_sample, k_prompt, v_prompt, st(hp_re, (1, 1)), st(hp_im, (1, 1)),
            k_sample, v_sample, st(hs_re, (1, nb)), st(hs_im, (1, nb)))
```
